```python
import math
import jax, jax.numpy as jnp
from jax import lax
import numpy as np


D_MODEL = 1024
BATCH = 32
SEQ = 2048
DEPTH = 1

CHUNK = 64
A_HEADS = 8
A_LATENT = 128
IDX_HEADS = 8
IDX_DIM = 64
TOPK_KEYS_MAX = 256
A_QBLOCK = 64
REL_BUCKETS = 32
REL_MAX_DIST = 128
SSM_D_INNER = 1024
SSM_HEADDIM = 64
SSM_HEADS = SSM_D_INNER // SSM_HEADDIM
SSM_GROUPS = 4
SSM_STATE = 128
SSM_CONV = 4
SSM_CONV_DIM = SSM_D_INNER + 2 * SSM_GROUPS * SSM_STATE
DT_MIN = 0.001
DT_MAX = 0.1
N_EXPERTS = 32
TOPK_EXPERTS = 4
D_FF_EXPERT = D_MODEL
SWIGLU_LIMIT = 7.0
SWIGLU_ALPHA = 1.702
MOE_BLOCK = 256
DEEPNORM_ALPHA = (2.0 * DEPTH) ** 0.25
DEEPNORM_BETA = (8.0 * DEPTH) ** -0.25
LN_EPS = 1e-5
IN_SPLITS = (A_HEADS * A_LATENT, A_LATENT, IDX_HEADS * IDX_DIM, IDX_DIM, IDX_HEADS,
             SSM_D_INNER, SSM_CONV_DIM, SSM_HEADS, D_MODEL, D_MODEL)
D_IN_PROJ = sum(IN_SPLITS)

kernel_name = "hybrid_dsa_ssd_moe_streaming_block"


def layer_norm(x, eps=LN_EPS):
    x32 = x.astype(jnp.float32)
    mu = jnp.mean(x32, -1, keepdims=True)
    var = jnp.mean(jnp.square(x32 - mu), -1, keepdims=True)
    return ((x32 - mu) * lax.rsqrt(var + eps)).astype(x.dtype)


def rms_norm(x, w, eps=LN_EPS):
    x32 = x.astype(jnp.float32)
    y = x32 * lax.rsqrt(jnp.mean(jnp.square(x32), -1, keepdims=True) + eps)
    return y.astype(x.dtype) * w


def t5_bucket(rel):
    half = REL_BUCKETS // 2
    max_exact = half // 2
    ret = (rel > 0).astype(jnp.int32) * half
    n = jnp.abs(rel)
    nf = jnp.maximum(n, 1).astype(jnp.float32)
    large = max_exact + (jnp.log(nf / max_exact) / math.log(REL_MAX_DIST / max_exact)
                         * (half - max_exact)).astype(jnp.int32)
    large = jnp.minimum(large, half - 1)
    return ret + jnp.where(n < max_exact, n, large)


def sparse_indexed_attention(q, kv, iq, ik, iw, rel_bias):
    bsz, seq = kv.shape[0], kv.shape[1]
    n_sel = min(TOPK_KEYS_MAX, seq // 4)
    n_blk = seq // A_QBLOCK
    key_chunk = jnp.arange(seq, dtype=jnp.int32) // CHUNK
    q_pos = jnp.arange(seq, dtype=jnp.int32).reshape(n_blk, A_QBLOCK)

    def blocks(a):
        return jnp.swapaxes(a.reshape(bsz, n_blk, A_QBLOCK, *a.shape[2:]), 0, 1)

    gather_keys = jax.vmap(lambda a, i: a[i])

    def one_block(args):
        qb, iqb, iwb, pos = args
        q_chunk = pos // CHUNK
        idx = jnp.einsum('bqhd,bsd->bqhs', iqb, ik) * IDX_DIM ** -0.5
        score = jnp.einsum('bqhs,bqh->bqs', jax.nn.relu(idx), iwb).astype(jnp.float32)
        admissible = key_chunk[None, :] <= q_chunk[:, None]
        score = jnp.where(admissible[None], score, -jnp.inf)
        _, sel = lax.top_k(score, n_sel)
        kv_sel = gather_keys(kv, sel)
        logits = jnp.einsum('bqhc,bqkc->bqhk', qb, kv_sel).astype(jnp.float32) * A_LATENT ** -0.5
        bias = rel_bias[t5_bucket(sel - pos[None, :, None])].astype(jnp.float32)
        logits = logits + jnp.transpose(bias, (0, 1, 3, 2))
        valid = (sel // CHUNK) <= q_chunk[None, :, None]
        logits = jnp.where(valid[:, :, None, :], logits, -jnp.inf)
        probs = jax.nn.softmax(logits, axis=-1).astype(kv.dtype)
        return jnp.einsum('bqhk,bqkc->bqhc', probs, kv_sel)

    out = lax.map(one_block, (blocks(q), blocks(iq), blocks(iw), q_pos))
    return jnp.swapaxes(out, 0, 1).reshape(bsz, seq, A_HEADS * A_LATENT)


def segsum(a):
    t = a.shape[-1]
    cs = jnp.cumsum(a, axis=-1)
    d = cs[..., :, None] - cs[..., None, :]
    return jnp.where(jnp.tril(jnp.ones((t, t), dtype=bool)), d, -jnp.inf)


def ssd_mixer(z, xbc, dt_raw, conv_w, conv_b, dt_bias, a_log, d_skip, norm_w):
    bsz, seq = xbc.shape[0], xbc.shape[1]
    n_ch = seq // CHUNK
    hpg = SSM_HEADS // SSM_GROUPS
    xbc = lax.conv_general_dilated(xbc, conv_w[:, None, :], window_strides=(1,),
                                   padding=((SSM_CONV - 1, 0),),
                                   dimension_numbers=('NWC', 'WIO', 'NWC'),
                                   feature_group_count=SSM_CONV_DIM) + conv_b
    xbc = jax.nn.silu(xbc)
    xs, bm, cm = jnp.split(xbc, [SSM_D_INNER, SSM_D_INNER + SSM_GROUPS * SSM_STATE], axis=-1)
    xs = xs.reshape(bsz, n_ch, CHUNK, SSM_GROUPS, hpg, SSM_HEADDIM)
    bm = bm.reshape(bsz, n_ch, CHUNK, SSM_GROUPS, SSM_STATE)
    cm = cm.reshape(bsz, n_ch, CHUNK, SSM_GROUPS, SSM_STATE)
    dt = jax.nn.softplus((dt_raw + dt_bias).astype(jnp.float32))
    a = -jnp.exp(a_log.astype(jnp.float32))
    dt = dt.reshape(bsz, n_ch, CHUNK, SSM_GROUPS, hpg)
    a_dt = jnp.transpose(dt * a.reshape(SSM_GROUPS, hpg), (0, 3, 4, 1, 2))
    xdt = xs * dt[..., None]
    a_cs = jnp.cumsum(a_dt, axis=-1)
    lmat = jnp.exp(segsum(a_dt))
    cb = jnp.einsum('bclgn,bcsgn->bcgls', cm, bm)
    y_diag = jnp.einsum('bcgls,bgrcls,bcsgrp->bclgrp', cb, lmat, xdt)
    decay_states = jnp.exp(a_cs[..., -1:] - a_cs)
    states = jnp.einsum('bclgn,bgrcl,bclgrp->bcgrpn', bm, decay_states, xdt)
    chunk_decay = jnp.exp(a_cs[..., -1])

    def step(h, inp):
        st, dec = inp
        return h * dec[..., None, None] + st, h

    h0 = jnp.zeros((bsz, SSM_GROUPS, hpg, SSM_HEADDIM, SSM_STATE), states.dtype)
    _, prev = lax.scan(step, h0, (jnp.swapaxes(states, 0, 1), jnp.moveaxis(chunk_decay, -1, 0)))
    prev = jnp.swapaxes(prev, 0, 1)
    y_off = jnp.einsum('bclgn,bcgrpn,bgrcl->bclgrp', cm, prev, jnp.exp(a_cs))
    y = y_diag + y_off + xs * d_skip.reshape(SSM_GROUPS, hpg)[:, :, None]
    y = y.reshape(bsz, seq, SSM_D_INNER) * jax.nn.silu(z)
    yg = y.reshape(bsz, seq, SSM_GROUPS, SSM_D_INNER // SSM_GROUPS).astype(jnp.float32)
    yg = yg * lax.rsqrt(jnp.mean(jnp.square(yg), -1, keepdims=True) + LN_EPS)
    return yg.reshape(bsz, seq, SSM_D_INNER).astype(z.dtype) * norm_w


def moe_ffn(u, w_router, b_router, w1, b1, w2, b2):
    bsz, seq, d = u.shape
    n_tok = bsz * seq
    n_asg = n_tok * TOPK_EXPERTS
    xt = u.reshape(n_tok, d)
    logits = (xt @ w_router + b_router).astype(jnp.float32)
    top_val, top_idx = lax.top_k(logits, TOPK_EXPERTS)
    gates = jax.nn.softmax(top_val, axis=-1).astype(u.dtype)
    e_flat = top_idx.reshape(n_asg)
    tok_flat = jnp.arange(n_asg, dtype=jnp.int32) // TOPK_EXPERTS
    g_flat = gates.reshape(n_asg)
    order = jnp.argsort(e_flat)
    e_sorted = e_flat[order]
    counts = jax.ops.segment_sum(jnp.ones((n_asg,), jnp.int32), e_flat, num_segments=N_EXPERTS)
    starts = jnp.cumsum(counts) - counts
    padded = (counts + MOE_BLOCK - 1) // MOE_BLOCK * MOE_BLOCK
    pends = jnp.cumsum(padded)
    pstarts = pends - padded
    dest = pstarts[e_sorted] + jnp.arange(n_asg, dtype=jnp.int32) - starts[e_sorted]
    n_blocks = n_asg // MOE_BLOCK + N_EXPERTS
    n_slots = n_blocks * MOE_BLOCK
    slot_tok = jnp.zeros((n_slots,), jnp.int32).at[dest].set(tok_flat[order])
    slot_gate = jnp.zeros((n_slots,), u.dtype).at[dest].set(g_flat[order])
    block_start = jnp.arange(n_blocks, dtype=jnp.int32) * MOE_BLOCK
    block_expert = jnp.minimum(jnp.sum(block_start[:, None] >= pends[None, :], axis=1), N_EXPERTS - 1)

    def expert_block(acc, inp):
        tok, g, e = inp
        h = xt[tok] @ w1[e] + b1[e]
        gate, up = h[:, :D_FF_EXPERT], h[:, D_FF_EXPERT:]
        gate = jnp.minimum(gate, SWIGLU_LIMIT)
        up = jnp.clip(up, -SWIGLU_LIMIT, SWIGLU_LIMIT)
        act = (up + 1.0) * gate * jax.nn.sigmoid(SWIGLU_ALPHA * gate)
        y = act @ w2[e] + b2[e]
        return acc.at[tok].add(y * g[:, None]), None

    acc, _ = lax.scan(expert_block, jnp.zeros((n_tok, d), xt.dtype),
                      (slot_tok.reshape(n_blocks, MOE_BLOCK), slot_gate.reshape(n_blocks, MOE_BLOCK),
                       block_expert))
    return acc.reshape(bsz, seq, d)


def setup_inputs(seed: int = 0) -> dict:
    key = jax.random.key(seed)
    ks = jax.random.split(key, 28)

    def nrm(k, shape, scale):
        return jax.random.normal(k, shape, jnp.float32) * scale

    u_dt = jax.random.uniform(ks[11], (DEPTH, SSM_HEADS), jnp.float32)
    dt0 = jnp.exp(u_dt * (math.log(DT_MAX) - math.log(DT_MIN)) + math.log(DT_MIN))
    return {
        'x': nrm(ks[0], (BATCH, SEQ, D_MODEL), 1.0),
        'c': nrm(ks[1], (BATCH, D_MODEL), 1.0),
        'w_mod': nrm(ks[2], (DEPTH, D_MODEL, 6 * D_MODEL), 0.5 * D_MODEL ** -0.5),
        'b_mod': nrm(ks[3], (DEPTH, 6 * D_MODEL), 0.02),
        'w_in': nrm(ks[4], (DEPTH, D_MODEL, D_IN_PROJ), D_MODEL ** -0.5),
        'kv_norm_w': 1.0 + nrm(ks[5], (DEPTH, A_LATENT), 0.1),
        'idx_k_norm_w': 1.0 + nrm(ks[6], (DEPTH, IDX_DIM), 0.1),
        'idx_k_norm_b': nrm(ks[7], (DEPTH, IDX_DIM), 0.02),
        'rel_bias': nrm(ks[8], (REL_BUCKETS, A_HEADS), 0.5),
        'conv_w': nrm(ks[9], (DEPTH, SSM_CONV, SSM_CONV_DIM), SSM_CONV ** -0.5),
        'conv_b': nrm(ks[10], (DEPTH, SSM_CONV_DIM), 0.02),
        'dt_bias': dt0 + jnp.log(-jnp.expm1(-dt0)),
        'a_log': jnp.log(jax.random.uniform(ks[12], (DEPTH, SSM_HEADS), jnp.float32, 1.0, 16.0)),
        'd_skip': 1.0 + nrm(ks[13], (DEPTH, SSM_HEADS), 0.1),
        'ssm_norm_w': 1.0 + nrm(ks[14], (DEPTH, SSM_D_INNER), 0.1),
        'w_proj_a': nrm(ks[15], (DEPTH, A_HEADS * A_LATENT, D_MODEL), (A_HEADS * A_LATENT) ** -0.5),
        'w_proj_b': nrm(ks[16], (DEPTH, SSM_D_INNER, D_MODEL), SSM_D_INNER ** -0.5),
        'w_out': nrm(ks[17], (DEPTH, D_MODEL, D_MODEL), DEEPNORM_BETA * D_MODEL ** -0.5),
        'ln1_g': 1.0 + nrm(ks[18], (DEPTH, D_MODEL), 0.1),
        'ln1_b': nrm(ks[19], (DEPTH, D_MODEL), 0.02),
        'w_router': nrm(ks[20], (DEPTH, D_MODEL, N_EXPERTS), D_MODEL ** -0.5),
        'b_router': nrm(ks[21], (DEPTH, N_EXPERTS), 0.01),
        'w1': nrm(ks[22], (DEPTH, N_EXPERTS, D_MODEL, 2 * D_FF_EXPERT), D_MODEL ** -0.5),
        'b1': nrm(ks[23], (DEPTH, N_EXPERTS, 2 * D_FF_EXPERT), 0.02),
        'w2': nrm(ks[24], (DEPTH, N_EXPERTS, D_FF_EXPERT, D_MODEL), DEEPNORM_BETA * D_FF_EXPERT ** -0.5),
        'b2': nrm(ks[25], (DEPTH, N_EXPERTS, D_MODEL), 0.02),
        'ln2_g': 1.0 + nrm(ks[26], (DEPTH, D_MODEL), 0.1),
        'ln2_b': nrm(ks[27], (DEPTH, D_MODEL), 0.02),
    }


def reference(x, c, w_mod, b_mod, w_in, kv_norm_w, idx_k_norm_w, idx_k_norm_b, rel_bias,
              conv_w, conv_b, dt_bias, a_log, d_skip, ssm_norm_w, w_proj_a, w_proj_b, w_out,
              ln1_g, ln1_b, w_router, b_router, w1, b1, w2, b2, ln2_g, ln2_b):
    bsz, seq, _ = x.shape
    split_at = np.cumsum(IN_SPLITS)[:-1].tolist()
    for l in range(DEPTH):
        mod = jax.nn.silu(c) @ w_mod[l] + b_mod[l]
        shift1, scale1, gate1, shift2, scale2, gate2 = [m[:, None, :] for m in jnp.split(mod, 6, axis=-1)]
        u = layer_norm(x) * (1.0 + scale1) + shift1
        proj = u @ w_in[l]
        q_a, kv_a, iq, ik, iw, z, xbc, dt_raw, g_a, g_b = jnp.split(proj, split_at, axis=-1)
        q_a = q_a.reshape(bsz, seq, A_HEADS, A_LATENT)
        kv_a = rms_norm(kv_a, kv_norm_w[l])
        iq = iq.reshape(bsz, seq, IDX_HEADS, IDX_DIM)
        ik = layer_norm(ik) * idx_k_norm_w[l] + idx_k_norm_b[l]
        iw = iw * IDX_HEADS ** -0.5
        o_a = sparse_indexed_attention(q_a, kv_a, iq, ik, iw, rel_bias)
        o_b = ssd_mixer(z, xbc, dt_raw, conv_w[l], conv_b[l], dt_bias[l], a_log[l],
                        d_skip[l], ssm_norm_w[l])
        merged = jax.nn.sigmoid(g_a) * (o_a @ w_proj_a[l]) + jax.nn.sigmoid(g_b) * (o_b @ w_proj_b[l])
        x = layer_norm(DEEPNORM_ALPHA * x + gate1 * (merged @ w_out[l])) * ln1_g[l] + ln1_b[l]
        u2 = layer_norm(x) * (1.0 + scale2) + shift2
        y = moe_ffn(u2, w_router[l], b_router[l], w1[l], b1[l], w2[l], b2[l])
        x = layer_norm(DEEPNORM_ALPHA * x + gate2 * y) * ln2_g[l] + ln2_b[l]
    return x
```

```python
import functools
import math

import jax
import jax.numpy as jnp
import numpy as np
from jax import lax
from jax.experimental import pallas as pl
from jax.experimental.pallas import tpu as pltpu

F32 = jnp.float32
BF16 = jnp.bfloat16
I32 = jnp.int32

CHUNK = 64
A_HEADS = 8
A_LATENT = 128
IDX_HEADS = 8
IDX_DIM = 64
TOPK_KEYS_MAX = 256
REL_BUCKETS = 32
REL_MAX_DIST = 128
SSM_D_INNER = 1024
SSM_HEADDIM = 64
SSM_HEADS = SSM_D_INNER // SSM_HEADDIM
SSM_GROUPS = 4
SSM_STATE = 128
SSM_CONV = 4
SSM_CONV_DIM = SSM_D_INNER + 2 * SSM_GROUPS * SSM_STATE
N_EXPERTS = 32
TOPK_EXPERTS = 4
SWIGLU_LIMIT = 7.0
SWIGLU_ALPHA = 1.702
LN_EPS = 1e-5

LANES = 128
INT_MIN = -2147483648
VMEM_LIMIT = 56 * 1024 * 1024

HI = lax.Precision.HIGHEST


def _cparams(sem):
    return pltpu.CompilerParams(dimension_semantics=sem, vmem_limit_bytes=VMEM_LIMIT)


def _ln(x):
    mu = jnp.mean(x, axis=-1, keepdims=True)
    xc = x - mu
    var = jnp.mean(xc * xc, axis=-1, keepdims=True)
    return xc * lax.rsqrt(var + LN_EPS)


def _const_spec(shape):
    nd = len(shape)
    return pl.BlockSpec(shape, lambda *_: (0,) * nd, pipeline_mode=pl.Buffered(1))


def _mod_kernel(c_ref, w_ref, b_ref, o_ref):
    c = c_ref[...]
    sc = c * jax.nn.sigmoid(c)
    o_ref[...] = jnp.dot(sc, w_ref[...], precision=HI, preferred_element_type=F32) + b_ref[...]


def _mod_call(c, w_mod, b_mod):
    bsz, d = c.shape
    n = w_mod.shape[1]
    tn = 1024
    return pl.pallas_call(
        _mod_kernel,
        grid=(n // tn,),
        in_specs=[pl.BlockSpec((bsz, d), lambda j: (0, 0)),
                  pl.BlockSpec((d, tn), lambda j: (0, j)),
                  pl.BlockSpec((1, tn), lambda j: (0, j))],
        out_specs=pl.BlockSpec((bsz, tn), lambda j: (0, j)),
        out_shape=jax.ShapeDtypeStruct((bsz, n), F32),
        compiler_params=_cparams(("arbitrary",)),
        name="mod",
    )(c, w_mod, b_mod.reshape(1, n))


_C_Q, _C_KV, _C_IQ, _C_SM, _C_Z, _C_XBC, _C_GA, _C_GB, _C_END = 0, 1024, 1152, 1664, 1792, 2816, 4864, 5888, 6912
_SM_IW = IDX_DIM
_SM_DT = IDX_DIM + IDX_HEADS


def _inproj_kernel(x_ref, mod_ref, w_ref, kvw_ref, ikw_ref, ikb_ref,
                   q_ref, kv_ref, iq_ref, sm_ref, z_ref, xbc_ref, ga_ref, gb_ref):
    u = _ln(x_ref[...]) * (1.0 + mod_ref[0, 1:2, :]) + mod_ref[0, 0:1, :]
    ub = u.astype(BF16)

    def mm(a, b):
        return jnp.dot(ub, w_ref[:, a:b], preferred_element_type=F32)

    q_ref[...] = mm(_C_Q, _C_KV).astype(BF16)
    kv = mm(_C_KV, _C_IQ)
    kv = kv * lax.rsqrt(jnp.mean(kv * kv, axis=-1, keepdims=True) + LN_EPS)
    kv_ref[...] = (kv * kvw_ref[...]).astype(BF16)
    iq_ref[...] = mm(_C_IQ, _C_SM).astype(BF16)
    g = mm(_C_SM, _C_Z)
    lane = lax.broadcasted_iota(I32, g.shape, 1)
    is_ik = lane < IDX_DIM
    mu = jnp.sum(jnp.where(is_ik, g, 0.0), axis=-1, keepdims=True) * (1.0 / IDX_DIM)
    gc = g - mu
    var = jnp.sum(jnp.where(is_ik, gc * gc, 0.0), axis=-1, keepdims=True) * (1.0 / IDX_DIM)
    ik = gc * lax.rsqrt(var + LN_EPS) * ikw_ref[...] + ikb_ref[...]
    sm_ref[...] = jnp.where(is_ik, ik, jnp.where(lane < _SM_DT, g * (IDX_HEADS ** -0.5), g))
    z_ref[...] = mm(_C_Z, _C_XBC)
    xbc_ref[...] = mm(_C_XBC, _C_GA)
    ga_ref[...] = mm(_C_GA, _C_GB)
    gb_ref[...] = mm(_C_GB, _C_END)


def _inproj_call(x2, mod3, w_perm, kvw, ikw, ikb, seq):
    n_tok, d = x2.shape
    tm = 256
    row = lambda i: (i, 0)

    def ospec(n):
        return pl.BlockSpec((tm, n), row)

    outs = [(1024, BF16), (128, BF16), (512, BF16), (128, F32), (1024, F32), (2048, F32), (1024, F32), (1024, F32)]
    return pl.pallas_call(
        _inproj_kernel,
        grid=(n_tok // tm,),
        in_specs=[pl.BlockSpec((tm, d), row),
                  pl.BlockSpec((1, 6, d), lambda i: ((i * tm) // seq, 0, 0)),
                  _const_spec(w_perm.shape), _const_spec((1, 128)), _const_spec((1, 128)), _const_spec((1, 128))],
        out_specs=[ospec(n) for n, _ in outs],
        out_shape=[jax.ShapeDtypeStruct((n_tok, n), dt) for n, dt in outs],
        compiler_params=_cparams(("arbitrary",)),
        name="inproj",
    )(x2, mod3, w_perm, kvw, ikw, ikb)


_TQ = 128
_KB = 256


def _t5_bucket(rel):
    half = REL_BUCKETS // 2
    max_exact = half // 2
    ret = (rel > 0).astype(jnp.int32) * half
    n = jnp.abs(rel)
    nf = jnp.maximum(n, 1).astype(jnp.float32)
    large = max_exact + (jnp.log(nf / max_exact) / math.log(REL_MAX_DIST / max_exact)
                         * (half - max_exact)).astype(jnp.int32)
    large = jnp.minimum(large, half - 1)
    return ret + jnp.where(n < max_exact, n, large)


def _bias_tiles(rel_bias):
    i = jnp.arange(_TQ, dtype=jnp.int32)[:, None]
    c = jnp.arange(_KB, dtype=jnp.int32)[None, :]
    tiles = []
    for delta in (-2 * LANES, -LANES, 0):
        tiles.append(rel_bias[_t5_bucket(c + delta - i)])
    far = rel_bias[_t5_bucket(jnp.full((_TQ, _KB), -REL_MAX_DIST, jnp.int32))]
    tiles.append(far)
    b = jnp.stack(tiles).astype(F32)
    return jnp.transpose(b, (0, 3, 1, 2)).reshape(4, A_HEADS * _TQ, _KB)


def _attn_kernel(q_ref, iq_ref, iw_ref, ikt_ref, kvt_ref, kv_ref, bias_ref, o_ref,
                 keys_ref, m_ref, l_ref, acc_ref, *, n_sel, jbits):
    tq, kb = _TQ, _KB
    qs = pl.program_id(1) * tq
    nkb = (qs + tq + kb - 1) // kb
    qpos = lax.broadcasted_iota(I32, (tq, 1), 0) + qs
    qend = (qpos // CHUNK + 1) * CHUNK
    lane = lax.broadcasted_iota(I32, (1, kb), 1)

    iq = iq_ref[...]
    iq_all = jnp.concatenate([iq[:, p * LANES:(p + 1) * LANES] for p in range(IDX_HEADS // 2)], axis=0)
    iw = iw_ref[...] * (IDX_DIM ** -0.5)

    def score_body(j, carry):
        ev = jnp.dot(iq_all, ikt_ref[0, 0, j], preferred_element_type=F32)
        od = jnp.dot(iq_all, ikt_ref[0, 1, j], preferred_element_type=F32)
        acc = jnp.zeros((tq, kb), F32)
        for p in range(IDX_HEADS // 2):
            acc = acc + jnp.maximum(ev[p * tq:(p + 1) * tq], 0.0) * iw[:, 2 * p:2 * p + 1]
            acc = acc + jnp.maximum(od[p * tq:(p + 1) * tq], 0.0) * iw[:, 2 * p + 1:2 * p + 2]
        bits = pltpu.bitcast(acc, I32)
        key = jnp.where(bits < 0, bits ^ 0x7FFFFFFF, bits)
        key = jnp.where(bits == INT_MIN, 0, key)
        key = jnp.where(lane + j * kb < qend, key, INT_MIN)
        keys_ref[j] = key
        return carry

    lax.fori_loop(0, nkb, score_body, 0)

    def count(pred):
        def body(j, acc):
            m = jnp.where(pred(keys_ref[j], lane + j * kb), 1.0, 0.0)
            for s in range(kb // LANES):
                acc = acc + m[:, s * LANES:(s + 1) * LANES]
            return acc
        acc = lax.fori_loop(0, nkb, body, jnp.zeros((tq, LANES), F32))
        return jnp.sum(acc, axis=1, keepdims=True)

    def bit_body(it, tu):
        cu = tu | (jnp.int32(1) << (31 - it))
        cs = cu ^ INT_MIN
        cnt = count(lambda k, kidx: k >= cs)
        return jnp.where(cnt >= n_sel, cu, tu)

    thr = lax.fori_loop(0, 32, bit_body, jnp.zeros((tq, 1), I32)) ^ INT_MIN
    need = n_sel - count(lambda k, kidx: k > thr)

    def j_body(it, jj):
        cj = jj | (jnp.int32(1) << (jbits - 1 - it))
        f = count(lambda k, kidx: (k == thr) & (kidx < cj))
        return jnp.where(f <= need, cj, jj)

    jj = lax.fori_loop(0, jbits, j_body, jnp.zeros((tq, 1), I32))

    q = q_ref[...]
    q_all = jnp.concatenate([q[:, h * LANES:(h + 1) * LANES] for h in range(A_HEADS)], axis=0)
    m_ref[...] = jnp.full(m_ref.shape, -jnp.inf, F32)
    l_ref[...] = jnp.zeros(l_ref.shape, F32)
    acc_ref[...] = jnp.zeros(acc_ref.shape, F32)
    scale = A_LATENT ** -0.5

    def att_body(j, carry):
        k = keys_ref[j]
        kidx = lane + j * kb
        sel = ((k > thr) | ((k == thr) & (kidx < jj))) & (kidx < qend)
        s = jnp.dot(q_all, kvt_ref[0, j], preferred_element_type=F32) * scale
        delta = (j * kb - qs) // LANES
        v = jnp.where(delta < -2, 3, delta + 2)
        s = s + bias_ref[v]
        s = jnp.concatenate([jnp.where(sel, s[h * tq:(h + 1) * tq], -jnp.inf) for h in range(A_HEADS)], axis=0)
        m_prev = m_ref[...]
        m_new = jnp.maximum(m_prev, jnp.max(s, axis=1, keepdims=True))
        m_safe = jnp.where(m_new == -jnp.inf, 0.0, m_new)
        alpha = jnp.exp(m_prev - m_safe)
        p = jnp.exp(s - m_safe)
        l_ref[...] = alpha * l_ref[...] + jnp.sum(p, axis=1, keepdims=True)
        acc_ref[...] = alpha * acc_ref[...] + jnp.dot(p.astype(BF16), kv_ref[0, j], preferred_element_type=F32)
        m_ref[...] = m_new
        return carry

    lax.fori_loop(0, nkb, att_body, 0)
    o = acc_ref[...] / l_ref[...]
    for h in range(A_HEADS):
        o_ref[:, h * LANES:(h + 1) * LANES] = o[h * tq:(h + 1) * tq].astype(o_ref.dtype)


def _attn_call(q, iq, iw, ikt2, kvt, kvb, bias, bsz, seq):
    tq, kb = _TQ, _KB
    nq = seq // tq
    nblk = seq // kb
    n_sel = min(TOPK_KEYS_MAX, seq // 4)
    jbits = int(seq).bit_length()
    row = lambda b, i: (b * nq + i, 0)
    kern = functools.partial(_attn_kernel, n_sel=float(n_sel), jbits=jbits)
    return pl.pallas_call(
        kern,
        grid=(bsz, nq),
        in_specs=[pl.BlockSpec((tq, A_HEADS * A_LATENT), row),
                  pl.BlockSpec((tq, IDX_HEADS * IDX_DIM), row),
                  pl.BlockSpec((tq, IDX_HEADS), row),
                  pl.BlockSpec((1, 2, nblk, LANES, kb), lambda b, i: (b, 0, 0, 0, 0)),
                  pl.BlockSpec((1, nblk, A_LATENT, kb), lambda b, i: (b, 0, 0, 0)),
                  pl.BlockSpec((1, nblk, kb, A_LATENT), lambda b, i: (b, 0, 0, 0)),
                  _const_spec(bias.shape)],
        out_specs=pl.BlockSpec((tq, A_HEADS * A_LATENT), row),
        out_shape=jax.ShapeDtypeStruct((bsz * seq, A_HEADS * A_LATENT), BF16),
        scratch_shapes=[pltpu.VMEM((nblk, tq, kb), I32),
                        pltpu.VMEM((A_HEADS * tq, 1), F32),
                        pltpu.VMEM((A_HEADS * tq, 1), F32),
                        pltpu.VMEM((A_HEADS * tq, A_LATENT), F32)],
        compiler_params=_cparams(("arbitrary", "arbitrary")),
        name="attention",
    )(q, iq, iw, ikt2, kvt, kvb, bias)


_SSD_L = 256
_PAIRS = SSM_HEADS // 2


def _ssd_kernel(z_ref, xbc_ref, dt_ref, dtt_ref, cw_ref, cb_ref, dtb_ref, dtbt_ref, al_ref, alt_ref,
                dsk_ref, nw_ref, o_ref, ext_ref, state_ref, y_ref):
    L = _SSD_L
    hd = SSM_HEADDIM

    @pl.when(pl.program_id(1) == 0)
    def _():
        ext_ref[0:8, :] = jnp.zeros((8, SSM_CONV_DIM), F32)
        state_ref[...] = jnp.zeros(state_ref.shape, F32)

    x = xbc_ref[...]
    ext_ref[8:8 + L, :] = x
    w = cw_ref[...]
    conv = x * w[3:4] + cb_ref[...]
    for k in range(1, SSM_CONV):
        conv = conv + ext_ref[8 - k:8 - k + L, :] * w[SSM_CONV - 1 - k:SSM_CONV - k]
    ext_ref[0:8, :] = x[L - 8:L]
    act = conv * jax.nn.sigmoid(conv)
    xs = act[:, :SSM_D_INNER]
    boff = SSM_D_INNER
    coff = SSM_D_INNER + SSM_GROUPS * SSM_STATE

    def softplus(v):
        return jnp.maximum(v, 0.0) + jnp.log1p(jnp.exp(-jnp.abs(v)))

    dt = softplus(dt_ref[...] + dtb_ref[...])
    dtt = softplus(dtt_ref[0] + dtbt_ref[...])
    a_col = dt * (-jnp.exp(al_ref[...]))
    a_row = dtt * (-jnp.exp(alt_ref[...]))
    ri = lax.broadcasted_iota(I32, (L, L), 0)
    ci = lax.broadcasted_iota(I32, (L, L), 1)
    causal = ci <= ri
    cs_col = jnp.dot(jnp.where(causal, 1.0, 0.0), a_col, precision=HI, preferred_element_type=F32)
    cs_row = jnp.dot(a_row, jnp.where(ri <= ci, 1.0, 0.0), precision=HI, preferred_element_type=F32)
    cs_last = cs_col[L - 1:L, :]
    lane = lax.broadcasted_iota(I32, (1, LANES), 1)
    lo = lane < hd
    sub = lax.broadcasted_iota(I32, (LANES, 1), 0)

    for g in range(SSM_GROUPS):
        bm = act[:, boff + g * SSM_STATE: boff + (g + 1) * SSM_STATE].astype(BF16)
        cm = act[:, coff + g * SSM_STATE: coff + (g + 1) * SSM_STATE].astype(BF16)
        cb = lax.dot_general(cm, bm, (((1,), (1,)), ((), ())), preferred_element_type=F32)
        for pp in range(_PAIRS // SSM_GROUPS):
            p = g * (_PAIRS // SSM_GROUPS) + pp
            h0, h1 = 2 * p, 2 * p + 1
            xp = xs[:, p * LANES:(p + 1) * LANES]
            dtl = jnp.where(lo, dt[:, h0:h0 + 1], dt[:, h1:h1 + 1])
            xdt = xp * dtl
            csl = jnp.where(lo, cs_col[:, h0:h0 + 1], cs_col[:, h1:h1 + 1])
            last = jnp.where(lo, cs_last[:, h0:h0 + 1], cs_last[:, h1:h1 + 1])
            ydiag = jnp.zeros((L, LANES), F32)
            for h, msk in ((h0, lo), (h1, jnp.logical_not(lo))):
                seg = cs_col[:, h:h + 1] - cs_row[h:h + 1, :]
                gm = (cb * jnp.exp(jnp.where(causal, seg, -jnp.inf))).astype(BF16)
                ydiag = ydiag + jnp.dot(gm, jnp.where(msk, xdt, 0.0).astype(BF16), preferred_element_type=F32)
            prev = state_ref[p]
            yoff = lax.dot_general(cm, prev.astype(BF16), (((1,), (1,)), ((), ())), preferred_element_type=F32)
            y_ref[:, p * LANES:(p + 1) * LANES] = ydiag + yoff * jnp.exp(csl) + xp * dsk_ref[:, p * LANES:(p + 1) * LANES]
            wx = (xdt * jnp.exp(last - csl)).astype(BF16)
            st = lax.dot_general(wx, bm, (((0,), (0,)), ((), ())), preferred_element_type=F32)
            cdec = jnp.where(sub < hd, jnp.exp(cs_last[:, h0:h0 + 1]), jnp.exp(cs_last[:, h1:h1 + 1]))
            state_ref[p] = prev * cdec + st

    z = z_ref[...]
    y = y_ref[...] * (z * jax.nn.sigmoid(z))
    gw = SSM_D_INNER // SSM_GROUPS
    for g in range(SSM_GROUPS):
        yg = y[:, g * gw:(g + 1) * gw]
        yg = yg * lax.rsqrt(jnp.mean(yg * yg, axis=-1, keepdims=True) + LN_EPS)
        o_ref[:, g * gw:(g + 1) * gw] = (yg * nw_ref[:, g * gw:(g + 1) * gw]).astype(o_ref.dtype)


def _ssd_call(z, xbc, dt, dtt, conv_w, conv_b, dt_bias, a_log, d_skip, norm_w, bsz, seq):
    L = _SSD_L
    nc = seq // L
    row = lambda b, c: (b * nc + c, 0)
    h = SSM_HEADS
    return pl.pallas_call(
        _ssd_kernel,
        grid=(bsz, nc),
        in_specs=[pl.BlockSpec((L, SSM_D_INNER), row),
                  pl.BlockSpec((L, SSM_CONV_DIM), row),
                  pl.BlockSpec((L, h), row),
                  pl.BlockSpec((1, h, L), lambda b, c: (b, 0, c)),
                  _const_spec((SSM_CONV, SSM_CONV_DIM)), _const_spec((1, SSM_CONV_DIM)),
                  _const_spec((1, h)), _const_spec((h, 1)), _const_spec((1, h)), _const_spec((h, 1)),
                  _const_spec((1, SSM_D_INNER)), _const_spec((1, SSM_D_INNER))],
        out_specs=pl.BlockSpec((L, SSM_D_INNER), row),
        out_shape=jax.ShapeDtypeStruct((bsz * seq, SSM_D_INNER), BF16),
        scratch_shapes=[pltpu.VMEM((L + 8, SSM_CONV_DIM), F32),
                        pltpu.VMEM((_PAIRS, LANES, SSM_STATE), F32),
                        pltpu.VMEM((L, SSM_D_INNER), F32)],
        compiler_params=_cparams(("arbitrary", "arbitrary")),
        name="ssd",
    )(z, xbc, dt, dtt, conv_w, conv_b.reshape(1, -1), dt_bias.reshape(1, h), dt_bias.reshape(h, 1),
      a_log.reshape(1, h), a_log.reshape(h, 1), jnp.repeat(d_skip, SSM_HEADDIM).reshape(1, -1),
      norm_w.reshape(1, -1))


_TM = 256


def _merge_kernel(oa_ref, ob_ref, ga_ref, gb_ref, x_ref, mod_ref, wpa_ref, wpb_ref, wo_ref, g1_ref, b1_ref,
                  wr_ref, br_ref, x1_ref, u2_ref, route_ref, gate_ref, cnt_ref, base_ref, *, alpha):
    tm = _TM

    @pl.when(pl.program_id(0) == 0)
    def _():
        base_ref[...] = jnp.zeros(base_ref.shape, F32)

    ma = jnp.dot(oa_ref[...], wpa_ref[...], preferred_element_type=F32)
    mb = jnp.dot(ob_ref[...], wpb_ref[...], preferred_element_type=F32)
    merged = jax.nn.sigmoid(ga_ref[...]) * ma + jax.nn.sigmoid(gb_ref[...]) * mb
    t = jnp.dot(merged.astype(BF16), wo_ref[...], preferred_element_type=F32)
    x1 = _ln(alpha * x_ref[...] + mod_ref[0, 2:3, :] * t) * g1_ref[...] + b1_ref[...]
    x1_ref[...] = x1
    u2 = _ln(x1) * (1.0 + mod_ref[0, 4:5, :]) + mod_ref[0, 3:4, :]
    u2_ref[...] = u2
    lane = lax.broadcasted_iota(I32, (tm, LANES), 1)
    lanef = lane.astype(F32)
    logits = jnp.dot(u2, wr_ref[...], precision=HI, preferred_element_type=F32) + br_ref[...]
    logits = jnp.where(lane < N_EXPERTS, logits, -jnp.inf)

    vals, ids = [], []
    for _ in range(TOPK_EXPERTS):
        m = jnp.max(logits, axis=1, keepdims=True)
        idx = jnp.min(jnp.where(logits == m, lanef, float(LANES)), axis=1, keepdims=True)
        vals.append(m)
        ids.append(idx)
        logits = jnp.where(lanef == idx, -jnp.inf, logits)
    es = [jnp.exp(v - vals[0]) for v in vals]
    den = es[0] + es[1] + es[2] + es[3]

    onehot = jnp.zeros((tm, LANES), F32)
    for idx in ids:
        onehot = onehot + jnp.where(lanef == idx, 1.0, 0.0)
    ri = lax.broadcasted_iota(I32, (tm, tm), 0)
    ci = lax.broadcasted_iota(I32, (tm, tm), 1)
    before = jnp.where(ci < ri, 1.0, 0.0).astype(BF16)
    prefix = jnp.dot(before, onehot.astype(BF16), preferred_element_type=F32) + base_ref[...]
    route = jnp.zeros((tm, LANES), F32)
    gates = jnp.zeros((tm, LANES), F32)
    for j in range(TOPK_EXPERTS):
        rank = jnp.sum(jnp.where(lanef == ids[j], prefix, 0.0), axis=1, keepdims=True)
        route = jnp.where(lane == j, ids[j], route)
        route = jnp.where(lane == TOPK_EXPERTS + j, rank, route)
        gates = jnp.where(lane == j, es[j] / den, gates)
    route_ref[...] = route.astype(I32)
    gate_ref[...] = gates
    base = base_ref[...] + jnp.sum(onehot, axis=0, keepdims=True)
    base_ref[...] = base
    cnt_ref[...] = jnp.broadcast_to(base, cnt_ref.shape)


def _merge_call(o_a, o_b, g_a, g_b, x2, mod3, wpa, wpb, wo, ln_g, ln_b, wr, br, seq, alpha):
    n_tok, d = x2.shape
    tm = _TM
    row = lambda i: (i, 0)
    blk = pl.BlockSpec((tm, d), row)
    sm = pl.BlockSpec((tm, LANES), row)
    return pl.pallas_call(
        functools.partial(_merge_kernel, alpha=alpha),
        grid=(n_tok // tm,),
        in_specs=[blk, blk, blk, blk, blk,
                  pl.BlockSpec((1, 6, d), lambda i: ((i * tm) // seq, 0, 0)),
                  _const_spec((d, d)), _const_spec((d, d)), _const_spec((d, d)),
                  _const_spec((1, d)), _const_spec((1, d)), _const_spec((d, LANES)), _const_spec((1, LANES))],
        out_specs=[blk, blk, sm, sm, pl.BlockSpec((8, LANES), lambda i: (0, 0))],
        out_shape=[jax.ShapeDtypeStruct((n_tok, d), F32), jax.ShapeDtypeStruct((n_tok, d), F32),
                   jax.ShapeDtypeStruct((n_tok, LANES), I32), jax.ShapeDtypeStruct((n_tok, LANES), F32),
                   jax.ShapeDtypeStruct((8, LANES), F32)],
        scratch_shapes=[pltpu.VMEM((1, LANES), F32)],
        compiler_params=_cparams(("arbitrary",)),
        name="merge",
    )(o_a, o_b, g_a, g_b, x2, mod3, wpa, wpb, wo, ln_g, ln_b, wr, br)


_TD = 256


def _dispatch_kernel(dest_ref, u2_ref, init_ref, xs_ref, sem):
    del init_ref

    def issue(t, carry):
        for j in range(TOPK_EXPERTS):
            d = dest_ref[t * TOPK_EXPERTS + j]
            pltpu.make_async_copy(u2_ref.at[pl.ds(t, 1), :], xs_ref.at[pl.ds(d, 1), :], sem).start()
        return carry

    lax.fori_loop(0, _TD, issue, 0)

    def drain(t, carry):
        pltpu.make_async_copy(u2_ref.at[pl.ds(0, 1), :], xs_ref.at[pl.ds(0, 1), :], sem).wait()
        return carry

    lax.fori_loop(0, _TD * TOPK_EXPERTS, drain, 0)


def _dispatch_call(dest_flat, u2, n_slots):
    n_tok, d = u2.shape
    return pl.pallas_call(
        _dispatch_kernel,
        grid=(n_tok // _TD,),
        in_specs=[pl.BlockSpec((_TD * TOPK_EXPERTS,), lambda i: (i,), memory_space=pltpu.SMEM),
                  pl.BlockSpec((_TD, d), lambda i: (i, 0)),
                  pl.BlockSpec(memory_space=pl.ANY)],
        out_specs=pl.BlockSpec(memory_space=pl.ANY),
        out_shape=jax.ShapeDtypeStruct((n_slots, d), F32),
        scratch_shapes=[pltpu.SemaphoreType.DMA(())],
        input_output_aliases={2: 0},
        compiler_params=_cparams(("arbitrary",)),
        name="dispatch",
    )(dest_flat, u2, jnp.zeros((n_slots, d), F32))


_TMB = 512


def _expert_kernel(be_ref, nu_ref, xs_ref, w1_ref, b1_ref, w2_ref, b2_ref, y_ref):
    del be_ref
    i = pl.program_id(0)
    f = w2_ref.shape[1]

    @pl.when(i < nu_ref[0])
    def _():
        h = jnp.dot(xs_ref[...].astype(BF16), w1_ref[0], preferred_element_type=F32) + b1_ref[0]
        gate = jnp.minimum(h[:, :f], SWIGLU_LIMIT)
        up = jnp.clip(h[:, f:], -SWIGLU_LIMIT, SWIGLU_LIMIT)
        act = (up + 1.0) * gate * jax.nn.sigmoid(SWIGLU_ALPHA * gate)
        y_ref[...] = jnp.dot(act.astype(BF16), w2_ref[0], preferred_element_type=F32) + b2_ref[0]

    @pl.when(i >= nu_ref[0])
    def _():
        y_ref[...] = jnp.zeros(y_ref.shape, F32)


def _expert_call(block_expert, n_used, xs, w1, b1, w2, b2):
    n_slots, d = xs.shape
    ne, _, f2 = w1.shape
    f = f2 // 2
    grid_spec = pltpu.PrefetchScalarGridSpec(
        num_scalar_prefetch=2,
        grid=(n_slots // _TMB,),
        in_specs=[pl.BlockSpec((_TMB, d), lambda i, be, nu: (i, 0)),
                  pl.BlockSpec((1, d, f2), lambda i, be, nu: (be[i], 0, 0)),
                  pl.BlockSpec((1, 1, f2), lambda i, be, nu: (be[i], 0, 0)),
                  pl.BlockSpec((1, f, d), lambda i, be, nu: (be[i], 0, 0)),
                  pl.BlockSpec((1, 1, d), lambda i, be, nu: (be[i], 0, 0))],
        out_specs=pl.BlockSpec((_TMB, d), lambda i, be, nu: (i, 0)),
    )
    return pl.pallas_call(
        _expert_kernel,
        grid_spec=grid_spec,
        out_shape=jax.ShapeDtypeStruct((n_slots, d), F32),
        compiler_params=_cparams(("arbitrary",)),
        name="experts",
    )(block_expert, n_used, xs, w1, b1.reshape(ne, 1, f2), w2, b2.reshape(ne, 1, d))


_TC = 256


def _combine_kernel(dest_ref, gate_ref, x1_ref, mod_ref, g2_ref, b2_ref, y_hbm, o_ref, buf_ref, sem, *, alpha):
    def issue(t, carry):
        for j in range(TOPK_EXPERTS):
            d = dest_ref[t * TOPK_EXPERTS + j]
            pltpu.make_async_copy(y_hbm.at[pl.ds(d, 1), :], buf_ref.at[j, pl.ds(t, 1), :], sem).start()
        return carry

    lax.fori_loop(0, _TC, issue, 0)

    def drain(t, carry):
        pltpu.make_async_copy(y_hbm.at[pl.ds(0, 1), :], buf_ref.at[0, pl.ds(0, 1), :], sem).wait()
        return carry

    lax.fori_loop(0, _TC * TOPK_EXPERTS, drain, 0)
    gates = gate_ref[...]
    y = gates[:, 0:1] * buf_ref[0]
    for j in range(1, TOPK_EXPERTS):
        y = y + gates[:, j:j + 1] * buf_ref[j]
    o_ref[...] = _ln(alpha * x1_ref[...] + mod_ref[0, 5:6, :] * y) * g2_ref[...] + b2_ref[...]


def _combine_call(dest_flat, gates, x1, mod3, ln_g, ln_b, y, seq, alpha):
    n_tok, d = x1.shape
    tc = _TC
    row = lambda i: (i, 0)
    return pl.pallas_call(
        functools.partial(_combine_kernel, alpha=alpha),
        grid=(n_tok // tc,),
        in_specs=[pl.BlockSpec((tc * TOPK_EXPERTS,), lambda i: (i,), memory_space=pltpu.SMEM),
                  pl.BlockSpec((tc, LANES), row),
                  pl.BlockSpec((tc, d), row),
                  pl.BlockSpec((1, 6, d), lambda i: ((i * tc) // seq, 0, 0)),
                  _const_spec((1, d)), _const_spec((1, d)),
                  pl.BlockSpec(memory_space=pl.ANY)],
        out_specs=pl.BlockSpec((tc, d), row),
        out_shape=jax.ShapeDtypeStruct((n_tok, d), F32),
        scratch_shapes=[pltpu.VMEM((TOPK_EXPERTS, tc, d), F32), pltpu.SemaphoreType.DMA(())],
        compiler_params=_cparams(("arbitrary",)),
        name="combine",
    )(dest_flat, gates, x1, mod3, ln_g, ln_b, y)


def _permute_w_in(w):
    d = w.shape[0]
    s = np.cumsum([0, A_HEADS * A_LATENT, A_LATENT, IDX_HEADS * IDX_DIM, IDX_DIM, IDX_HEADS,
                   SSM_D_INNER, SSM_CONV_DIM, SSM_HEADS, d, d]).tolist()
    q, kv, iq, ik, iw, z, xbc, dt, ga, gb = [w[:, s[i]:s[i + 1]] for i in range(10)]
    pad = jnp.zeros((d, LANES - IDX_DIM - IDX_HEADS - SSM_HEADS), w.dtype)
    return jnp.concatenate([q, kv, iq, ik, iw, dt, pad, z, xbc, ga, gb], axis=1).astype(BF16)


def _pad_lanes(v, fill=0.0):
    return jnp.pad(v.reshape(1, -1), ((0, 0), (0, LANES - v.shape[-1])), constant_values=fill)


def kernel(x, c, w_mod, b_mod, w_in, kv_norm_w, idx_k_norm_w, idx_k_norm_b, rel_bias, conv_w, conv_b, dt_bias,
           a_log, d_skip, ssm_norm_w, w_proj_a, w_proj_b, w_out, ln1_g, ln1_b, w_router, b_router, w1, b1, w2, b2,
           ln2_g, ln2_b):
    bsz, seq, d = x.shape
    depth = w_mod.shape[0]
    alpha = (2.0 * depth) ** 0.25
    n_tok = bsz * seq
    n_asg = n_tok * TOPK_EXPERTS
    n_blocks = n_asg // _TMB + N_EXPERTS
    n_slots = n_blocks * _TMB
    nblk = seq // _KB
    bias = _bias_tiles(rel_bias)
    x2 = x.reshape(n_tok, d)
    for l in range(depth):
        mod3 = _mod_call(c, w_mod[l], b_mod[l]).reshape(bsz, 6, d)
        q, kvn, iq, small, z, xbc, g_a, g_b = _inproj_call(
            x2, mod3, _permute_w_in(w_in[l]), kv_norm_w[l].reshape(1, -1),
            _pad_lanes(idx_k_norm_w[l]), _pad_lanes(idx_k_norm_b[l]), seq)
        ik = small[:, :IDX_DIM].astype(BF16).reshape(bsz, nblk, _KB, IDX_DIM)
        ikt = jnp.swapaxes(ik, 2, 3)
        zpad = jnp.zeros_like(ikt)
        ikt2 = jnp.stack([jnp.concatenate([ikt, zpad], axis=2), jnp.concatenate([zpad, ikt], axis=2)], axis=1)
        kvb = kvn.reshape(bsz, nblk, _KB, A_LATENT)
        kvt = jnp.swapaxes(kvb, 2, 3)
        iw = small[:, _SM_IW:_SM_IW + IDX_HEADS]
        o_a = _attn_call(q, iq, iw, ikt2, kvt, kvb, bias, bsz, seq)
        dt = small[:, _SM_DT:_SM_DT + SSM_HEADS]
        dtt = jnp.swapaxes(dt.reshape(bsz, seq, SSM_HEADS), 1, 2)
        o_b = _ssd_call(z, xbc, dt, dtt, conv_w[l], conv_b[l], dt_bias[l], a_log[l], d_skip[l], ssm_norm_w[l],
                        bsz, seq)
        wr = jnp.pad(w_router[l], ((0, 0), (0, LANES - N_EXPERTS)))
        x1, u2, route, gates, cnt = _merge_call(
            o_a, o_b, g_a, g_b, x2, mod3, w_proj_a[l].astype(BF16), w_proj_b[l].astype(BF16),
            w_out[l].astype(BF16), ln1_g[l].reshape(1, -1), ln1_b[l].reshape(1, -1), wr, _pad_lanes(b_router[l]),
            seq, alpha)
        counts = cnt[0, :N_EXPERTS].astype(I32)
        padded = (counts + _TMB - 1) // _TMB * _TMB
        pends = jnp.cumsum(padded)
        pstarts = pends - padded
        eid = route[:, :TOPK_EXPERTS]
        dest_flat = (pstarts[eid] + route[:, TOPK_EXPERTS:2 * TOPK_EXPERTS]).reshape(n_asg)
        block_start = jnp.arange(n_blocks, dtype=I32) * _TMB
        block_expert = jnp.minimum(jnp.sum(block_start[:, None] >= pends[None, :], axis=1), N_EXPERTS - 1).astype(I32)
        n_used = (pends[-1:] // _TMB).astype(I32)
        xs = _dispatch_call(dest_flat, u2, n_slots)
        y = _expert_call(block_expert, n_used, xs, w1[l].astype(BF16), b1[l], w2[l].astype(BF16), b2[l])
        x2 = _combine_call(dest_flat, gates, x1, mod3, ln2_g[l].reshape(1, -1), ln2_b[l].reshape(1, -1), y, seq, alpha)
    return x2.reshape(bsz, seq, d)
```

```python
import functools
import math

import jax
import jax.numpy as jnp
import numpy as np
from jax import lax
from jax.experimental import pallas as pl
from jax.experimental.pallas import tpu as pltpu

F32 = jnp.float32
BF16 = jnp.bfloat16
I32 = jnp.int32

CHUNK = 64
A_HEADS = 8
A_LATENT = 128
IDX_HEADS = 8
IDX_DIM = 64
TOPK_KEYS_MAX = 256
REL_BUCKETS = 32
REL_MAX_DIST = 128
SSM_D_INNER = 1024
SSM_HEADDIM = 64
SSM_HEADS = SSM_D_INNER // SSM_HEADDIM
SSM_GROUPS = 4
SSM_STATE = 128
SSM_CONV = 4
SSM_CONV_DIM = SSM_D_INNER + 2 * SSM_GROUPS * SSM_STATE
N_EXPERTS = 32
TOPK_EXPERTS = 4
SWIGLU_LIMIT = 7.0
SWIGLU_ALPHA = 1.702
LN_EPS = 1e-5

LANES = 128
INT_MIN = -2147483648
VMEM_LIMIT = 56 * 1024 * 1024

HI = lax.Precision.HIGHEST


def _cparams(sem):
    return pltpu.CompilerParams(dimension_semantics=sem, vmem_limit_bytes=VMEM_LIMIT)


def _ln(x):
    mu = jnp.mean(x, axis=-1, keepdims=True)
    xc = x - mu
    var = jnp.mean(xc * xc, axis=-1, keepdims=True)
    return xc * lax.rsqrt(var + LN_EPS)


def _const_spec(shape):
    nd = len(shape)
    return pl.BlockSpec(shape, lambda *_: (0,) * nd, pipeline_mode=pl.Buffered(1))


def _mod_kernel(c_ref, w_ref, b_ref, o_ref):
    c = c_ref[...]
    sc = c * jax.nn.sigmoid(c)
    o_ref[...] = jnp.dot(sc, w_ref[...], precision=HI, preferred_element_type=F32) + b_ref[...]


def _mod_call(c, w_mod, b_mod):
    bsz, d = c.shape
    n = w_mod.shape[1]
    tn = 1024
    return pl.pallas_call(
        _mod_kernel,
        grid=(n // tn,),
        in_specs=[pl.BlockSpec((bsz, d), lambda j: (0, 0)),
                  pl.BlockSpec((d, tn), lambda j: (0, j)),
                  pl.BlockSpec((1, tn), lambda j: (0, j))],
        out_specs=pl.BlockSpec((bsz, tn), lambda j: (0, j)),
        out_shape=jax.ShapeDtypeStruct((bsz, n), F32),
        compiler_params=_cparams(("arbitrary",)),
        name="mod",
    )(c, w_mod, b_mod.reshape(1, n))


_C_Q, _C_KV, _C_IQ, _C_SM, _C_Z, _C_XBC, _C_GA, _C_GB, _C_END = 0, 1024, 1152, 1664, 1792, 2816, 4864, 5888, 6912
_SM_IW = IDX_DIM
_SM_DT = IDX_DIM + IDX_HEADS


def _inproj_kernel(x_ref, mod_ref, w_ref, kvw_ref, ikw_ref, ikb_ref,
                   q_ref, kv_ref, iq_ref, sm_ref, z_ref, xbc_ref, ga_ref, gb_ref):
    u = _ln(x_ref[...]) * (1.0 + mod_ref[0, 1:2, :]) + mod_ref[0, 0:1, :]
    ub = u.astype(BF16)

    def mm(a, b):
        return jnp.dot(ub, w_ref[:, a:b], preferred_element_type=F32)

    q_ref[...] = mm(_C_Q, _C_KV).astype(BF16)
    kv = mm(_C_KV, _C_IQ)
    kv = kv * lax.rsqrt(jnp.mean(kv * kv, axis=-1, keepdims=True) + LN_EPS)
    kv_ref[...] = (kv * kvw_ref[...]).astype(BF16)
    iq_ref[...] = mm(_C_IQ, _C_SM).astype(BF16)
    g = mm(_C_SM, _C_Z)
    lane = lax.broadcasted_iota(I32, g.shape, 1)
    is_ik = lane < IDX_DIM
    mu = jnp.sum(jnp.where(is_ik, g, 0.0), axis=-1, keepdims=True) * (1.0 / IDX_DIM)
    gc = g - mu
    var = jnp.sum(jnp.where(is_ik, gc * gc, 0.0), axis=-1, keepdims=True) * (1.0 / IDX_DIM)
    ik = gc * lax.rsqrt(var + LN_EPS) * ikw_ref[...] + ikb_ref[...]
    sm_ref[...] = jnp.where(is_ik, ik, jnp.where(lane < _SM_DT, g * (IDX_HEADS ** -0.5), g))
    z_ref[...] = mm(_C_Z, _C_XBC)
    xbc_ref[...] = mm(_C_XBC, _C_GA)
    ga_ref[...] = mm(_C_GA, _C_GB)
    gb_ref[...] = mm(_C_GB, _C_END)


def _inproj_call(x2, mod3, w_perm, kvw, ikw, ikb, seq):
    n_tok, d = x2.shape
    tm = 256
    row = lambda i: (i, 0)

    def ospec(n):
        return pl.BlockSpec((tm, n), row)

    outs = [(1024, BF16), (128, BF16), (512, BF16), (128, F32), (1024, F32), (2048, F32), (1024, F32), (1024, F32)]
    return pl.pallas_call(
        _inproj_kernel,
        grid=(n_tok // tm,),
        in_specs=[pl.BlockSpec((tm, d), row),
                  pl.BlockSpec((1, 6, d), lambda i: ((i * tm) // seq, 0, 0)),
                  _const_spec(w_perm.shape), _const_spec((1, 128)), _const_spec((1, 128)), _const_spec((1, 128))],
        out_specs=[ospec(n) for n, _ in outs],
        out_shape=[jax.ShapeDtypeStruct((n_tok, n), dt) for n, dt in outs],
        compiler_params=_cparams(("arbitrary",)),
        name="inproj",
    )(x2, mod3, w_perm, kvw, ikw, ikb)


_TQ = 128
_KB = 256


def _t5_bucket(rel):
    half = REL_BUCKETS // 2
    max_exact = half // 2
    ret = (rel > 0).astype(jnp.int32) * half
    n = jnp.abs(rel)
    nf = jnp.maximum(n, 1).astype(jnp.float32)
    large = max_exact + (jnp.log(nf / max_exact) / math.log(REL_MAX_DIST / max_exact)
                         * (half - max_exact)).astype(jnp.int32)
    large = jnp.minimum(large, half - 1)
    return ret + jnp.where(n < max_exact, n, large)


def _bias_tiles(rel_bias):
    i = jnp.arange(_TQ, dtype=jnp.int32)[:, None]
    c = jnp.arange(_KB, dtype=jnp.int32)[None, :]
    tiles = []
    for delta in (-2 * LANES, -LANES, 0):
        tiles.append(rel_bias[_t5_bucket(c + delta - i)])
    far = rel_bias[_t5_bucket(jnp.full((_TQ, _KB), -REL_MAX_DIST, jnp.int32))]
    tiles.append(far)
    b = jnp.stack(tiles).astype(F32)
    return jnp.transpose(b, (0, 3, 1, 2)).reshape(4, A_HEADS * _TQ, _KB)


def _attn_kernel(q_ref, iq_ref, iwt_ref, ik2_ref, kvt_ref, kv_ref, bias_ref, o_ref,
                 keys_ref, s_ref, mx_ref, l_ref, acc_ref, *, n_sel, jbits):
    tq, kb = _TQ, _KB
    qs = pl.program_id(1) * tq
    nkb = (qs + tq + kb - 1) // kb
    nt = (((1,), (1,)), ((), ()))

    qpos = lax.broadcasted_iota(I32, (1, tq), 1) + qs
    qend = (qpos // CHUNK + 1) * CHUNK
    krow = lax.broadcasted_iota(I32, (kb, tq), 0)
    iq = iq_ref[...]
    iwt = iwt_ref[0] * (IDX_DIM ** -0.5)

    def score_body(j, carry):
        acc = jnp.zeros((kb, tq), F32)
        for p in range(IDX_HEADS // 2):
            pair = iq[:, p * LANES:(p + 1) * LANES]
            for par in range(2):
                h = 2 * p + par
                s = lax.dot_general(ik2_ref[0, par, j], pair, nt, preferred_element_type=F32)
                acc = acc + jnp.maximum(s, 0.0) * iwt[h:h + 1, :]
        bits = pltpu.bitcast(acc, I32)
        key = jnp.where(bits < 0, bits ^ 0x7FFFFFFF, bits)
        key = jnp.where(bits == INT_MIN, 0, key)
        keys_ref[j] = jnp.where(krow + j * kb < qend, key, INT_MIN)
        return carry

    lax.fori_loop(0, nkb, score_body, 0)

    def count(pred):
        def body(j, acc):
            m = jnp.where(pred(keys_ref[j], krow + j * kb), 1.0, 0.0)
            parts = [m[8 * i:8 * i + 8] for i in range(kb // 8)]
            while len(parts) > 1:
                parts = [parts[i] + parts[i + 1] for i in range(0, len(parts), 2)]
            return acc + parts[0]
        acc = lax.fori_loop(0, nkb, body, jnp.zeros((8, tq), F32))
        return jnp.sum(acc, axis=0, keepdims=True)

    def bit_body(it, tu):
        cu = tu | (jnp.int32(1) << (31 - it))
        cs = cu ^ INT_MIN
        cnt = count(lambda k, kidx: k >= cs)
        return jnp.where(cnt >= n_sel, cu, tu)

    thr = lax.fori_loop(0, 32, bit_body, jnp.zeros((1, tq), I32)) ^ INT_MIN
    n_ge = count(lambda k, kidx: k >= thr)

    def tie_search():
        need = n_sel - count(lambda k, kidx: k > thr)

        def j_body(it, jj):
            cj = jj | (jnp.int32(1) << (jbits - 1 - it))
            f = count(lambda k, kidx: (k == thr) & (kidx < cj))
            return jnp.where(f <= need, cj, jj)

        return lax.fori_loop(0, jbits, j_body, jnp.zeros((1, tq), I32))

    jj = lax.cond(jnp.max(n_ge) > n_sel, tie_search, lambda: jnp.full((1, tq), (1 << jbits) - 1, I32))

    q = q_ref[...]
    q_all = jnp.concatenate([q[:, h * LANES:(h + 1) * LANES] for h in range(A_HEADS)], axis=0)
    eye = jnp.where(lax.broadcasted_iota(I32, (tq, tq), 0) == lax.broadcasted_iota(I32, (tq, tq), 1),
                    1.0, 0.0).astype(BF16)
    mx_ref[...] = jnp.full(mx_ref.shape, -jnp.inf, F32)
    scale = A_LATENT ** -0.5

    def logit_body(j, carry):
        k = keys_ref[j]
        kidx = krow + j * kb
        selt = ((k > thr) | ((k == thr) & (kidx < jj))) & (kidx < qend)
        sel = lax.dot_general(eye, jnp.where(selt, 1.0, 0.0).astype(BF16), nt, preferred_element_type=F32) > 0.5
        s = jnp.dot(q_all, kvt_ref[0, j], preferred_element_type=F32) * scale
        delta = (j * kb - qs) // LANES
        s = s + bias_ref[jnp.where(delta < -2, 3, delta + 2)]
        s = jnp.concatenate([jnp.where(sel, s[h * tq:(h + 1) * tq], -jnp.inf) for h in range(A_HEADS)], axis=0)
        s_ref[j] = s
        mx = mx_ref[...]
        for c in range(kb // LANES):
            mx = jnp.maximum(mx, s[:, c * LANES:(c + 1) * LANES])
        mx_ref[...] = mx
        return carry

    lax.fori_loop(0, nkb, logit_body, 0)
    m = jnp.max(mx_ref[...], axis=1, keepdims=True)
    m = jnp.where(m == -jnp.inf, 0.0, m)
    mx_ref[...] = jnp.broadcast_to(m, mx_ref.shape)
    l_ref[...] = jnp.zeros(l_ref.shape, F32)
    acc_ref[...] = jnp.zeros(acc_ref.shape, F32)

    def pv_body(j, carry):
        mb = mx_ref[...]
        s = s_ref[j]
        ps = [jnp.exp(s[:, c * LANES:(c + 1) * LANES] - mb) for c in range(kb // LANES)]
        lsum = l_ref[...]
        for pc in ps:
            lsum = lsum + pc
        l_ref[...] = lsum
        p = jnp.concatenate(ps, axis=1).astype(BF16)
        acc_ref[...] += jnp.dot(p, kv_ref[0, j], preferred_element_type=F32)
        return carry

    lax.fori_loop(0, nkb, pv_body, 0)
    o = acc_ref[...] / jnp.sum(l_ref[...], axis=1, keepdims=True)
    for h in range(A_HEADS):
        o_ref[:, h * LANES:(h + 1) * LANES] = o[h * tq:(h + 1) * tq].astype(o_ref.dtype)


def _attn_call(q, iq, iwt, ik2, kvt, kvb, bias, bsz, seq):
    tq, kb = _TQ, _KB
    nq = seq // tq
    nblk = seq // kb
    n_sel = min(TOPK_KEYS_MAX, seq // 4)
    jbits = int(seq).bit_length()
    row = lambda b, i: (b * nq + i, 0)
    kern = functools.partial(_attn_kernel, n_sel=float(n_sel), jbits=jbits)
    return pl.pallas_call(
        kern,
        grid=(bsz, nq),
        in_specs=[pl.BlockSpec((tq, A_HEADS * A_LATENT), row),
                  pl.BlockSpec((tq, IDX_HEADS * IDX_DIM), row),
                  pl.BlockSpec((1, IDX_HEADS, tq), lambda b, i: (b, 0, i)),
                  pl.BlockSpec((1, 2, nblk, kb, LANES), lambda b, i: (b, 0, 0, 0, 0)),
                  pl.BlockSpec((1, nblk, A_LATENT, kb), lambda b, i: (b, 0, 0, 0)),
                  pl.BlockSpec((1, nblk, kb, A_LATENT), lambda b, i: (b, 0, 0, 0)),
                  _const_spec(bias.shape)],
        out_specs=pl.BlockSpec((tq, A_HEADS * A_LATENT), row),
        out_shape=jax.ShapeDtypeStruct((bsz * seq, A_HEADS * A_LATENT), BF16),
        scratch_shapes=[pltpu.VMEM((nblk, kb, tq), I32),
                        pltpu.VMEM((nblk, A_HEADS * tq, kb), F32),
                        pltpu.VMEM((A_HEADS * tq, LANES), F32),
                        pltpu.VMEM((A_HEADS * tq, LANES), F32),
                        pltpu.VMEM((A_HEADS * tq, A_LATENT), F32)],
        compiler_params=_cparams(("arbitrary", "arbitrary")),
        name="attention",
    )(q, iq, iwt, ik2, kvt, kvb, bias)


_SSD_L = 256
_PAIRS = SSM_HEADS // 2


def _ssd_kernel(z_ref, xbc_ref, dt_ref, dtt_ref, cw_ref, cb_ref, dtb_ref, dtbt_ref, al_ref, alt_ref,
                dsk_ref, nw_ref, o_ref, ext_ref, state_ref, y_ref):
    L = _SSD_L
    hd = SSM_HEADDIM

    @pl.when(pl.program_id(1) == 0)
    def _():
        ext_ref[0:8, :] = jnp.zeros((8, SSM_CONV_DIM), F32)
        state_ref[...] = jnp.zeros(state_ref.shape, F32)

    x = xbc_ref[...]
    ext_ref[8:8 + L, :] = x
    w = cw_ref[...]
    conv = x * w[3:4] + cb_ref[...]
    for k in range(1, SSM_CONV):
        conv = conv + ext_ref[8 - k:8 - k + L, :] * w[SSM_CONV - 1 - k:SSM_CONV - k]
    ext_ref[0:8, :] = x[L - 8:L]
    act = conv * jax.nn.sigmoid(conv)
    xs = act[:, :SSM_D_INNER]
    boff = SSM_D_INNER
    coff = SSM_D_INNER + SSM_GROUPS * SSM_STATE

    def softplus(v):
        return jnp.maximum(v, 0.0) + jnp.log1p(jnp.exp(-jnp.abs(v)))

    dt = softplus(dt_ref[...] + dtb_ref[...])
    dtt = softplus(dtt_ref[0] + dtbt_ref[...])
    a_col = dt * (-jnp.exp(al_ref[...]))
    a_row = dtt * (-jnp.exp(alt_ref[...]))
    ri = lax.broadcasted_iota(I32, (L, L), 0)
    ci = lax.broadcasted_iota(I32, (L, L), 1)
    causal = ci <= ri
    cs_col = jnp.dot(jnp.where(causal, 1.0, 0.0), a_col, precision=HI, preferred_element_type=F32)
    cs_row = jnp.dot(a_row, jnp.where(ri <= ci, 1.0, 0.0), precision=HI, preferred_element_type=F32)
    cs_last = cs_col[L - 1:L, :]
    lane = lax.broadcasted_iota(I32, (1, LANES), 1)
    lo = lane < hd
    sub = lax.broadcasted_iota(I32, (LANES, 1), 0)

    for g in range(SSM_GROUPS):
        bm = act[:, boff + g * SSM_STATE: boff + (g + 1) * SSM_STATE].astype(BF16)
        cm = act[:, coff + g * SSM_STATE: coff + (g + 1) * SSM_STATE].astype(BF16)
        cb = lax.dot_general(cm, bm, (((1,), (1,)), ((), ())), preferred_element_type=F32)
        for pp in range(_PAIRS // SSM_GROUPS):
            p = g * (_PAIRS // SSM_GROUPS) + pp
            h0, h1 = 2 * p, 2 * p + 1
            xp = xs[:, p * LANES:(p + 1) * LANES]
            dtl = jnp.where(lo, dt[:, h0:h0 + 1], dt[:, h1:h1 + 1])
            xdt = xp * dtl
            csl = jnp.where(lo, cs_col[:, h0:h0 + 1], cs_col[:, h1:h1 + 1])
            last = jnp.where(lo, cs_last[:, h0:h0 + 1], cs_last[:, h1:h1 + 1])
            ydiag = jnp.zeros((L, LANES), F32)
            for h, msk in ((h0, lo), (h1, jnp.logical_not(lo))):
                seg = cs_col[:, h:h + 1] - cs_row[h:h + 1, :]
                gm = (cb * jnp.exp(jnp.where(causal, seg, -jnp.inf))).astype(BF16)
                ydiag = ydiag + jnp.dot(gm, jnp.where(msk, xdt, 0.0).astype(BF16), preferred_element_type=F32)
            prev = state_ref[p]
            yoff = lax.dot_general(cm, prev.astype(BF16), (((1,), (1,)), ((), ())), preferred_element_type=F32)
            y_ref[:, p * LANES:(p + 1) * LANES] = ydiag + yoff * jnp.exp(csl) + xp * dsk_ref[:, p * LANES:(p + 1) * LANES]
            wx = (xdt * jnp.exp(last - csl)).astype(BF16)
            st = lax.dot_general(wx, bm, (((0,), (0,)), ((), ())), preferred_element_type=F32)
            cdec = jnp.where(sub < hd, jnp.exp(cs_last[:, h0:h0 + 1]), jnp.exp(cs_last[:, h1:h1 + 1]))
            state_ref[p] = prev * cdec + st

    z = z_ref[...]
    y = y_ref[...] * (z * jax.nn.sigmoid(z))
    gw = SSM_D_INNER // SSM_GROUPS
    for g in range(SSM_GROUPS):
        yg = y[:, g * gw:(g + 1) * gw]
        yg = yg * lax.rsqrt(jnp.mean(yg * yg, axis=-1, keepdims=True) + LN_EPS)
        o_ref[:, g * gw:(g + 1) * gw] = (yg * nw_ref[:, g * gw:(g + 1) * gw]).astype(o_ref.dtype)


def _ssd_call(z, xbc, dt, dtt, conv_w, conv_b, dt_bias, a_log, d_skip, norm_w, bsz, seq):
    L = _SSD_L
    nc = seq // L
    row = lambda b, c: (b * nc + c, 0)
    h = SSM_HEADS
    return pl.pallas_call(
        _ssd_kernel,
        grid=(bsz, nc),
        in_specs=[pl.BlockSpec((L, SSM_D_INNER), row),
                  pl.BlockSpec((L, SSM_CONV_DIM), row),
                  pl.BlockSpec((L, h), row),
                  pl.BlockSpec((1, h, L), lambda b, c: (b, 0, c)),
                  _const_spec((SSM_CONV, SSM_CONV_DIM)), _const_spec((1, SSM_CONV_DIM)),
                  _const_spec((1, h)), _const_spec((h, 1)), _const_spec((1, h)), _const_spec((h, 1)),
                  _const_spec((1, SSM_D_INNER)), _const_spec((1, SSM_D_INNER))],
        out_specs=pl.BlockSpec((L, SSM_D_INNER), row),
        out_shape=jax.ShapeDtypeStruct((bsz * seq, SSM_D_INNER), BF16),
        scratch_shapes=[pltpu.VMEM((L + 8, SSM_CONV_DIM), F32),
                        pltpu.VMEM((_PAIRS, LANES, SSM_STATE), F32),
                        pltpu.VMEM((L, SSM_D_INNER), F32)],
        compiler_params=_cparams(("arbitrary", "arbitrary")),
        name="ssd",
    )(z, xbc, dt, dtt, conv_w, conv_b.reshape(1, -1), dt_bias.reshape(1, h), dt_bias.reshape(h, 1),
      a_log.reshape(1, h), a_log.reshape(h, 1), jnp.repeat(d_skip, SSM_HEADDIM).reshape(1, -1),
      norm_w.reshape(1, -1))


_TM = 256


def _merge_kernel(oa_ref, ob_ref, ga_ref, gb_ref, x_ref, mod_ref, wpa_ref, wpb_ref, wo_ref, g1_ref, b1_ref,
                  wr_ref, br_ref, x1_ref, u2_ref, route_ref, gate_ref, cnt_ref, base_ref, *, alpha):
    tm = _TM

    @pl.when(pl.program_id(0) == 0)
    def _():
        base_ref[...] = jnp.zeros(base_ref.shape, F32)

    ma = jnp.dot(oa_ref[...], wpa_ref[...], preferred_element_type=F32)
    mb = jnp.dot(ob_ref[...], wpb_ref[...], preferred_element_type=F32)
    merged = jax.nn.sigmoid(ga_ref[...]) * ma + jax.nn.sigmoid(gb_ref[...]) * mb
    t = jnp.dot(merged.astype(BF16), wo_ref[...], preferred_element_type=F32)
    x1 = _ln(alpha * x_ref[...] + mod_ref[0, 2:3, :] * t) * g1_ref[...] + b1_ref[...]
    x1_ref[...] = x1
    u2 = _ln(x1) * (1.0 + mod_ref[0, 4:5, :]) + mod_ref[0, 3:4, :]
    u2_ref[...] = u2
    lane = lax.broadcasted_iota(I32, (tm, LANES), 1)
    lanef = lane.astype(F32)
    logits = jnp.dot(u2, wr_ref[...], precision=HI, preferred_element_type=F32) + br_ref[...]
    logits = jnp.where(lane < N_EXPERTS, logits, -jnp.inf)

    vals, ids = [], []
    for _ in range(TOPK_EXPERTS):
        m = jnp.max(logits, axis=1, keepdims=True)
        idx = jnp.min(jnp.where(logits == m, lanef, float(LANES)), axis=1, keepdims=True)
        vals.append(m)
        ids.append(idx)
        logits = jnp.where(lanef == idx, -jnp.inf, logits)
    es = [jnp.exp(v - vals[0]) for v in vals]
    den = es[0] + es[1] + es[2] + es[3]

    onehot = jnp.zeros((tm, LANES), F32)
    for idx in ids:
        onehot = onehot + jnp.where(lanef == idx, 1.0, 0.0)
    ri = lax.broadcasted_iota(I32, (tm, tm), 0)
    ci = lax.broadcasted_iota(I32, (tm, tm), 1)
    before = jnp.where(ci < ri, 1.0, 0.0).astype(BF16)
    prefix = jnp.dot(before, onehot.astype(BF16), preferred_element_type=F32) + base_ref[...]
    route = jnp.zeros((tm, LANES), F32)
    gates = jnp.zeros((tm, LANES), F32)
    for j in range(TOPK_EXPERTS):
        rank = jnp.sum(jnp.where(lanef == ids[j], prefix, 0.0), axis=1, keepdims=True)
        route = jnp.where(lane == j, ids[j], route)
        route = jnp.where(lane == TOPK_EXPERTS + j, rank, route)
        gates = jnp.where(lane == j, es[j] / den, gates)
    route_ref[...] = route.astype(I32)
    gate_ref[...] = gates
    base = base_ref[...] + jnp.sum(onehot, axis=0, keepdims=True)
    base_ref[...] = base
    cnt_ref[...] = jnp.broadcast_to(base, cnt_ref.shape)


def _merge_call(o_a, o_b, g_a, g_b, x2, mod3, wpa, wpb, wo, ln_g, ln_b, wr, br, seq, alpha):
    n_tok, d = x2.shape
    tm = _TM
    row = lambda i: (i, 0)
    blk = pl.BlockSpec((tm, d), row)
    sm = pl.BlockSpec((tm, LANES), row)
    return pl.pallas_call(
        functools.partial(_merge_kernel, alpha=alpha),
        grid=(n_tok // tm,),
        in_specs=[blk, blk, blk, blk, blk,
                  pl.BlockSpec((1, 6, d), lambda i: ((i * tm) // seq, 0, 0)),
                  _const_spec((d, d)), _const_spec((d, d)), _const_spec((d, d)),
                  _const_spec((1, d)), _const_spec((1, d)), _const_spec((d, LANES)), _const_spec((1, LANES))],
        out_specs=[blk, blk, sm, sm, pl.BlockSpec((8, LANES), lambda i: (0, 0))],
        out_shape=[jax.ShapeDtypeStruct((n_tok, d), F32), jax.ShapeDtypeStruct((n_tok, d), F32),
                   jax.ShapeDtypeStruct((n_tok, LANES), I32), jax.ShapeDtypeStruct((n_tok, LANES), F32),
                   jax.ShapeDtypeStruct((8, LANES), F32)],
        scratch_shapes=[pltpu.VMEM((1, LANES), F32)],
        compiler_params=_cparams(("arbitrary",)),
        name="merge",
    )(o_a, o_b, g_a, g_b, x2, mod3, wpa, wpb, wo, ln_g, ln_b, wr, br)


_TD = 256


def _dispatch_kernel(dest_ref, u2_ref, init_ref, xs_ref, sem):
    del init_ref

    def issue(t, carry):
        for j in range(TOPK_EXPERTS):
            d = dest_ref[t * TOPK_EXPERTS + j]
            pltpu.make_async_copy(u2_ref.at[pl.ds(t, 1), :], xs_ref.at[pl.ds(d, 1), :], sem).start()
        return carry

    lax.fori_loop(0, _TD, issue, 0)
    for _ in range(TOPK_EXPERTS):
        pltpu.make_async_copy(u2_ref, xs_ref.at[pl.ds(0, _TD), :], sem).wait()


def _dispatch_call(dest_flat, u2, n_slots):
    n_tok, d = u2.shape
    return pl.pallas_call(
        _dispatch_kernel,
        grid=(n_tok // _TD,),
        in_specs=[pl.BlockSpec((_TD * TOPK_EXPERTS,), lambda i: (i,), memory_space=pltpu.SMEM),
                  pl.BlockSpec((_TD, d), lambda i: (i, 0)),
                  pl.BlockSpec(memory_space=pl.ANY)],
        out_specs=pl.BlockSpec(memory_space=pl.ANY),
        out_shape=jax.ShapeDtypeStruct((n_slots, d), F32),
        scratch_shapes=[pltpu.SemaphoreType.DMA(())],
        input_output_aliases={2: 0},
        compiler_params=_cparams(("arbitrary",)),
        name="dispatch",
    )(dest_flat, u2, jnp.zeros((n_slots, d), F32))


_TMB = 512


def _expert_kernel(be_ref, nu_ref, xs_ref, w1_ref, b1_ref, w2_ref, b2_ref, y_ref):
    del be_ref
    i = pl.program_id(0)
    f = w2_ref.shape[1]

    @pl.when(i < nu_ref[0])
    def _():
        h = jnp.dot(xs_ref[...].astype(BF16), w1_ref[0], preferred_element_type=F32) + b1_ref[0]
        gate = jnp.minimum(h[:, :f], SWIGLU_LIMIT)
        up = jnp.clip(h[:, f:], -SWIGLU_LIMIT, SWIGLU_LIMIT)
        act = (up + 1.0) * gate * jax.nn.sigmoid(SWIGLU_ALPHA * gate)
        y_ref[...] = jnp.dot(act.astype(BF16), w2_ref[0], preferred_element_type=F32) + b2_ref[0]

    @pl.when(i >= nu_ref[0])
    def _():
        y_ref[...] = jnp.zeros(y_ref.shape, F32)


def _expert_call(block_expert, n_used, xs, w1, b1, w2, b2):
    n_slots, d = xs.shape
    ne, _, f2 = w1.shape
    f = f2 // 2
    grid_spec = pltpu.PrefetchScalarGridSpec(
        num_scalar_prefetch=2,
        grid=(n_slots // _TMB,),
        in_specs=[pl.BlockSpec((_TMB, d), lambda i, be, nu: (i, 0)),
                  pl.BlockSpec((1, d, f2), lambda i, be, nu: (be[i], 0, 0)),
                  pl.BlockSpec((1, 1, f2), lambda i, be, nu: (be[i], 0, 0)),
                  pl.BlockSpec((1, f, d), lambda i, be, nu: (be[i], 0, 0)),
                  pl.BlockSpec((1, 1, d), lambda i, be, nu: (be[i], 0, 0))],
        out_specs=pl.BlockSpec((_TMB, d), lambda i, be, nu: (i, 0)),
    )
    return pl.pallas_call(
        _expert_kernel,
        grid_spec=grid_spec,
        out_shape=jax.ShapeDtypeStruct((n_slots, d), F32),
        compiler_params=_cparams(("arbitrary",)),
        name="experts",
    )(block_expert, n_used, xs, w1, b1.reshape(ne, 1, f2), w2, b2.reshape(ne, 1, d))


_TC = 256


def _combine_kernel(dest_ref, gate_ref, x1_ref, mod_ref, g2_ref, b2_ref, y_hbm, o_ref, buf_ref, sem, *, alpha):
    def issue(t, carry):
        for j in range(TOPK_EXPERTS):
            d = dest_ref[t * TOPK_EXPERTS + j]
            pltpu.make_async_copy(y_hbm.at[pl.ds(d, 1), :], buf_ref.at[j, pl.ds(t, 1), :], sem).start()
        return carry

    lax.fori_loop(0, _TC, issue, 0)
    for j in range(TOPK_EXPERTS):
        pltpu.make_async_copy(y_hbm.at[pl.ds(0, _TC), :], buf_ref.at[j], sem).wait()
    gates = gate_ref[...]
    y = gates[:, 0:1] * buf_ref[0]
    for j in range(1, TOPK_EXPERTS):
        y = y + gates[:, j:j + 1] * buf_ref[j]
    o_ref[...] = _ln(alpha * x1_ref[...] + mod_ref[0, 5:6, :] * y) * g2_ref[...] + b2_ref[...]


def _combine_call(dest_flat, gates, x1, mod3, ln_g, ln_b, y, seq, alpha):
    n_tok, d = x1.shape
    tc = _TC
    row = lambda i: (i, 0)
    return pl.pallas_call(
        functools.partial(_combine_kernel, alpha=alpha),
        grid=(n_tok // tc,),
        in_specs=[pl.BlockSpec((tc * TOPK_EXPERTS,), lambda i: (i,), memory_space=pltpu.SMEM),
                  pl.BlockSpec((tc, LANES), row),
                  pl.BlockSpec((tc, d), row),
                  pl.BlockSpec((1, 6, d), lambda i: ((i * tc) // seq, 0, 0)),
                  _const_spec((1, d)), _const_spec((1, d)),
                  pl.BlockSpec(memory_space=pl.ANY)],
        out_specs=pl.BlockSpec((tc, d), row),
        out_shape=jax.ShapeDtypeStruct((n_tok, d), F32),
        scratch_shapes=[pltpu.VMEM((TOPK_EXPERTS, tc, d), F32), pltpu.SemaphoreType.DMA(())],
        compiler_params=_cparams(("arbitrary",)),
        name="combine",
    )(dest_flat, gates, x1, mod3, ln_g, ln_b, y)


def _permute_w_in(w):
    d = w.shape[0]
    s = np.cumsum([0, A_HEADS * A_LATENT, A_LATENT, IDX_HEADS * IDX_DIM, IDX_DIM, IDX_HEADS,
                   SSM_D_INNER, SSM_CONV_DIM, SSM_HEADS, d, d]).tolist()
    q, kv, iq, ik, iw, z, xbc, dt, ga, gb = [w[:, s[i]:s[i + 1]] for i in range(10)]
    pad = jnp.zeros((d, LANES - IDX_DIM - IDX_HEADS - SSM_HEADS), w.dtype)
    return jnp.concatenate([q, kv, iq, ik, iw, dt, pad, z, xbc, ga, gb], axis=1).astype(BF16)


def _pad_lanes(v, fill=0.0):
    return jnp.pad(v.reshape(1, -1), ((0, 0), (0, LANES - v.shape[-1])), constant_values=fill)


def kernel(x, c, w_mod, b_mod, w_in, kv_norm_w, idx_k_norm_w, idx_k_norm_b, rel_bias, conv_w, conv_b, dt_bias,
           a_log, d_skip, ssm_norm_w, w_proj_a, w_proj_b, w_out, ln1_g, ln1_b, w_router, b_router, w1, b1, w2, b2,
           ln2_g, ln2_b):
    bsz, seq, d = x.shape
    depth = w_mod.shape[0]
    alpha = (2.0 * depth) ** 0.25
    n_tok = bsz * seq
    n_asg = n_tok * TOPK_EXPERTS
    n_blocks = n_asg // _TMB + N_EXPERTS
    n_slots = n_blocks * _TMB
    nblk = seq // _KB
    bias = _bias_tiles(rel_bias)
    x2 = x.reshape(n_tok, d)
    for l in range(depth):
        mod3 = _mod_call(c, w_mod[l], b_mod[l]).reshape(bsz, 6, d)
        q, kvn, iq, small, z, xbc, g_a, g_b = _inproj_call(
            x2, mod3, _permute_w_in(w_in[l]), kv_norm_w[l].reshape(1, -1),
            _pad_lanes(idx_k_norm_w[l]), _pad_lanes(idx_k_norm_b[l]), seq)
        ik = small[:, :IDX_DIM].astype(BF16).reshape(bsz, nblk, _KB, IDX_DIM)
        zpad = jnp.zeros_like(ik)
        ik2 = jnp.stack([jnp.concatenate([ik, zpad], axis=3), jnp.concatenate([zpad, ik], axis=3)], axis=1)
        kvb = kvn.reshape(bsz, nblk, _KB, A_LATENT)
        kvt = jnp.swapaxes(kvb, 2, 3)
        iwt = jnp.swapaxes(small[:, _SM_IW:_SM_IW + IDX_HEADS].reshape(bsz, seq, IDX_HEADS), 1, 2)
        o_a = _attn_call(q, iq, iwt, ik2, kvt, kvb, bias, bsz, seq)
        dt = small[:, _SM_DT:_SM_DT + SSM_HEADS]
        dtt = jnp.swapaxes(dt.reshape(bsz, seq, SSM_HEADS), 1, 2)
        o_b = _ssd_call(z, xbc, dt, dtt, conv_w[l], conv_b[l], dt_bias[l], a_log[l], d_skip[l], ssm_norm_w[l],
                        bsz, seq)
        wr = jnp.pad(w_router[l], ((0, 0), (0, LANES - N_EXPERTS)))
        x1, u2, route, gates, cnt = _merge_call(
            o_a, o_b, g_a, g_b, x2, mod3, w_proj_a[l].astype(BF16), w_proj_b[l].astype(BF16),
            w_out[l].astype(BF16), ln1_g[l].reshape(1, -1), ln1_b[l].reshape(1, -1), wr, _pad_lanes(b_router[l]),
            seq, alpha)
        counts = cnt[0, :N_EXPERTS].astype(I32)
        padded = (counts + _TMB - 1) // _TMB * _TMB
        pends = jnp.cumsum(padded)
        pstarts = pends - padded
        eid = route[:, :TOPK_EXPERTS]
        dest_flat = (pstarts[eid] + route[:, TOPK_EXPERTS:2 * TOPK_EXPERTS]).reshape(n_asg)
        block_start = jnp.arange(n_blocks, dtype=I32) * _TMB
        block_expert = jnp.minimum(jnp.sum(block_start[:, None] >= pends[None, :], axis=1), N_EXPERTS - 1).astype(I32)
        n_used = (pends[-1:] // _TMB).astype(I32)
        xs = _dispatch_call(dest_flat, u2, n_slots)
        y = _expert_call(block_expert, n_used, xs, w1[l].astype(BF16), b1[l], w2[l].astype(BF16), b2[l])
        x2 = _combine_call(dest_flat, gates, x1, mod3, ln2_g[l].reshape(1, -1), ln2_b[l].reshape(1, -1), y, seq, alpha)
    return x2.reshape(bsz, seq, d)
```

```python
import functools
import math

import jax
import jax.numpy as jnp
import numpy as np
from jax import lax
from jax.experimental import pallas as pl
from jax.experimental.pallas import tpu as pltpu

F32 = jnp.float32
BF16 = jnp.bfloat16
I32 = jnp.int32
I16 = jnp.int16

CHUNK = 64
A_HEADS = 8
A_LATENT = 128
IDX_HEADS = 8
IDX_DIM = 64
TOPK_KEYS_MAX = 256
REL_BUCKETS = 32
REL_MAX_DIST = 128
SSM_D_INNER = 1024
SSM_HEADDIM = 64
SSM_HEADS = SSM_D_INNER // SSM_HEADDIM
SSM_GROUPS = 4
SSM_STATE = 128
SSM_CONV = 4
SSM_CONV_DIM = SSM_D_INNER + 2 * SSM_GROUPS * SSM_STATE
N_EXPERTS = 32
TOPK_EXPERTS = 4
SWIGLU_LIMIT = 7.0
SWIGLU_ALPHA = 1.702
LN_EPS = 1e-5

LANES = 128
INT_MIN = -2147483648
VMEM_LIMIT = 56 * 1024 * 1024

HI = lax.Precision.HIGHEST


def _cparams(sem):
    return pltpu.CompilerParams(dimension_semantics=sem, vmem_limit_bytes=VMEM_LIMIT)


def _ln(x):
    mu = jnp.mean(x, axis=-1, keepdims=True)
    xc = x - mu
    var = jnp.mean(xc * xc, axis=-1, keepdims=True)
    return xc * lax.rsqrt(var + LN_EPS)


def _const_spec(shape):
    nd = len(shape)
    return pl.BlockSpec(shape, lambda *_: (0,) * nd, pipeline_mode=pl.Buffered(1))


def _mod_kernel(c_ref, w_ref, b_ref, o_ref):
    c = c_ref[...]
    sc = c * jax.nn.sigmoid(c)
    o_ref[...] = jnp.dot(sc, w_ref[...], precision=HI, preferred_element_type=F32) + b_ref[...]


def _mod_call(c, w_mod, b_mod):
    bsz, d = c.shape
    n = w_mod.shape[1]
    tn = 1024
    return pl.pallas_call(
        _mod_kernel,
        grid=(n // tn,),
        in_specs=[pl.BlockSpec((bsz, d), lambda j: (0, 0)),
                  pl.BlockSpec((d, tn), lambda j: (0, j)),
                  pl.BlockSpec((1, tn), lambda j: (0, j))],
        out_specs=pl.BlockSpec((bsz, tn), lambda j: (0, j)),
        out_shape=jax.ShapeDtypeStruct((bsz, n), F32),
        compiler_params=_cparams(("arbitrary",)),
        name="mod",
    )(c, w_mod, b_mod.reshape(1, n))


_C_Q, _C_KV, _C_IQ, _C_SM, _C_Z, _C_XBC, _C_GA, _C_GB, _C_END = 0, 1024, 1152, 1664, 1792, 2816, 4864, 5888, 6912
_SM_IW = IDX_DIM
_SM_DT = IDX_DIM + IDX_HEADS


def _inproj_kernel(x_ref, mod_ref, w_ref, kvw_ref, ikw_ref, ikb_ref,
                   q_ref, kv_ref, iq_ref, sm_ref, z_ref, xbc_ref, ga_ref, gb_ref):
    u = _ln(x_ref[...]) * (1.0 + mod_ref[0, 1:2, :]) + mod_ref[0, 0:1, :]
    ub = u.astype(BF16)

    def mm(a, b):
        return jnp.dot(ub, w_ref[:, a:b], preferred_element_type=F32)

    q_ref[...] = mm(_C_Q, _C_KV).astype(BF16)
    kv = mm(_C_KV, _C_IQ)
    kv = kv * lax.rsqrt(jnp.mean(kv * kv, axis=-1, keepdims=True) + LN_EPS)
    kv_ref[...] = (kv * kvw_ref[...]).astype(BF16)
    iq_ref[...] = mm(_C_IQ, _C_SM).astype(BF16)
    g = mm(_C_SM, _C_Z)
    lane = lax.broadcasted_iota(I32, g.shape, 1)
    is_ik = lane < IDX_DIM
    mu = jnp.sum(jnp.where(is_ik, g, 0.0), axis=-1, keepdims=True) * (1.0 / IDX_DIM)
    gc = g - mu
    var = jnp.sum(jnp.where(is_ik, gc * gc, 0.0), axis=-1, keepdims=True) * (1.0 / IDX_DIM)
    ik = gc * lax.rsqrt(var + LN_EPS) * ikw_ref[...] + ikb_ref[...]
    sm_ref[...] = jnp.where(is_ik, ik, jnp.where(lane < _SM_DT, g * (IDX_HEADS ** -0.5), g))
    z_ref[...] = mm(_C_Z, _C_XBC)
    xbc_ref[...] = mm(_C_XBC, _C_GA)
    ga_ref[...] = mm(_C_GA, _C_GB)
    gb_ref[...] = mm(_C_GB, _C_END)


def _inproj_call(x2, mod3, w_perm, kvw, ikw, ikb, seq):
    n_tok, d = x2.shape
    tm = 256
    row = lambda i: (i, 0)

    def ospec(n):
        return pl.BlockSpec((tm, n), row)

    outs = [(1024, BF16), (128, BF16), (512, BF16), (128, F32), (1024, F32), (2048, F32), (1024, F32), (1024, F32)]
    return pl.pallas_call(
        _inproj_kernel,
        grid=(n_tok // tm,),
        in_specs=[pl.BlockSpec((tm, d), row),
                  pl.BlockSpec((1, 6, d), lambda i: ((i * tm) // seq, 0, 0)),
                  _const_spec(w_perm.shape), _const_spec((1, 128)), _const_spec((1, 128)), _const_spec((1, 128))],
        out_specs=[ospec(n) for n, _ in outs],
        out_shape=[jax.ShapeDtypeStruct((n_tok, n), dt) for n, dt in outs],
        compiler_params=_cparams(("arbitrary",)),
        name="inproj",
    )(x2, mod3, w_perm, kvw, ikw, ikb)


_TQ = 128
_KB = 256


def _t5_bucket(rel):
    half = REL_BUCKETS // 2
    max_exact = half // 2
    ret = (rel > 0).astype(jnp.int32) * half
    n = jnp.abs(rel)
    nf = jnp.maximum(n, 1).astype(jnp.float32)
    large = max_exact + (jnp.log(nf / max_exact) / math.log(REL_MAX_DIST / max_exact)
                         * (half - max_exact)).astype(jnp.int32)
    large = jnp.minimum(large, half - 1)
    return ret + jnp.where(n < max_exact, n, large)


def _bias_tiles(rel_bias):
    i = jnp.arange(_TQ, dtype=jnp.int32)[:, None]
    c = jnp.arange(_KB, dtype=jnp.int32)[None, :]
    rel = [c + delta - i for delta in (-2 * LANES, -LANES, 0)] + [jnp.full((_TQ, _KB), -REL_MAX_DIST, jnp.int32)]
    bucket = _t5_bucket(jnp.stack(rel))
    b = jnp.zeros((4, A_HEADS, _TQ, _KB), F32)
    for k in range(REL_BUCKETS):
        b = jnp.where((bucket == k)[:, None], rel_bias[k].astype(F32)[None, :, None, None], b)
    return b.reshape(4, A_HEADS * _TQ, _KB)


def _attn_kernel(q_ref, iq_ref, iwt_ref, ik2_ref, kvt_ref, kv_ref, bias_ref, o_ref,
                 keys_ref, hi_ref, lo_ref, s_ref, mx_ref, l_ref, acc_ref, *, n_sel, jbits):
    tq, kb = _TQ, _KB
    qs = pl.program_id(1) * tq
    nkb = (qs + tq + kb - 1) // kb
    nt = (((1,), (1,)), ((), ()))

    qpos = lax.broadcasted_iota(I32, (1, tq), 1) + qs
    qend = (qpos // CHUNK + 1) * CHUNK
    krow = lax.broadcasted_iota(I32, (kb, tq), 0)
    iq = iq_ref[...]
    iwt = iwt_ref[0] * (IDX_DIM ** -0.5)

    def score_body(j, carry):
        acc = jnp.zeros((kb, tq), F32)
        for p in range(IDX_HEADS // 2):
            pair = iq[:, p * LANES:(p + 1) * LANES]
            for par in range(2):
                h = 2 * p + par
                s = lax.dot_general(ik2_ref[0, par, j], pair, nt, preferred_element_type=F32)
                acc = acc + jnp.maximum(s, 0.0) * iwt[h:h + 1, :]
        bits = pltpu.bitcast(acc, I32)
        key = jnp.where(bits < 0, bits ^ 0x7FFFFFFF, bits)
        key = jnp.where(bits == INT_MIN, 0, key)
        key = jnp.where(krow + j * kb < qend, key, INT_MIN)
        keys_ref[j] = key
        hi_ref[j] = (key >> 16).astype(I16)
        return carry

    lax.fori_loop(0, nkb, score_body, 0)

    def count16(ref, pred):
        def body(j, acc):
            m = jnp.where(pred(ref[j]), jnp.int16(1), jnp.int16(0))
            parts = [m[16 * i:16 * i + 16] for i in range(kb // 16)]
            while len(parts) > 1:
                parts = [parts[i] + parts[i + 1] for i in range(0, len(parts), 2)]
            return acc + parts[0]
        acc = lax.fori_loop(0, nkb, body, jnp.zeros((16, tq), I16))
        return jnp.sum(acc.astype(F32), axis=0, keepdims=True)

    def search16(ref, offset):
        def bit_body(it, tu):
            cu = tu | (jnp.int32(1) << (15 - it))
            cs = (cu - 32768).astype(I16)
            cnt = offset + count16(ref, lambda v: v >= cs)
            return jnp.where(cnt >= n_sel, cu, tu)
        return lax.fori_loop(0, 16, bit_body, jnp.zeros((1, tq), I32))

    th = search16(hi_ref, 0.0) - 32768
    th16 = th.astype(I16)
    n_above = count16(hi_ref, lambda v: v > th16)

    def low_body(j, carry):
        key = keys_ref[j]
        lo_ref[j] = jnp.where((key >> 16) == th, (key & 0xFFFF) - 32768, -32768).astype(I16)
        return carry

    lax.fori_loop(0, nkb, low_body, 0)
    thr = (th << 16) + search16(lo_ref, n_above)

    def count(pred):
        def body(j, acc):
            m = jnp.where(pred(keys_ref[j], krow + j * kb), 1.0, 0.0)
            parts = [m[8 * i:8 * i + 8] for i in range(kb // 8)]
            while len(parts) > 1:
                parts = [parts[i] + parts[i + 1] for i in range(0, len(parts), 2)]
            return acc + parts[0]
        acc = lax.fori_loop(0, nkb, body, jnp.zeros((8, tq), F32))
        return jnp.sum(acc, axis=0, keepdims=True)

    n_ge = count(lambda k, kidx: k >= thr)

    def tie_search():
        need = n_sel - count(lambda k, kidx: k > thr)

        def j_body(it, jj):
            cj = jj | (jnp.int32(1) << (jbits - 1 - it))
            f = count(lambda k, kidx: (k == thr) & (kidx < cj))
            return jnp.where(f <= need, cj, jj)

        return lax.fori_loop(0, jbits, j_body, jnp.zeros((1, tq), I32))

    jj = lax.cond(jnp.max(n_ge) > n_sel, tie_search, lambda: jnp.full((1, tq), (1 << jbits) - 1, I32))

    q = q_ref[...]
    q_all = jnp.concatenate([q[:, h * LANES:(h + 1) * LANES] for h in range(A_HEADS)], axis=0)
    eye = jnp.where(lax.broadcasted_iota(I32, (tq, tq), 0) == lax.broadcasted_iota(I32, (tq, tq), 1),
                    1.0, 0.0).astype(BF16)
    mx_ref[...] = jnp.full(mx_ref.shape, -jnp.inf, F32)
    scale = A_LATENT ** -0.5

    def logit_body(j, carry):
        k = keys_ref[j]
        kidx = krow + j * kb
        selt = ((k > thr) | ((k == thr) & (kidx < jj))) & (kidx < qend)
        sel = lax.dot_general(eye, jnp.where(selt, 1.0, 0.0).astype(BF16), nt, preferred_element_type=F32) > 0.5
        s = jnp.dot(q_all, kvt_ref[0, j], preferred_element_type=F32) * scale
        delta = (j * kb - qs) // LANES
        s = s + bias_ref[jnp.where(delta < -2, 3, delta + 2)]
        s = jnp.concatenate([jnp.where(sel, s[h * tq:(h + 1) * tq], -jnp.inf) for h in range(A_HEADS)], axis=0)
        s_ref[j] = s
        mx = mx_ref[...]
        for c in range(kb // LANES):
            mx = jnp.maximum(mx, s[:, c * LANES:(c + 1) * LANES])
        mx_ref[...] = mx
        return carry

    lax.fori_loop(0, nkb, logit_body, 0)
    m = jnp.max(mx_ref[...], axis=1, keepdims=True)
    m = jnp.where(m == -jnp.inf, 0.0, m)
    mx_ref[...] = jnp.broadcast_to(m, mx_ref.shape)
    l_ref[...] = jnp.zeros(l_ref.shape, F32)
    acc_ref[...] = jnp.zeros(acc_ref.shape, F32)

    def pv_body(j, carry):
        mb = mx_ref[...]
        s = s_ref[j]
        ps = [jnp.exp(s[:, c * LANES:(c + 1) * LANES] - mb) for c in range(kb // LANES)]
        lsum = l_ref[...]
        for pc in ps:
            lsum = lsum + pc
        l_ref[...] = lsum
        p = jnp.concatenate(ps, axis=1).astype(BF16)
        acc_ref[...] += jnp.dot(p, kv_ref[0, j], preferred_element_type=F32)
        return carry

    lax.fori_loop(0, nkb, pv_body, 0)
    o = acc_ref[...] / jnp.sum(l_ref[...], axis=1, keepdims=True)
    for h in range(A_HEADS):
        o_ref[:, h * LANES:(h + 1) * LANES] = o[h * tq:(h + 1) * tq].astype(o_ref.dtype)


def _attn_call(q, iq, iwt, ik2, kvt, kvb, bias, bsz, seq):
    tq, kb = _TQ, _KB
    nq = seq // tq
    nblk = seq // kb
    n_sel = min(TOPK_KEYS_MAX, seq // 4)
    jbits = int(seq).bit_length()
    row = lambda b, i: (b * nq + i, 0)
    kern = functools.partial(_attn_kernel, n_sel=float(n_sel), jbits=jbits)
    return pl.pallas_call(
        kern,
        grid=(bsz, nq),
        in_specs=[pl.BlockSpec((tq, A_HEADS * A_LATENT), row),
                  pl.BlockSpec((tq, IDX_HEADS * IDX_DIM), row),
                  pl.BlockSpec((1, IDX_HEADS, tq), lambda b, i: (b, 0, i)),
                  pl.BlockSpec((1, 2, nblk, kb, LANES), lambda b, i: (b, 0, 0, 0, 0)),
                  pl.BlockSpec((1, nblk, A_LATENT, kb), lambda b, i: (b, 0, 0, 0)),
                  pl.BlockSpec((1, nblk, kb, A_LATENT), lambda b, i: (b, 0, 0, 0)),
                  _const_spec(bias.shape)],
        out_specs=pl.BlockSpec((tq, A_HEADS * A_LATENT), row),
        out_shape=jax.ShapeDtypeStruct((bsz * seq, A_HEADS * A_LATENT), BF16),
        scratch_shapes=[pltpu.VMEM((nblk, kb, tq), I32),
                        pltpu.VMEM((nblk, kb, tq), I16),
                        pltpu.VMEM((nblk, kb, tq), I16),
                        pltpu.VMEM((nblk, A_HEADS * tq, kb), F32),
                        pltpu.VMEM((A_HEADS * tq, LANES), F32),
                        pltpu.VMEM((A_HEADS * tq, LANES), F32),
                        pltpu.VMEM((A_HEADS * tq, A_LATENT), F32)],
        compiler_params=_cparams(("arbitrary", "arbitrary")),
        name="attention",
    )(q, iq, iwt, ik2, kvt, kvb, bias)


_SSD_L = 256
_PAIRS = SSM_HEADS // 2


def _ssd_kernel(z_ref, xbc_ref, dt_ref, dtt_ref, cw_ref, cb_ref, dtb_ref, dtbt_ref, al_ref, alt_ref,
                dsk_ref, nw_ref, o_ref, ext_ref, state_ref, y_ref):
    L = _SSD_L
    hd = SSM_HEADDIM

    @pl.when(pl.program_id(1) == 0)
    def _():
        ext_ref[0:8, :] = jnp.zeros((8, SSM_CONV_DIM), F32)
        state_ref[...] = jnp.zeros(state_ref.shape, F32)

    x = xbc_ref[...]
    ext_ref[8:8 + L, :] = x
    w = cw_ref[...]
    conv = x * w[3:4] + cb_ref[...]
    for k in range(1, SSM_CONV):
        conv = conv + ext_ref[8 - k:8 - k + L, :] * w[SSM_CONV - 1 - k:SSM_CONV - k]
    ext_ref[0:8, :] = x[L - 8:L]
    act = conv * jax.nn.sigmoid(conv)
    xs = act[:, :SSM_D_INNER]
    boff = SSM_D_INNER
    coff = SSM_D_INNER + SSM_GROUPS * SSM_STATE

    def softplus(v):
        return jnp.maximum(v, 0.0) + jnp.log1p(jnp.exp(-jnp.abs(v)))

    dt = softplus(dt_ref[...] + dtb_ref[...])
    dtt = softplus(dtt_ref[0] + dtbt_ref[...])
    a_col = dt * (-jnp.exp(al_ref[...]))
    a_row = dtt * (-jnp.exp(alt_ref[...]))
    ri = lax.broadcasted_iota(I32, (L, L), 0)
    ci = lax.broadcasted_iota(I32, (L, L), 1)
    causal = ci <= ri
    cs_col = jnp.dot(jnp.where(causal, 1.0, 0.0), a_col, precision=HI, preferred_element_type=F32)
    cs_row = jnp.dot(a_row, jnp.where(ri <= ci, 1.0, 0.0), precision=HI, preferred_element_type=F32)
    cs_last = cs_col[L - 1:L, :]
    lane = lax.broadcasted_iota(I32, (1, LANES), 1)
    lo = lane < hd
    sub = lax.broadcasted_iota(I32, (LANES, 1), 0)

    for g in range(SSM_GROUPS):
        bm = act[:, boff + g * SSM_STATE: boff + (g + 1) * SSM_STATE].astype(BF16)
        cm = act[:, coff + g * SSM_STATE: coff + (g + 1) * SSM_STATE].astype(BF16)
        cb = lax.dot_general(cm, bm, (((1,), (1,)), ((), ())), preferred_element_type=F32)
        for pp in range(_PAIRS // SSM_GROUPS):
            p = g * (_PAIRS // SSM_GROUPS) + pp
            h0, h1 = 2 * p, 2 * p + 1
            xp = xs[:, p * LANES:(p + 1) * LANES]
            dtl = jnp.where(lo, dt[:, h0:h0 + 1], dt[:, h1:h1 + 1])
            xdt = xp * dtl
            csl = jnp.where(lo, cs_col[:, h0:h0 + 1], cs_col[:, h1:h1 + 1])
            last = jnp.where(lo, cs_last[:, h0:h0 + 1], cs_last[:, h1:h1 + 1])
            ydiag = jnp.zeros((L, LANES), F32)
            for h, msk in ((h0, lo), (h1, jnp.logical_not(lo))):
                seg = cs_col[:, h:h + 1] - cs_row[h:h + 1, :]
                gm = (cb * jnp.exp(jnp.where(causal, seg, -jnp.inf))).astype(BF16)
                ydiag = ydiag + jnp.dot(gm, jnp.where(msk, xdt, 0.0).astype(BF16), preferred_element_type=F32)
            prev = state_ref[p]
            yoff = lax.dot_general(cm, prev.astype(BF16), (((1,), (1,)), ((), ())), preferred_element_type=F32)
            y_ref[:, p * LANES:(p + 1) * LANES] = ydiag + yoff * jnp.exp(csl) + xp * dsk_ref[:, p * LANES:(p + 1) * LANES]
            wx = (xdt * jnp.exp(last - csl)).astype(BF16)
            st = lax.dot_general(wx, bm, (((0,), (0,)), ((), ())), preferred_element_type=F32)
            cdec = jnp.where(sub < hd, jnp.exp(cs_last[:, h0:h0 + 1]), jnp.exp(cs_last[:, h1:h1 + 1]))
            state_ref[p] = prev * cdec + st

    z = z_ref[...]
    y = y_ref[...] * (z * jax.nn.sigmoid(z))
    gw = SSM_D_INNER // SSM_GROUPS
    for g in range(SSM_GROUPS):
        yg = y[:, g * gw:(g + 1) * gw]
        yg = yg * lax.rsqrt(jnp.mean(yg * yg, axis=-1, keepdims=True) + LN_EPS)
        o_ref[:, g * gw:(g + 1) * gw] = (yg * nw_ref[:, g * gw:(g + 1) * gw]).astype(o_ref.dtype)


def _ssd_call(z, xbc, dt, dtt, conv_w, conv_b, dt_bias, a_log, d_skip, norm_w, bsz, seq):
    L = _SSD_L
    nc = seq // L
    row = lambda b, c: (b * nc + c, 0)
    h = SSM_HEADS
    return pl.pallas_call(
        _ssd_kernel,
        grid=(bsz, nc),
        in_specs=[pl.BlockSpec((L, SSM_D_INNER), row),
                  pl.BlockSpec((L, SSM_CONV_DIM), row),
                  pl.BlockSpec((L, h), row),
                  pl.BlockSpec((1, h, L), lambda b, c: (b, 0, c)),
                  _const_spec((SSM_CONV, SSM_CONV_DIM)), _const_spec((1, SSM_CONV_DIM)),
                  _const_spec((1, h)), _const_spec((h, 1)), _const_spec((1, h)), _const_spec((h, 1)),
                  _const_spec((1, SSM_D_INNER)), _const_spec((1, SSM_D_INNER))],
        out_specs=pl.BlockSpec((L, SSM_D_INNER), row),
        out_shape=jax.ShapeDtypeStruct((bsz * seq, SSM_D_INNER), BF16),
        scratch_shapes=[pltpu.VMEM((L + 8, SSM_CONV_DIM), F32),
                        pltpu.VMEM((_PAIRS, LANES, SSM_STATE), F32),
                        pltpu.VMEM((L, SSM_D_INNER), F32)],
        compiler_params=_cparams(("arbitrary", "arbitrary")),
        name="ssd",
    )(z, xbc, dt, dtt, conv_w, conv_b.reshape(1, -1), dt_bias.reshape(1, h), dt_bias.reshape(h, 1),
      a_log.reshape(1, h), a_log.reshape(h, 1), jnp.repeat(d_skip, SSM_HEADDIM).reshape(1, -1),
      norm_w.reshape(1, -1))


_TM = 256


def _merge_kernel(oa_ref, ob_ref, ga_ref, gb_ref, x_ref, mod_ref, wpa_ref, wpb_ref, wo_ref, g1_ref, b1_ref,
                  wrh_ref, wrl_ref, br_ref, x1_ref, u2_ref, route_ref, gate_ref, cnt_ref, base_ref, *, alpha):
    tm = _TM
    ne = N_EXPERTS

    @pl.when(pl.program_id(0) == 0)
    def _():
        base_ref[...] = jnp.zeros(base_ref.shape, F32)

    ma = jnp.dot(oa_ref[...], wpa_ref[...], preferred_element_type=F32)
    mb = jnp.dot(ob_ref[...], wpb_ref[...], preferred_element_type=F32)
    merged = jax.nn.sigmoid(ga_ref[...]) * ma + jax.nn.sigmoid(gb_ref[...]) * mb
    t = jnp.dot(merged.astype(BF16), wo_ref[...], preferred_element_type=F32)
    x1 = _ln(alpha * x_ref[...] + mod_ref[0, 2:3, :] * t) * g1_ref[...] + b1_ref[...]
    x1_ref[...] = x1
    u2 = _ln(x1) * (1.0 + mod_ref[0, 4:5, :]) + mod_ref[0, 3:4, :]
    u2_ref[...] = u2
    nt = (((1,), (1,)), ((), ()))
    uh = u2.astype(BF16)
    ul = (u2 - uh.astype(F32)).astype(BF16)
    wh, wl = wrh_ref[...], wrl_ref[...]
    logits = (lax.dot_general(wh, uh, nt, preferred_element_type=F32)
              + lax.dot_general(wl, uh, nt, preferred_element_type=F32)
              + lax.dot_general(wh, ul, nt, preferred_element_type=F32)) + br_ref[...]
    eio = lax.broadcasted_iota(I32, (ne, tm), 0).astype(F32)
    vals, ids = [], []
    for _ in range(TOPK_EXPERTS):
        m = jnp.max(logits, axis=0, keepdims=True)
        idx = jnp.min(jnp.where(logits == m, eio, float(ne)), axis=0, keepdims=True)
        vals.append(m)
        ids.append(idx)
        logits = jnp.where(eio == idx, -jnp.inf, logits)
    es = [jnp.exp(v - vals[0]) for v in vals]
    den = es[0] + es[1] + es[2] + es[3]

    onehot = jnp.zeros((ne, tm), F32)
    for idx in ids:
        onehot = onehot + jnp.where(eio == idx, 1.0, 0.0)
    ri = lax.broadcasted_iota(I32, (tm, tm), 0)
    ci = lax.broadcasted_iota(I32, (tm, tm), 1)
    before = jnp.where(ri < ci, 1.0, 0.0).astype(BF16)
    base = base_ref[...]
    prefix = jnp.dot(onehot.astype(BF16), before, preferred_element_type=F32) + base
    sub = lax.broadcasted_iota(I32, (8, tm), 0)
    route = jnp.zeros((8, tm), F32)
    gates = jnp.zeros((8, tm), F32)
    for j in range(TOPK_EXPERTS):
        rank = jnp.sum(jnp.where(eio == ids[j], prefix, 0.0), axis=0, keepdims=True)
        route = jnp.where(sub == j, ids[j], route)
        route = jnp.where(sub == TOPK_EXPERTS + j, rank, route)
        gates = jnp.where(sub == j, es[j] / den, gates)
    route_ref[...] = route.astype(I32)
    gate_ref[...] = gates
    base = base + jnp.sum(onehot, axis=1, keepdims=True)
    base_ref[...] = base
    cnt_ref[...] = jnp.broadcast_to(base, cnt_ref.shape)


def _merge_call(o_a, o_b, g_a, g_b, x2, mod3, wpa, wpb, wo, ln_g, ln_b, w_router, b_router, seq, alpha):
    n_tok, d = x2.shape
    tm = _TM
    ne = N_EXPERTS
    row = lambda i: (i, 0)
    blk = pl.BlockSpec((tm, d), row)
    sm = pl.BlockSpec((8, tm), lambda i: (0, i))
    wrt = w_router.T
    wrh = wrt.astype(BF16)
    wrl = (wrt - wrh.astype(F32)).astype(BF16)
    return pl.pallas_call(
        functools.partial(_merge_kernel, alpha=alpha),
        grid=(n_tok // tm,),
        in_specs=[blk, blk, blk, blk, blk,
                  pl.BlockSpec((1, 6, d), lambda i: ((i * tm) // seq, 0, 0)),
                  _const_spec((d, d)), _const_spec((d, d)), _const_spec((d, d)),
                  _const_spec((1, d)), _const_spec((1, d)), _const_spec((ne, d)), _const_spec((ne, d)),
                  _const_spec((ne, 1))],
        out_specs=[blk, blk, sm, sm, pl.BlockSpec((ne, LANES), lambda i: (0, 0))],
        out_shape=[jax.ShapeDtypeStruct((n_tok, d), F32), jax.ShapeDtypeStruct((n_tok, d), F32),
                   jax.ShapeDtypeStruct((8, n_tok), I32), jax.ShapeDtypeStruct((8, n_tok), F32),
                   jax.ShapeDtypeStruct((ne, LANES), F32)],
        scratch_shapes=[pltpu.VMEM((ne, 1), F32)],
        compiler_params=_cparams(("arbitrary",)),
        name="merge",
    )(o_a, o_b, g_a, g_b, x2, mod3, wpa, wpb, wo, ln_g, ln_b, wrh, wrl, b_router.reshape(ne, 1))


_TD = 256
_TMB = 512


def _dispatch_kernel(pend_ref, dest_ref, u2_ref, xs_ref, zero_ref, sem):
    @pl.when(pl.program_id(0) == 0)
    def _():
        zero_ref[...] = jnp.zeros(zero_ref.shape, F32)
        for e in range(N_EXPERTS):
            end = pend_ref[e]
            start = pend_ref[e - 1] if e else 0

            @pl.when(end > start)
            def _():
                dst = xs_ref.at[pl.ds(pl.multiple_of(end - _TMB, _TMB), _TMB), :]
                cp = pltpu.make_async_copy(zero_ref, dst, sem)
                cp.start()
                cp.wait()

    def issue(t, carry):
        for j in range(TOPK_EXPERTS):
            d = dest_ref[t * TOPK_EXPERTS + j]
            pltpu.make_async_copy(u2_ref.at[pl.ds(t, 1), :], xs_ref.at[pl.ds(d, 1), :], sem).start()
        return carry

    lax.fori_loop(0, _TD, issue, 0)
    for _ in range(TOPK_EXPERTS):
        pltpu.make_async_copy(u2_ref, xs_ref.at[pl.ds(0, _TD), :], sem).wait()


def _dispatch_call(pends, dest_flat, u2, n_slots):
    n_tok, d = u2.shape
    grid_spec = pltpu.PrefetchScalarGridSpec(
        num_scalar_prefetch=1,
        grid=(n_tok // _TD,),
        in_specs=[pl.BlockSpec((_TD * TOPK_EXPERTS,), lambda i, pe: (i,), memory_space=pltpu.SMEM),
                  pl.BlockSpec((_TD, d), lambda i, pe: (i, 0))],
        out_specs=pl.BlockSpec(memory_space=pl.ANY),
        scratch_shapes=[pltpu.VMEM((_TMB, d), F32), pltpu.SemaphoreType.DMA(())],
    )
    return pl.pallas_call(
        _dispatch_kernel,
        grid_spec=grid_spec,
        out_shape=jax.ShapeDtypeStruct((n_slots, d), F32),
        compiler_params=_cparams(("arbitrary",)),
        name="dispatch",
    )(pends, dest_flat, u2)


def _expert_kernel(be_ref, nu_ref, xs_ref, w1_ref, b1_ref, w2_ref, b2_ref, y_ref):
    del be_ref
    i = pl.program_id(0)
    f = w2_ref.shape[1]

    @pl.when(i < nu_ref[0])
    def _():
        h = jnp.dot(xs_ref[...].astype(BF16), w1_ref[0], preferred_element_type=F32) + b1_ref[0]
        gate = jnp.minimum(h[:, :f], SWIGLU_LIMIT)
        up = jnp.clip(h[:, f:], -SWIGLU_LIMIT, SWIGLU_LIMIT)
        act = (up + 1.0) * gate * jax.nn.sigmoid(SWIGLU_ALPHA * gate)
        y_ref[...] = jnp.dot(act.astype(BF16), w2_ref[0], preferred_element_type=F32) + b2_ref[0]

    @pl.when(i >= nu_ref[0])
    def _():
        y_ref[...] = jnp.zeros(y_ref.shape, F32)


def _expert_call(block_expert, n_used, xs, w1, b1, w2, b2):
    n_slots, d = xs.shape
    ne, _, f2 = w1.shape
    f = f2 // 2
    grid_spec = pltpu.PrefetchScalarGridSpec(
        num_scalar_prefetch=2,
        grid=(n_slots // _TMB,),
        in_specs=[pl.BlockSpec((_TMB, d), lambda i, be, nu: (i, 0)),
                  pl.BlockSpec((1, d, f2), lambda i, be, nu: (be[i], 0, 0)),
                  pl.BlockSpec((1, 1, f2), lambda i, be, nu: (be[i], 0, 0)),
                  pl.BlockSpec((1, f, d), lambda i, be, nu: (be[i], 0, 0)),
                  pl.BlockSpec((1, 1, d), lambda i, be, nu: (be[i], 0, 0))],
        out_specs=pl.BlockSpec((_TMB, d), lambda i, be, nu: (i, 0)),
    )
    return pl.pallas_call(
        _expert_kernel,
        grid_spec=grid_spec,
        out_shape=jax.ShapeDtypeStruct((n_slots, d), F32),
        compiler_params=_cparams(("arbitrary",)),
        name="experts",
    )(block_expert, n_used, xs, w1, b1.reshape(ne, 1, f2), w2, b2.reshape(ne, 1, d))


_TC = 256


def _combine_kernel(dest_ref, gate_ref, x1_ref, mod_ref, g2_ref, b2_ref, y_hbm, o_ref, buf_ref, sem, *, alpha):
    def issue(t, carry):
        for j in range(TOPK_EXPERTS):
            d = dest_ref[t * TOPK_EXPERTS + j]
            pltpu.make_async_copy(y_hbm.at[pl.ds(d, 1), :], buf_ref.at[j, pl.ds(t, 1), :], sem).start()
        return carry

    lax.fori_loop(0, _TC, issue, 0)
    for j in range(TOPK_EXPERTS):
        pltpu.make_async_copy(y_hbm.at[pl.ds(0, _TC), :], buf_ref.at[j], sem).wait()
    gates = gate_ref[...]
    y = gates[:, 0:1] * buf_ref[0]
    for j in range(1, TOPK_EXPERTS):
        y = y + gates[:, j:j + 1] * buf_ref[j]
    o_ref[...] = _ln(alpha * x1_ref[...] + mod_ref[0, 5:6, :] * y) * g2_ref[...] + b2_ref[...]


def _combine_call(dest_flat, gates, x1, mod3, ln_g, ln_b, y, seq, alpha):
    n_tok, d = x1.shape
    tc = _TC
    row = lambda i: (i, 0)
    return pl.pallas_call(
        functools.partial(_combine_kernel, alpha=alpha),
        grid=(n_tok // tc,),
        in_specs=[pl.BlockSpec((tc * TOPK_EXPERTS,), lambda i: (i,), memory_space=pltpu.SMEM),
                  pl.BlockSpec((tc, TOPK_EXPERTS), row),
                  pl.BlockSpec((tc, d), row),
                  pl.BlockSpec((1, 6, d), lambda i: ((i * tc) // seq, 0, 0)),
                  _const_spec((1, d)), _const_spec((1, d)),
                  pl.BlockSpec(memory_space=pl.ANY)],
        out_specs=pl.BlockSpec((tc, d), row),
        out_shape=jax.ShapeDtypeStruct((n_tok, d), F32),
        scratch_shapes=[pltpu.VMEM((TOPK_EXPERTS, tc, d), F32), pltpu.SemaphoreType.DMA(())],
        compiler_params=_cparams(("arbitrary",)),
        name="combine",
    )(dest_flat, gates, x1, mod3, ln_g, ln_b, y)


def _permute_w_in(w):
    d = w.shape[0]
    s = np.cumsum([0, A_HEADS * A_LATENT, A_LATENT, IDX_HEADS * IDX_DIM, IDX_DIM, IDX_HEADS,
                   SSM_D_INNER, SSM_CONV_DIM, SSM_HEADS, d, d]).tolist()
    q, kv, iq, ik, iw, z, xbc, dt, ga, gb = [w[:, s[i]:s[i + 1]] for i in range(10)]
    pad = jnp.zeros((d, LANES - IDX_DIM - IDX_HEADS - SSM_HEADS), w.dtype)
    return jnp.concatenate([q, kv, iq, ik, iw, dt, pad, z, xbc, ga, gb], axis=1).astype(BF16)


def _pad_lanes(v, fill=0.0):
    return jnp.pad(v.reshape(1, -1), ((0, 0), (0, LANES - v.shape[-1])), constant_values=fill)


def kernel(x, c, w_mod, b_mod, w_in, kv_norm_w, idx_k_norm_w, idx_k_norm_b, rel_bias, conv_w, conv_b, dt_bias,
           a_log, d_skip, ssm_norm_w, w_proj_a, w_proj_b, w_out, ln1_g, ln1_b, w_router, b_router, w1, b1, w2, b2,
           ln2_g, ln2_b):
    bsz, seq, d = x.shape
    depth = w_mod.shape[0]
    alpha = (2.0 * depth) ** 0.25
    n_tok = bsz * seq
    n_asg = n_tok * TOPK_EXPERTS
    n_blocks = n_asg // _TMB + N_EXPERTS
    n_slots = n_blocks * _TMB
    nblk = seq // _KB
    bias = _bias_tiles(rel_bias)
    x2 = x.reshape(n_tok, d)
    for l in range(depth):
        mod3 = _mod_call(c, w_mod[l], b_mod[l]).reshape(bsz, 6, d)
        q, kvn, iq, small, z, xbc, g_a, g_b = _inproj_call(
            x2, mod3, _permute_w_in(w_in[l]), kv_norm_w[l].reshape(1, -1),
            _pad_lanes(idx_k_norm_w[l]), _pad_lanes(idx_k_norm_b[l]), seq)
        ik = small[:, :IDX_DIM].astype(BF16).reshape(bsz, nblk, _KB, IDX_DIM)
        zpad = jnp.zeros_like(ik)
        ik2 = jnp.stack([jnp.concatenate([ik, zpad], axis=3), jnp.concatenate([zpad, ik], axis=3)], axis=1)
        kvb = kvn.reshape(bsz, nblk, _KB, A_LATENT)
        kvt = jnp.swapaxes(kvb, 2, 3)
        iwt = jnp.swapaxes(small[:, _SM_IW:_SM_IW + IDX_HEADS].reshape(bsz, seq, IDX_HEADS), 1, 2)
        o_a = _attn_call(q, iq, iwt, ik2, kvt, kvb, bias, bsz, seq)
        dt = small[:, _SM_DT:_SM_DT + SSM_HEADS]
        dtt = jnp.swapaxes(dt.reshape(bsz, seq, SSM_HEADS), 1, 2)
        o_b = _ssd_call(z, xbc, dt, dtt, conv_w[l], conv_b[l], dt_bias[l], a_log[l], d_skip[l], ssm_norm_w[l],
                        bsz, seq)
        x1, u2, route, gates, cnt = _merge_call(
            o_a, o_b, g_a, g_b, x2, mod3, w_proj_a[l].astype(BF16), w_proj_b[l].astype(BF16),
            w_out[l].astype(BF16), ln1_g[l].reshape(1, -1), ln1_b[l].reshape(1, -1), w_router[l], b_router[l],
            seq, alpha)
        counts = cnt[:, 0].astype(I32)
        padded = (counts + _TMB - 1) // _TMB * _TMB
        pends = jnp.cumsum(padded).astype(I32)
        pstarts = pends - padded
        eid = route[:TOPK_EXPERTS]
        onehot = eid[:, :, None] == jnp.arange(N_EXPERTS, dtype=I32)
        dest = jnp.sum(jnp.where(onehot, pstarts, 0), axis=-1) + route[TOPK_EXPERTS:2 * TOPK_EXPERTS]
        dest_flat = dest.T.reshape(n_asg)
        block_start = jnp.arange(n_blocks, dtype=I32) * _TMB
        block_expert = jnp.minimum(jnp.sum(block_start[:, None] >= pends[None, :], axis=1), N_EXPERTS - 1).astype(I32)
        n_used = (pends[-1:] // _TMB).astype(I32)
        xs = _dispatch_call(pends, dest_flat, u2, n_slots)
        y = _expert_call(block_expert, n_used, xs, w1[l].astype(BF16), b1[l], w2[l].astype(BF16), b2[l])
        x2 = _combine_call(dest_flat, gates[:TOPK_EXPERTS].T, x1, mod3, ln2_g[l].reshape(1, -1),
                           ln2_b[l].reshape(1, -1), y, seq, alpha)
    return x2.reshape(bsz, seq, d)
```

```python
import functools
import math

import jax
import jax.numpy as jnp
import numpy as np
from jax import lax
from jax.experimental import pallas as pl
from jax.experimental.pallas import tpu as pltpu

F32 = jnp.float32
BF16 = jnp.bfloat16
I32 = jnp.int32

CHUNK = 64
A_HEADS = 8
A_LATENT = 128
IDX_HEADS = 8
IDX_DIM = 64
TOPK_KEYS_MAX = 256
REL_BUCKETS = 32
REL_MAX_DIST = 128
SSM_D_INNER = 1024
SSM_HEADDIM = 64
SSM_HEADS = SSM_D_INNER // SSM_HEADDIM
SSM_GROUPS = 4
SSM_STATE = 128
SSM_CONV = 4
SSM_CONV_DIM = SSM_D_INNER + 2 * SSM_GROUPS * SSM_STATE
N_EXPERTS = 32
TOPK_EXPERTS = 4
SWIGLU_LIMIT = 7.0
SWIGLU_ALPHA = 1.702
LN_EPS = 1e-5

LANES = 128
INT_MIN = -2147483648
VMEM_LIMIT = 56 * 1024 * 1024

HI = lax.Precision.HIGHEST


def _cparams(sem):
    return pltpu.CompilerParams(dimension_semantics=sem, vmem_limit_bytes=VMEM_LIMIT)


def _ln(x):
    mu = jnp.mean(x, axis=-1, keepdims=True)
    xc = x - mu
    var = jnp.mean(xc * xc, axis=-1, keepdims=True)
    return xc * lax.rsqrt(var + LN_EPS)


def _const_spec(shape):
    nd = len(shape)
    return pl.BlockSpec(shape, lambda *_: (0,) * nd, pipeline_mode=pl.Buffered(1))


def _mod_kernel(c_ref, w_ref, b_ref, o_ref):
    c = c_ref[...]
    sc = c * jax.nn.sigmoid(c)
    o_ref[...] = jnp.dot(sc, w_ref[...], precision=HI, preferred_element_type=F32) + b_ref[...]


def _mod_call(c, w_mod, b_mod):
    bsz, d = c.shape
    n = w_mod.shape[1]
    tn = 1024
    return pl.pallas_call(
        _mod_kernel,
        grid=(n // tn,),
        in_specs=[pl.BlockSpec((bsz, d), lambda j: (0, 0)),
                  pl.BlockSpec((d, tn), lambda j: (0, j)),
                  pl.BlockSpec((1, tn), lambda j: (0, j))],
        out_specs=pl.BlockSpec((bsz, tn), lambda j: (0, j)),
        out_shape=jax.ShapeDtypeStruct((bsz, n), F32),
        compiler_params=_cparams(("arbitrary",)),
        name="mod",
    )(c, w_mod, b_mod.reshape(1, n))


_C_Q, _C_KV, _C_IQ, _C_SM, _C_Z, _C_XBC, _C_GA, _C_GB, _C_END = 0, 1024, 1152, 1664, 1792, 2816, 4864, 5888, 6912
_SM_IW = IDX_DIM
_SM_DT = IDX_DIM + IDX_HEADS


def _inproj_kernel(x_ref, mod_ref, w_ref, kvw_ref, ikw_ref, ikb_ref,
                   q_ref, kv_ref, iq_ref, sm_ref, z_ref, xbc_ref, ga_ref, gb_ref):
    u = _ln(x_ref[...]) * (1.0 + mod_ref[0, 1:2, :]) + mod_ref[0, 0:1, :]
    ub = u.astype(BF16)

    def mm(a, b):
        return jnp.dot(ub, w_ref[:, a:b], preferred_element_type=F32)

    q_ref[...] = mm(_C_Q, _C_KV).astype(BF16)
    kv = mm(_C_KV, _C_IQ)
    kv = kv * lax.rsqrt(jnp.mean(kv * kv, axis=-1, keepdims=True) + LN_EPS)
    kv_ref[...] = (kv * kvw_ref[...]).astype(BF16)
    iq_ref[...] = mm(_C_IQ, _C_SM).astype(BF16)
    g = mm(_C_SM, _C_Z)
    lane = lax.broadcasted_iota(I32, g.shape, 1)
    is_ik = lane < IDX_DIM
    mu = jnp.sum(jnp.where(is_ik, g, 0.0), axis=-1, keepdims=True) * (1.0 / IDX_DIM)
    gc = g - mu
    var = jnp.sum(jnp.where(is_ik, gc * gc, 0.0), axis=-1, keepdims=True) * (1.0 / IDX_DIM)
    ik = gc * lax.rsqrt(var + LN_EPS) * ikw_ref[...] + ikb_ref[...]
    sm_ref[...] = jnp.where(is_ik, ik, jnp.where(lane < _SM_DT, g * (IDX_HEADS ** -0.5), g))
    z_ref[...] = mm(_C_Z, _C_XBC)
    xbc_ref[...] = mm(_C_XBC, _C_GA)
    ga_ref[...] = mm(_C_GA, _C_GB)
    gb_ref[...] = mm(_C_GB, _C_END)


def _inproj_call(x2, mod3, w_perm, kvw, ikw, ikb, seq):
    n_tok, d = x2.shape
    tm = 256
    row = lambda i: (i, 0)

    def ospec(n):
        return pl.BlockSpec((tm, n), row)

    outs = [(1024, BF16), (128, BF16), (512, BF16), (128, F32), (1024, F32), (2048, F32), (1024, F32), (1024, F32)]
    return pl.pallas_call(
        _inproj_kernel,
        grid=(n_tok // tm,),
        in_specs=[pl.BlockSpec((tm, d), row),
                  pl.BlockSpec((1, 6, d), lambda i: ((i * tm) // seq, 0, 0)),
                  _const_spec(w_perm.shape), _const_spec((1, 128)), _const_spec((1, 128)), _const_spec((1, 128))],
        out_specs=[ospec(n) for n, _ in outs],
        out_shape=[jax.ShapeDtypeStruct((n_tok, n), dt) for n, dt in outs],
        compiler_params=_cparams(("arbitrary",)),
        name="inproj",
    )(x2, mod3, w_perm, kvw, ikw, ikb)


_TQ = 256
_KB = 256
_BIAS_STEP = math.gcd(_TQ, _KB)
_NEAR_DELTAS = tuple(range(-_KB, 1, _BIAS_STEP))
assert REL_MAX_DIST <= _BIAS_STEP


def _t5_bucket(rel):
    half = REL_BUCKETS // 2
    max_exact = half // 2
    ret = (rel > 0).astype(jnp.int32) * half
    n = jnp.abs(rel)
    nf = jnp.maximum(n, 1).astype(jnp.float32)
    large = max_exact + (jnp.log(nf / max_exact) / math.log(REL_MAX_DIST / max_exact)
                         * (half - max_exact)).astype(jnp.int32)
    large = jnp.minimum(large, half - 1)
    return ret + jnp.where(n < max_exact, n, large)


def _bias_tiles(rel_bias):
    i = jnp.arange(_TQ, dtype=jnp.int32)[:, None]
    c = jnp.arange(_KB, dtype=jnp.int32)[None, :]
    rel = [c + delta - i for delta in _NEAR_DELTAS] + [jnp.full((_TQ, _KB), -REL_MAX_DIST, jnp.int32)]
    bucket = _t5_bucket(jnp.stack(rel))
    b = jnp.zeros((len(rel), A_HEADS, _TQ, _KB), F32)
    for k in range(REL_BUCKETS):
        b = jnp.where((bucket == k)[:, None], rel_bias[k].astype(F32)[None, :, None, None], b)
    return b.reshape(len(rel), A_HEADS * _TQ, _KB)


def _attn_kernel(q_ref, iq_ref, iwt_ref, ik2_ref, kvt_ref, kv_ref, bias_ref, o_ref,
                 keys_ref, s_ref, mx_ref, l_ref, acc_ref, *, n_sel, jbits):
    tq, kb = _TQ, _KB
    qs = pl.program_id(1) * tq
    nkb = (qs + tq + kb - 1) // kb
    nt = (((1,), (1,)), ((), ()))

    qpos = lax.broadcasted_iota(I32, (1, tq), 1) + qs
    qend = (qpos // CHUNK + 1) * CHUNK
    krow = lax.broadcasted_iota(I32, (kb, tq), 0)
    iq = iq_ref[...]
    iwt = iwt_ref[0] * (IDX_DIM ** -0.5)

    def score_body(j, carry):
        acc = jnp.zeros((kb, tq), F32)
        for p in range(IDX_HEADS // 2):
            pair = iq[:, p * LANES:(p + 1) * LANES]
            for par in range(2):
                h = 2 * p + par
                s = lax.dot_general(ik2_ref[0, par, j], pair, nt, preferred_element_type=F32)
                acc = acc + jnp.maximum(s, 0.0) * iwt[h:h + 1, :]
        bits = pltpu.bitcast(acc, I32)
        key = jnp.where(bits < 0, bits ^ 0x7FFFFFFF, bits)
        key = jnp.where(bits == INT_MIN, 0, key)
        keys_ref[j] = jnp.where(krow + j * kb < qend, key, INT_MIN)
        return carry

    lax.fori_loop(0, nkb, score_body, 0)

    def count(pred):
        def body(j, acc):
            m = jnp.where(pred(keys_ref[j], krow + j * kb), 1.0, 0.0)
            parts = [m[8 * i:8 * i + 8] for i in range(kb // 8)]
            while len(parts) > 1:
                parts = [parts[i] + parts[i + 1] for i in range(0, len(parts), 2)]
            return acc + parts[0]
        acc = lax.fori_loop(0, nkb, body, jnp.zeros((8, tq), F32))
        return jnp.sum(acc, axis=0, keepdims=True)

    def bit_body(it, tu):
        cu = tu | (jnp.int32(1) << (31 - it))
        cs = cu ^ INT_MIN
        cnt = count(lambda k, kidx: k >= cs)
        return jnp.where(cnt >= n_sel, cu, tu)

    thr = lax.fori_loop(0, 32, bit_body, jnp.zeros((1, tq), I32)) ^ INT_MIN
    n_ge = count(lambda k, kidx: k >= thr)

    def tie_search():
        need = n_sel - count(lambda k, kidx: k > thr)

        def j_body(it, jj):
            cj = jj | (jnp.int32(1) << (jbits - 1 - it))
            f = count(lambda k, kidx: (k == thr) & (kidx < cj))
            return jnp.where(f <= need, cj, jj)

        return lax.fori_loop(0, jbits, j_body, jnp.zeros((1, tq), I32))

    jj = lax.cond(jnp.max(n_ge) > n_sel, tie_search, lambda: jnp.full((1, tq), (1 << jbits) - 1, I32))

    q = q_ref[...]
    q_all = jnp.concatenate([q[:, h * LANES:(h + 1) * LANES] for h in range(A_HEADS)], axis=0)
    eye = jnp.where(lax.broadcasted_iota(I32, (tq, tq), 0) == lax.broadcasted_iota(I32, (tq, tq), 1),
                    1.0, 0.0).astype(BF16)
    mx_ref[...] = jnp.full(mx_ref.shape, -jnp.inf, F32)
    scale = A_LATENT ** -0.5

    def logit_body(j, carry):
        k = keys_ref[j]
        kidx = krow + j * kb
        selt = ((k > thr) | ((k == thr) & (kidx < jj))) & (kidx < qend)
        sel = lax.dot_general(eye, jnp.where(selt, 1.0, 0.0).astype(BF16), nt, preferred_element_type=F32) > 0.5
        s = jnp.dot(q_all, kvt_ref[0, j], preferred_element_type=F32) * scale
        v = (j * kb - qs + kb) // _BIAS_STEP
        s = s + bias_ref[jnp.where(v < 0, len(_NEAR_DELTAS), v)]
        s = jnp.concatenate([jnp.where(sel, s[h * tq:(h + 1) * tq], -jnp.inf) for h in range(A_HEADS)], axis=0)
        s_ref[j] = s
        mx = mx_ref[...]
        for c in range(kb // LANES):
            mx = jnp.maximum(mx, s[:, c * LANES:(c + 1) * LANES])
        mx_ref[...] = mx
        return carry

    lax.fori_loop(0, nkb, logit_body, 0)
    m = jnp.max(mx_ref[...], axis=1, keepdims=True)
    m = jnp.where(m == -jnp.inf, 0.0, m)
    mx_ref[...] = jnp.broadcast_to(m, mx_ref.shape)
    l_ref[...] = jnp.zeros(l_ref.shape, F32)
    acc_ref[...] = jnp.zeros(acc_ref.shape, F32)

    def pv_body(j, carry):
        mb = mx_ref[...]
        s = s_ref[j]
        ps = [jnp.exp(s[:, c * LANES:(c + 1) * LANES] - mb) for c in range(kb // LANES)]
        lsum = l_ref[...]
        for pc in ps:
            lsum = lsum + pc
        l_ref[...] = lsum
        p = jnp.concatenate(ps, axis=1).astype(BF16)
        acc_ref[...] += jnp.dot(p, kv_ref[0, j], preferred_element_type=F32)
        return carry

    lax.fori_loop(0, nkb, pv_body, 0)
    o = acc_ref[...] / jnp.sum(l_ref[...], axis=1, keepdims=True)
    for h in range(A_HEADS):
        o_ref[:, h * LANES:(h + 1) * LANES] = o[h * tq:(h + 1) * tq].astype(o_ref.dtype)


def _attn_call(q, iq, iwt, ik2, kvt, kvb, bias, bsz, seq):
    tq, kb = _TQ, _KB
    nq = seq // tq
    nblk = seq // kb
    n_sel = min(TOPK_KEYS_MAX, seq // 4)
    jbits = int(seq).bit_length()
    row = lambda b, i: (b * nq + i, 0)
    kern = functools.partial(_attn_kernel, n_sel=float(n_sel), jbits=jbits)
    return pl.pallas_call(
        kern,
        grid=(bsz, nq),
        in_specs=[pl.BlockSpec((tq, A_HEADS * A_LATENT), row),
                  pl.BlockSpec((tq, IDX_HEADS * IDX_DIM), row),
                  pl.BlockSpec((1, IDX_HEADS, tq), lambda b, i: (b, 0, i)),
                  pl.BlockSpec((1, 2, nblk, kb, LANES), lambda b, i: (b, 0, 0, 0, 0)),
                  pl.BlockSpec((1, nblk, A_LATENT, kb), lambda b, i: (b, 0, 0, 0)),
                  pl.BlockSpec((1, nblk, kb, A_LATENT), lambda b, i: (b, 0, 0, 0)),
                  _const_spec(bias.shape)],
        out_specs=pl.BlockSpec((tq, A_HEADS * A_LATENT), row),
        out_shape=jax.ShapeDtypeStruct((bsz * seq, A_HEADS * A_LATENT), BF16),
        scratch_shapes=[pltpu.VMEM((nblk, kb, tq), I32),
                        pltpu.VMEM((nblk, A_HEADS * tq, kb), F32),
                        pltpu.VMEM((A_HEADS * tq, LANES), F32),
                        pltpu.VMEM((A_HEADS * tq, LANES), F32),
                        pltpu.VMEM((A_HEADS * tq, A_LATENT), F32)],
        compiler_params=_cparams(("arbitrary", "arbitrary")),
        name="attention",
    )(q, iq, iwt, ik2, kvt, kvb, bias)


_SSD_L = 256
_PAIRS = SSM_HEADS // 2


def _ssd_kernel(z_ref, xbc_ref, dt_ref, dtt_ref, cw_ref, cb_ref, dtb_ref, dtbt_ref, al_ref, alt_ref,
                dsk_ref, nw_ref, o_ref, ext_ref, state_ref, y_ref):
    L = _SSD_L
    hd = SSM_HEADDIM

    @pl.when(pl.program_id(1) == 0)
    def _():
        ext_ref[0:8, :] = jnp.zeros((8, SSM_CONV_DIM), F32)
        state_ref[...] = jnp.zeros(state_ref.shape, F32)

    x = xbc_ref[...]
    ext_ref[8:8 + L, :] = x
    w = cw_ref[...]
    conv = x * w[3:4] + cb_ref[...]
    for k in range(1, SSM_CONV):
        conv = conv + ext_ref[8 - k:8 - k + L, :] * w[SSM_CONV - 1 - k:SSM_CONV - k]
    ext_ref[0:8, :] = x[L - 8:L]
    act = conv * jax.nn.sigmoid(conv)
    xs = act[:, :SSM_D_INNER]
    boff = SSM_D_INNER
    coff = SSM_D_INNER + SSM_GROUPS * SSM_STATE

    def softplus(v):
        return jnp.maximum(v, 0.0) + jnp.log1p(jnp.exp(-jnp.abs(v)))

    dt = softplus(dt_ref[...] + dtb_ref[...])
    dtt = softplus(dtt_ref[0] + dtbt_ref[...])
    a_col = dt * (-jnp.exp(al_ref[...]))
    a_row = dtt * (-jnp.exp(alt_ref[...]))
    ri = lax.broadcasted_iota(I32, (L, L), 0)
    ci = lax.broadcasted_iota(I32, (L, L), 1)
    causal = ci <= ri
    cs_col = jnp.dot(jnp.where(causal, 1.0, 0.0), a_col, precision=HI, preferred_element_type=F32)
    cs_row = jnp.dot(a_row, jnp.where(ri <= ci, 1.0, 0.0), precision=HI, preferred_element_type=F32)
    cs_last = cs_col[L - 1:L, :]
    lane = lax.broadcasted_iota(I32, (1, LANES), 1)
    lo = lane < hd
    sub = lax.broadcasted_iota(I32, (LANES, 1), 0)

    for g in range(SSM_GROUPS):
        bm = act[:, boff + g * SSM_STATE: boff + (g + 1) * SSM_STATE].astype(BF16)
        cm = act[:, coff + g * SSM_STATE: coff + (g + 1) * SSM_STATE].astype(BF16)
        cb = lax.dot_general(cm, bm, (((1,), (1,)), ((), ())), preferred_element_type=F32)
        for pp in range(_PAIRS // SSM_GROUPS):
            p = g * (_PAIRS // SSM_GROUPS) + pp
            h0, h1 = 2 * p, 2 * p + 1
            xp = xs[:, p * LANES:(p + 1) * LANES]
            dtl = jnp.where(lo, dt[:, h0:h0 + 1], dt[:, h1:h1 + 1])
            xdt = xp * dtl
            csl = jnp.where(lo, cs_col[:, h0:h0 + 1], cs_col[:, h1:h1 + 1])
            last = jnp.where(lo, cs_last[:, h0:h0 + 1], cs_last[:, h1:h1 + 1])
            ydiag = jnp.zeros((L, LANES), F32)
            for h, msk in ((h0, lo), (h1, jnp.logical_not(lo))):
                seg = cs_col[:, h:h + 1] - cs_row[h:h + 1, :]
                gm = (cb * jnp.exp(jnp.where(causal, seg, -jnp.inf))).astype(BF16)
                ydiag = ydiag + jnp.dot(gm, jnp.where(msk, xdt, 0.0).astype(BF16), preferred_element_type=F32)
            prev = state_ref[p]
            yoff = lax.dot_general(cm, prev.astype(BF16), (((1,), (1,)), ((), ())), preferred_element_type=F32)
            y_ref[:, p * LANES:(p + 1) * LANES] = ydiag + yoff * jnp.exp(csl) + xp * dsk_ref[:, p * LANES:(p + 1) * LANES]
            wx = (xdt * jnp.exp(last - csl)).astype(BF16)
            st = lax.dot_general(wx, bm, (((0,), (0,)), ((), ())), preferred_element_type=F32)
            cdec = jnp.where(sub < hd, jnp.exp(cs_last[:, h0:h0 + 1]), jnp.exp(cs_last[:, h1:h1 + 1]))
            state_ref[p] = prev * cdec + st

    z = z_ref[...]
    y = y_ref[...] * (z * jax.nn.sigmoid(z))
    gw = SSM_D_INNER // SSM_GROUPS
    for g in range(SSM_GROUPS):
        yg = y[:, g * gw:(g + 1) * gw]
        yg = yg * lax.rsqrt(jnp.mean(yg * yg, axis=-1, keepdims=True) + LN_EPS)
        o_ref[:, g * gw:(g + 1) * gw] = (yg * nw_ref[:, g * gw:(g + 1) * gw]).astype(o_ref.dtype)


def _ssd_call(z, xbc, dt, dtt, conv_w, conv_b, dt_bias, a_log, d_skip, norm_w, bsz, seq):
    L = _SSD_L
    nc = seq // L
    row = lambda b, c: (b * nc + c, 0)
    h = SSM_HEADS
    return pl.pallas_call(
        _ssd_kernel,
        grid=(bsz, nc),
        in_specs=[pl.BlockSpec((L, SSM_D_INNER), row),
                  pl.BlockSpec((L, SSM_CONV_DIM), row),
                  pl.BlockSpec((L, h), row),
                  pl.BlockSpec((1, h, L), lambda b, c: (b, 0, c)),
                  _const_spec((SSM_CONV, SSM_CONV_DIM)), _const_spec((1, SSM_CONV_DIM)),
                  _const_spec((1, h)), _const_spec((h, 1)), _const_spec((1, h)), _const_spec((h, 1)),
                  _const_spec((1, SSM_D_INNER)), _const_spec((1, SSM_D_INNER))],
        out_specs=pl.BlockSpec((L, SSM_D_INNER), row),
        out_shape=jax.ShapeDtypeStruct((bsz * seq, SSM_D_INNER), BF16),
        scratch_shapes=[pltpu.VMEM((L + 8, SSM_CONV_DIM), F32),
                        pltpu.VMEM((_PAIRS, LANES, SSM_STATE), F32),
                        pltpu.VMEM((L, SSM_D_INNER), F32)],
        compiler_params=_cparams(("arbitrary", "arbitrary")),
        name="ssd",
    )(z, xbc, dt, dtt, conv_w, conv_b.reshape(1, -1), dt_bias.reshape(1, h), dt_bias.reshape(h, 1),
      a_log.reshape(1, h), a_log.reshape(h, 1), jnp.repeat(d_skip, SSM_HEADDIM).reshape(1, -1),
      norm_w.reshape(1, -1))


_TM = 256


def _merge_kernel(oa_ref, ob_ref, ga_ref, gb_ref, x_ref, mod_ref, wpa_ref, wpb_ref, wo_ref, g1_ref, b1_ref,
                  wrh_ref, wrl_ref, br_ref, x1_ref, u2_ref, route_ref, gate_ref, cnt_ref, base_ref, *, alpha):
    tm = _TM
    ne = N_EXPERTS

    @pl.when(pl.program_id(0) == 0)
    def _():
        base_ref[...] = jnp.zeros(base_ref.shape, F32)

    ma = jnp.dot(oa_ref[...], wpa_ref[...], preferred_element_type=F32)
    mb = jnp.dot(ob_ref[...], wpb_ref[...], preferred_element_type=F32)
    merged = jax.nn.sigmoid(ga_ref[...]) * ma + jax.nn.sigmoid(gb_ref[...]) * mb
    t = jnp.dot(merged.astype(BF16), wo_ref[...], preferred_element_type=F32)
    x1 = _ln(alpha * x_ref[...] + mod_ref[0, 2:3, :] * t) * g1_ref[...] + b1_ref[...]
    x1_ref[...] = x1
    u2 = _ln(x1) * (1.0 + mod_ref[0, 4:5, :]) + mod_ref[0, 3:4, :]
    u2_ref[...] = u2
    nt = (((1,), (1,)), ((), ()))
    uh = u2.astype(BF16)
    ul = (u2 - uh.astype(F32)).astype(BF16)
    wh, wl = wrh_ref[...], wrl_ref[...]
    logits = (lax.dot_general(wh, uh, nt, preferred_element_type=F32)
              + lax.dot_general(wl, uh, nt, preferred_element_type=F32)
              + lax.dot_general(wh, ul, nt, preferred_element_type=F32)) + br_ref[...]
    eio = lax.broadcasted_iota(I32, (ne, tm), 0).astype(F32)
    vals, ids = [], []
    for _ in range(TOPK_EXPERTS):
        m = jnp.max(logits, axis=0, keepdims=True)
        idx = jnp.min(jnp.where(logits == m, eio, float(ne)), axis=0, keepdims=True)
        vals.append(m)
        ids.append(idx)
        logits = jnp.where(eio == idx, -jnp.inf, logits)
    es = [jnp.exp(v - vals[0]) for v in vals]
    den = es[0] + es[1] + es[2] + es[3]

    onehot = jnp.zeros((ne, tm), F32)
    for idx in ids:
        onehot = onehot + jnp.where(eio == idx, 1.0, 0.0)
    ri = lax.broadcasted_iota(I32, (tm, tm), 0)
    ci = lax.broadcasted_iota(I32, (tm, tm), 1)
    before = jnp.where(ri < ci, 1.0, 0.0).astype(BF16)
    base = base_ref[...]
    prefix = jnp.dot(onehot.astype(BF16), before, preferred_element_type=F32) + base
    sub = lax.broadcasted_iota(I32, (8, tm), 0)
    route = jnp.zeros((8, tm), F32)
    gates = jnp.zeros((8, tm), F32)
    for j in range(TOPK_EXPERTS):
        rank = jnp.sum(jnp.where(eio == ids[j], prefix, 0.0), axis=0, keepdims=True)
        route = jnp.where(sub == j, ids[j], route)
        route = jnp.where(sub == TOPK_EXPERTS + j, rank, route)
        gates = jnp.where(sub == j, es[j] / den, gates)
    route_ref[...] = route.astype(I32)
    gate_ref[...] = gates
    base = base + jnp.sum(onehot, axis=1, keepdims=True)
    base_ref[...] = base
    cnt_ref[...] = jnp.broadcast_to(base, cnt_ref.shape)


def _merge_call(o_a, o_b, g_a, g_b, x2, mod3, wpa, wpb, wo, ln_g, ln_b, w_router, b_router, seq, alpha):
    n_tok, d = x2.shape
    tm = _TM
    ne = N_EXPERTS
    row = lambda i: (i, 0)
    blk = pl.BlockSpec((tm, d), row)
    sm = pl.BlockSpec((8, tm), lambda i: (0, i))
    wrt = w_router.T
    wrh = wrt.astype(BF16)
    wrl = (wrt - wrh.astype(F32)).astype(BF16)
    return pl.pallas_call(
        functools.partial(_merge_kernel, alpha=alpha),
        grid=(n_tok // tm,),
        in_specs=[blk, blk, blk, blk, blk,
                  pl.BlockSpec((1, 6, d), lambda i: ((i * tm) // seq, 0, 0)),
                  _const_spec((d, d)), _const_spec((d, d)), _const_spec((d, d)),
                  _const_spec((1, d)), _const_spec((1, d)), _const_spec((ne, d)), _const_spec((ne, d)),
                  _const_spec((ne, 1))],
        out_specs=[blk, blk, sm, sm, pl.BlockSpec((ne, LANES), lambda i: (0, 0))],
        out_shape=[jax.ShapeDtypeStruct((n_tok, d), F32), jax.ShapeDtypeStruct((n_tok, d), F32),
                   jax.ShapeDtypeStruct((8, n_tok), I32), jax.ShapeDtypeStruct((8, n_tok), F32),
                   jax.ShapeDtypeStruct((ne, LANES), F32)],
        scratch_shapes=[pltpu.VMEM((ne, 1), F32)],
        compiler_params=_cparams(("arbitrary",)),
        name="merge",
    )(o_a, o_b, g_a, g_b, x2, mod3, wpa, wpb, wo, ln_g, ln_b, wrh, wrl, b_router.reshape(ne, 1))


_TD = 256
_TMB = 512


def _dispatch_kernel(pend_ref, dest_ref, u2_ref, xs_ref, zero_ref, sem):
    @pl.when(pl.program_id(0) == 0)
    def _():
        zero_ref[...] = jnp.zeros(zero_ref.shape, F32)
        for e in range(N_EXPERTS):
            end = pend_ref[e]
            start = pend_ref[e - 1] if e else 0

            @pl.when(end > start)
            def _():
                dst = xs_ref.at[pl.ds(pl.multiple_of(end - _TMB, _TMB), _TMB), :]
                cp = pltpu.make_async_copy(zero_ref, dst, sem)
                cp.start()
                cp.wait()

    def issue(t, carry):
        for j in range(TOPK_EXPERTS):
            d = dest_ref[t * TOPK_EXPERTS + j]
            pltpu.make_async_copy(u2_ref.at[pl.ds(t, 1), :], xs_ref.at[pl.ds(d, 1), :], sem).start()
        return carry

    lax.fori_loop(0, _TD, issue, 0)
    for _ in range(TOPK_EXPERTS):
        pltpu.make_async_copy(u2_ref, xs_ref.at[pl.ds(0, _TD), :], sem).wait()


def _dispatch_call(pends, dest_flat, u2, n_slots):
    n_tok, d = u2.shape
    grid_spec = pltpu.PrefetchScalarGridSpec(
        num_scalar_prefetch=1,
        grid=(n_tok // _TD,),
        in_specs=[pl.BlockSpec((_TD * TOPK_EXPERTS,), lambda i, pe: (i,), memory_space=pltpu.SMEM),
                  pl.BlockSpec((_TD, d), lambda i, pe: (i, 0))],
        out_specs=pl.BlockSpec(memory_space=pl.ANY),
        scratch_shapes=[pltpu.VMEM((_TMB, d), F32), pltpu.SemaphoreType.DMA(())],
    )
    return pl.pallas_call(
        _dispatch_kernel,
        grid_spec=grid_spec,
        out_shape=jax.ShapeDtypeStruct((n_slots, d), F32),
        compiler_params=_cparams(("arbitrary",)),
        name="dispatch",
    )(pends, dest_flat, u2)


def _expert_kernel(be_ref, nu_ref, xs_ref, w1_ref, b1_ref, w2_ref, b2_ref, y_ref, w1b_ref, w2b_ref):
    i = pl.program_id(0)
    f = w2_ref.shape[1]
    used = i < nu_ref[0]
    new_expert = jnp.logical_or(i == 0, be_ref[i] != be_ref[jnp.maximum(i - 1, 0)])

    @pl.when(jnp.logical_and(used, new_expert))
    def _():
        w1b_ref[...] = w1_ref[0].astype(BF16)
        w2b_ref[...] = w2_ref[0].astype(BF16)

    @pl.when(used)
    def _():
        h = jnp.dot(xs_ref[...].astype(BF16), w1b_ref[...], preferred_element_type=F32) + b1_ref[0]
        gate = jnp.minimum(h[:, :f], SWIGLU_LIMIT)
        up = jnp.clip(h[:, f:], -SWIGLU_LIMIT, SWIGLU_LIMIT)
        act = (up + 1.0) * gate * jax.nn.sigmoid(SWIGLU_ALPHA * gate)
        y_ref[...] = jnp.dot(act.astype(BF16), w2b_ref[...], preferred_element_type=F32) + b2_ref[0]

    @pl.when(i >= nu_ref[0])
    def _():
        y_ref[...] = jnp.zeros(y_ref.shape, F32)


def _expert_call(block_expert, n_used, xs, w1, b1, w2, b2):
    n_slots, d = xs.shape
    ne, _, f2 = w1.shape
    f = f2 // 2
    grid_spec = pltpu.PrefetchScalarGridSpec(
        num_scalar_prefetch=2,
        grid=(n_slots // _TMB,),
        in_specs=[pl.BlockSpec((_TMB, d), lambda i, be, nu: (i, 0)),
                  pl.BlockSpec((1, d, f2), lambda i, be, nu: (be[i], 0, 0)),
                  pl.BlockSpec((1, 1, f2), lambda i, be, nu: (be[i], 0, 0)),
                  pl.BlockSpec((1, f, d), lambda i, be, nu: (be[i], 0, 0)),
                  pl.BlockSpec((1, 1, d), lambda i, be, nu: (be[i], 0, 0))],
        out_specs=pl.BlockSpec((_TMB, d), lambda i, be, nu: (i, 0)),
        scratch_shapes=[pltpu.VMEM((d, f2), BF16), pltpu.VMEM((f, d), BF16)],
    )
    return pl.pallas_call(
        _expert_kernel,
        grid_spec=grid_spec,
        out_shape=jax.ShapeDtypeStruct((n_slots, d), F32),
        compiler_params=_cparams(("arbitrary",)),
        name="experts",
    )(block_expert, n_used, xs, w1, b1.reshape(ne, 1, f2), w2, b2.reshape(ne, 1, d))


_TC = 256


def _combine_kernel(dest_ref, gate_ref, x1_ref, mod_ref, g2_ref, b2_ref, y_hbm, o_ref, buf_ref, sem, *, alpha):
    def issue(t, carry):
        for j in range(TOPK_EXPERTS):
            d = dest_ref[t * TOPK_EXPERTS + j]
            pltpu.make_async_copy(y_hbm.at[pl.ds(d, 1), :], buf_ref.at[j, pl.ds(t, 1), :], sem).start()
        return carry

    lax.fori_loop(0, _TC, issue, 0, unroll=2)
    for j in range(TOPK_EXPERTS):
        pltpu.make_async_copy(y_hbm.at[pl.ds(0, _TC), :], buf_ref.at[j], sem).wait()
    gates = gate_ref[...]
    y = gates[:, 0:1] * buf_ref[0]
    for j in range(1, TOPK_EXPERTS):
        y = y + gates[:, j:j + 1] * buf_ref[j]
    o_ref[...] = _ln(alpha * x1_ref[...] + mod_ref[0, 5:6, :] * y) * g2_ref[...] + b2_ref[...]


def _combine_call(dest_flat, gates, x1, mod3, ln_g, ln_b, y, seq, alpha):
    n_tok, d = x1.shape
    tc = _TC
    row = lambda i: (i, 0)
    return pl.pallas_call(
        functools.partial(_combine_kernel, alpha=alpha),
        grid=(n_tok // tc,),
        in_specs=[pl.BlockSpec((tc * TOPK_EXPERTS,), lambda i: (i,), memory_space=pltpu.SMEM),
                  pl.BlockSpec((tc, TOPK_EXPERTS), row),
                  pl.BlockSpec((tc, d), row),
                  pl.BlockSpec((1, 6, d), lambda i: ((i * tc) // seq, 0, 0)),
                  _const_spec((1, d)), _const_spec((1, d)),
                  pl.BlockSpec(memory_space=pl.ANY)],
        out_specs=pl.BlockSpec((tc, d), row),
        out_shape=jax.ShapeDtypeStruct((n_tok, d), F32),
        scratch_shapes=[pltpu.VMEM((TOPK_EXPERTS, tc, d), F32), pltpu.SemaphoreType.DMA(())],
        compiler_params=_cparams(("arbitrary",)),
        name="combine",
    )(dest_flat, gates, x1, mod3, ln_g, ln_b, y)


def _permute_w_in(w):
    d = w.shape[0]
    s = np.cumsum([0, A_HEADS * A_LATENT, A_LATENT, IDX_HEADS * IDX_DIM, IDX_DIM, IDX_HEADS,
                   SSM_D_INNER, SSM_CONV_DIM, SSM_HEADS, d, d]).tolist()
    q, kv, iq, ik, iw, z, xbc, dt, ga, gb = [w[:, s[i]:s[i + 1]] for i in range(10)]
    pad = jnp.zeros((d, LANES - IDX_DIM - IDX_HEADS - SSM_HEADS), w.dtype)
    return jnp.concatenate([q, kv, iq, ik, iw, dt, pad, z, xbc, ga, gb], axis=1).astype(BF16)


def _pad_lanes(v, fill=0.0):
    return jnp.pad(v.reshape(1, -1), ((0, 0), (0, LANES - v.shape[-1])), constant_values=fill)


def kernel(x, c, w_mod, b_mod, w_in, kv_norm_w, idx_k_norm_w, idx_k_norm_b, rel_bias, conv_w, conv_b, dt_bias,
           a_log, d_skip, ssm_norm_w, w_proj_a, w_proj_b, w_out, ln1_g, ln1_b, w_router, b_router, w1, b1, w2, b2,
           ln2_g, ln2_b):
    bsz, seq, d = x.shape
    depth = w_mod.shape[0]
    alpha = (2.0 * depth) ** 0.25
    n_tok = bsz * seq
    n_asg = n_tok * TOPK_EXPERTS
    n_blocks = n_asg // _TMB + N_EXPERTS
    n_slots = n_blocks * _TMB
    nblk = seq // _KB
    bias = _bias_tiles(rel_bias)
    x2 = x.reshape(n_tok, d)
    for l in range(depth):
        mod3 = _mod_call(c, w_mod[l], b_mod[l]).reshape(bsz, 6, d)
        q, kvn, iq, small, z, xbc, g_a, g_b = _inproj_call(
            x2, mod3, _permute_w_in(w_in[l]), kv_norm_w[l].reshape(1, -1),
            _pad_lanes(idx_k_norm_w[l]), _pad_lanes(idx_k_norm_b[l]), seq)
        ik = small[:, :IDX_DIM].astype(BF16).reshape(bsz, nblk, _KB, IDX_DIM)
        zpad = jnp.zeros_like(ik)
        ik2 = jnp.stack([jnp.concatenate([ik, zpad], axis=3), jnp.concatenate([zpad, ik], axis=3)], axis=1)
        kvb = kvn.reshape(bsz, nblk, _KB, A_LATENT)
        kvt = jnp.swapaxes(kvb, 2, 3)
        iwt = jnp.swapaxes(small[:, _SM_IW:_SM_IW + IDX_HEADS].reshape(bsz, seq, IDX_HEADS), 1, 2)
        o_a = _attn_call(q, iq, iwt, ik2, kvt, kvb, bias, bsz, seq)
        dt = small[:, _SM_DT:_SM_DT + SSM_HEADS]
        dtt = jnp.swapaxes(dt.reshape(bsz, seq, SSM_HEADS), 1, 2)
        o_b = _ssd_call(z, xbc, dt, dtt, conv_w[l], conv_b[l], dt_bias[l], a_log[l], d_skip[l], ssm_norm_w[l],
                        bsz, seq)
        x1, u2, route, gates, cnt = _merge_call(
            o_a, o_b, g_a, g_b, x2, mod3, w_proj_a[l].astype(BF16), w_proj_b[l].astype(BF16),
            w_out[l].astype(BF16), ln1_g[l].reshape(1, -1), ln1_b[l].reshape(1, -1), w_router[l], b_router[l],
            seq, alpha)
        counts = cnt[:, 0].astype(I32)
        padded = (counts + _TMB - 1) // _TMB * _TMB
        pends = jnp.cumsum(padded).astype(I32)
        pstarts = pends - padded
        eid = route[:TOPK_EXPERTS]
        onehot = eid[:, :, None] == jnp.arange(N_EXPERTS, dtype=I32)
        dest = jnp.sum(jnp.where(onehot, pstarts, 0), axis=-1) + route[TOPK_EXPERTS:2 * TOPK_EXPERTS]
        dest_flat = dest.T.reshape(n_asg)
        block_start = jnp.arange(n_blocks, dtype=I32) * _TMB
        block_expert = jnp.minimum(jnp.sum(block_start[:, None] >= pends[None, :], axis=1), N_EXPERTS - 1).astype(I32)
        n_used = (pends[-1:] // _TMB).astype(I32)
        xs = _dispatch_call(pends, dest_flat, u2, n_slots)
        y = _expert_call(block_expert, n_used, xs, w1[l], b1[l], w2[l], b2[l])
        x2 = _combine_call(dest_flat, gates[:TOPK_EXPERTS].T, x1, mod3, ln2_g[l].reshape(1, -1),
                           ln2_b[l].reshape(1, -1), y, seq, alpha)
    return x2.reshape(bsz, seq, d)
```

```python
import functools
import math

import jax
import jax.numpy as jnp
import numpy as np
from jax import lax
from jax.experimental import pallas as pl
from jax.experimental.pallas import tpu as pltpu

F32 = jnp.float32
BF16 = jnp.bfloat16
I32 = jnp.int32

CHUNK = 64
A_HEADS = 8
A_LATENT = 128
IDX_HEADS = 8
IDX_DIM = 64
TOPK_KEYS_MAX = 256
REL_BUCKETS = 32
REL_MAX_DIST = 128
SSM_D_INNER = 1024
SSM_HEADDIM = 64
SSM_HEADS = SSM_D_INNER // SSM_HEADDIM
SSM_GROUPS = 4
SSM_STATE = 128
SSM_CONV = 4
SSM_CONV_DIM = SSM_D_INNER + 2 * SSM_GROUPS * SSM_STATE
N_EXPERTS = 32
TOPK_EXPERTS = 4
SWIGLU_LIMIT = 7.0
SWIGLU_ALPHA = 1.702
LN_EPS = 1e-5

LANES = 128
INT_MIN = -2147483648
VMEM_LIMIT = 56 * 1024 * 1024

HI = lax.Precision.HIGHEST


def _cparams(sem):
    return pltpu.CompilerParams(dimension_semantics=sem, vmem_limit_bytes=VMEM_LIMIT)


def _ln(x):
    mu = jnp.mean(x, axis=-1, keepdims=True)
    xc = x - mu
    var = jnp.mean(xc * xc, axis=-1, keepdims=True)
    return xc * lax.rsqrt(var + LN_EPS)


def _const_spec(shape):
    nd = len(shape)
    return pl.BlockSpec(shape, lambda *_: (0,) * nd, pipeline_mode=pl.Buffered(1))


def _mod_kernel(c_ref, w_ref, b_ref, o_ref):
    c = c_ref[...]
    sc = c * jax.nn.sigmoid(c)
    o_ref[...] = jnp.dot(sc, w_ref[...], precision=HI, preferred_element_type=F32) + b_ref[...]


def _mod_call(c, w_mod, b_mod):
    bsz, d = c.shape
    n = w_mod.shape[1]
    tn = 1024
    return pl.pallas_call(
        _mod_kernel,
        grid=(n // tn,),
        in_specs=[pl.BlockSpec((bsz, d), lambda j: (0, 0)),
                  pl.BlockSpec((d, tn), lambda j: (0, j)),
                  pl.BlockSpec((1, tn), lambda j: (0, j))],
        out_specs=pl.BlockSpec((bsz, tn), lambda j: (0, j)),
        out_shape=jax.ShapeDtypeStruct((bsz, n), F32),
        compiler_params=_cparams(("arbitrary",)),
        name="mod",
    )(c, w_mod, b_mod.reshape(1, n))


_C_Q, _C_KV, _C_IQ, _C_SM, _C_Z, _C_XBC, _C_GA, _C_GB, _C_END = 0, 1024, 1152, 1664, 1792, 2816, 4864, 5888, 6912
_SM_IW = IDX_DIM
_SM_DT = IDX_DIM + SSM_HEADS


def _inproj_kernel(x_ref, mod_ref, w_ref, kvw_ref, ikw_ref, ikb_ref,
                   q_ref, kv_ref, iq_ref, ik2_ref, sm_ref, smt_ref, z_ref, xbc_ref, ga_ref, gb_ref):
    u = _ln(x_ref[...]) * (1.0 + mod_ref[0, 1:2, :]) + mod_ref[0, 0:1, :]
    ub = u.astype(BF16)

    def mm(a, b):
        return jnp.dot(ub, w_ref[:, a:b], preferred_element_type=F32)

    q_ref[...] = mm(_C_Q, _C_KV).astype(BF16)
    kv = mm(_C_KV, _C_IQ)
    kv = kv * lax.rsqrt(jnp.mean(kv * kv, axis=-1, keepdims=True) + LN_EPS)
    kv_ref[...] = (kv * kvw_ref[...]).astype(BF16)
    iq_ref[...] = mm(_C_IQ, _C_SM).astype(BF16)
    g = mm(_C_SM, _C_Z)
    lane = lax.broadcasted_iota(I32, g.shape, 1)
    is_ik = lane < IDX_DIM
    mu = jnp.sum(jnp.where(is_ik, g, 0.0), axis=-1, keepdims=True) * (1.0 / IDX_DIM)
    gc = g - mu
    var = jnp.sum(jnp.where(is_ik, gc * gc, 0.0), axis=-1, keepdims=True) * (1.0 / IDX_DIM)
    ik = jnp.where(is_ik, gc * lax.rsqrt(var + LN_EPS) * ikw_ref[...] + ikb_ref[...], 0.0)
    ik2_ref[0] = ik.astype(BF16)
    ik2_ref[1] = pltpu.roll(ik, IDX_DIM, axis=1).astype(BF16)
    sm = jnp.where(lane < _SM_IW + IDX_HEADS, g * (IDX_HEADS ** -0.5), g)
    sm_ref[...] = sm
    smt_ref[0] = sm.T
    z_ref[...] = mm(_C_Z, _C_XBC)
    xbc_ref[...] = mm(_C_XBC, _C_GA)
    ga_ref[...] = mm(_C_GA, _C_GB)
    gb_ref[...] = mm(_C_GB, _C_END)


def _inproj_call(x2, mod3, w_perm, kvw, ikw, ikb, seq):
    n_tok, d = x2.shape
    tm = 256
    row = lambda i: (i, 0)

    def ospec(n):
        return pl.BlockSpec((tm, n), row)

    spt = seq // tm
    sd = jax.ShapeDtypeStruct
    return pl.pallas_call(
        _inproj_kernel,
        grid=(n_tok // tm,),
        in_specs=[pl.BlockSpec((tm, d), row),
                  pl.BlockSpec((1, 6, d), lambda i: (i // spt, 0, 0)),
                  _const_spec(w_perm.shape), _const_spec((1, 128)), _const_spec((1, 128)), _const_spec((1, 128))],
        out_specs=[ospec(1024), ospec(128), ospec(512),
                   pl.BlockSpec((2, tm, LANES), lambda i: (0, i, 0)),
                   ospec(LANES),
                   pl.BlockSpec((1, LANES, tm), lambda i: (i // spt, 0, i % spt)),
                   ospec(1024), ospec(2048), ospec(1024), ospec(1024)],
        out_shape=[sd((n_tok, 1024), BF16), sd((n_tok, 128), BF16), sd((n_tok, 512), BF16),
                   sd((2, n_tok, LANES), BF16), sd((n_tok, LANES), F32), sd((n_tok // seq, LANES, seq), F32),
                   sd((n_tok, 1024), F32), sd((n_tok, 2048), F32), sd((n_tok, 1024), F32), sd((n_tok, 1024), F32)],
        compiler_params=_cparams(("arbitrary",)),
        name="inproj",
    )(x2, mod3, w_perm, kvw, ikw, ikb)


_TQ = 256
_KB = 256
_BIAS_STEP = math.gcd(_TQ, _KB)
_NEAR_DELTAS = tuple(range(-_KB, 1, _BIAS_STEP))
assert REL_MAX_DIST <= _BIAS_STEP


def _t5_bucket(rel):
    half = REL_BUCKETS // 2
    max_exact = half // 2
    ret = (rel > 0).astype(jnp.int32) * half
    n = jnp.abs(rel)
    nf = jnp.maximum(n, 1).astype(jnp.float32)
    large = max_exact + (jnp.log(nf / max_exact) / math.log(REL_MAX_DIST / max_exact)
                         * (half - max_exact)).astype(jnp.int32)
    large = jnp.minimum(large, half - 1)
    return ret + jnp.where(n < max_exact, n, large)


def _bias_tiles(rel_bias):
    i = jnp.arange(_TQ, dtype=jnp.int32)[:, None]
    c = jnp.arange(_KB, dtype=jnp.int32)[None, :]
    rel = [c + delta - i for delta in _NEAR_DELTAS] + [jnp.full((_TQ, _KB), -REL_MAX_DIST, jnp.int32)]
    bucket = _t5_bucket(jnp.stack(rel))
    b = jnp.zeros((len(rel), A_HEADS, _TQ, _KB), F32)
    for k in range(REL_BUCKETS):
        b = jnp.where((bucket == k)[:, None], rel_bias[k].astype(F32)[None, :, None, None], b)
    return (b * (A_LATENT ** 0.5)).reshape(len(rel), A_HEADS * _TQ, _KB)


def _attn_kernel(q_ref, iq_ref, iwt_ref, ik2_ref, kvt_ref, kv_ref, bias_ref, o_ref,
                 keys_ref, s_ref, mx_ref, l_ref, acc_ref, *, n_sel, jbits):
    tq, kb = _TQ, _KB
    qs = pl.program_id(1) * tq
    nkb = (qs + tq + kb - 1) // kb
    nt = (((1,), (1,)), ((), ()))

    qpos = lax.broadcasted_iota(I32, (1, tq), 1) + qs
    qend = (qpos // CHUNK + 1) * CHUNK
    krow = lax.broadcasted_iota(I32, (kb, tq), 0)
    iq = iq_ref[...]
    iwt = iwt_ref[0] * (IDX_DIM ** -0.5)

    def score_body(j, carry):
        acc = jnp.zeros((kb, tq), F32)
        for p in range(IDX_HEADS // 2):
            pair = iq[:, p * LANES:(p + 1) * LANES]
            for par in range(2):
                h = 2 * p + par
                s = lax.dot_general(ik2_ref[par, 0, j], pair, nt, preferred_element_type=F32)
                acc = acc + jnp.maximum(s, 0.0) * iwt[h:h + 1, :]
        bits = pltpu.bitcast(acc, I32)
        key = jnp.where(bits < 0, bits ^ 0x7FFFFFFF, bits)
        key = jnp.where(bits == INT_MIN, 0, key)
        keys_ref[j] = jnp.where(krow + j * kb < qend, key, INT_MIN)
        return carry

    lax.fori_loop(0, nkb, score_body, 0)

    def count(pred):
        def body(j, acc):
            m = jnp.where(pred(keys_ref[j], krow + j * kb), 1.0, 0.0)
            parts = [m[8 * i:8 * i + 8] for i in range(kb // 8)]
            while len(parts) > 1:
                parts = [parts[i] + parts[i + 1] for i in range(0, len(parts), 2)]
            return acc + parts[0]
        acc = lax.fori_loop(0, nkb, body, jnp.zeros((8, tq), F32))
        return jnp.sum(acc, axis=0, keepdims=True)

    def bit_body(it, tu):
        cu = tu | (jnp.int32(1) << (31 - it))
        cs = cu ^ INT_MIN
        cnt = count(lambda k, kidx: k >= cs)
        return jnp.where(cnt >= n_sel, cu, tu)

    thr = lax.fori_loop(0, 32, bit_body, jnp.zeros((1, tq), I32)) ^ INT_MIN
    n_ge = count(lambda k, kidx: k >= thr)

    def tie_search():
        need = n_sel - count(lambda k, kidx: k > thr)

        def j_body(it, jj):
            cj = jj | (jnp.int32(1) << (jbits - 1 - it))
            f = count(lambda k, kidx: (k == thr) & (kidx < cj))
            return jnp.where(f <= need, cj, jj)

        return lax.fori_loop(0, jbits, j_body, jnp.zeros((1, tq), I32))

    jj = lax.cond(jnp.max(n_ge) > n_sel, tie_search, lambda: jnp.full((1, tq), (1 << jbits) - 1, I32))

    q = q_ref[...]
    q_all = jnp.concatenate([q[:, h * LANES:(h + 1) * LANES] for h in range(A_HEADS)], axis=0)
    eye = jnp.where(lax.broadcasted_iota(I32, (tq, tq), 0) == lax.broadcasted_iota(I32, (tq, tq), 1),
                    1.0, 0.0).astype(BF16)
    mx_ref[...] = jnp.full(mx_ref.shape, -jnp.inf, F32)
    exp2_scale = A_LATENT ** -0.5 * math.log2(math.e)

    def logit_body(j, carry):
        k = keys_ref[j]
        kidx = krow + j * kb
        selt = ((k > thr) | ((k == thr) & (kidx < jj))) & (kidx < qend)
        sel = lax.dot_general(eye, jnp.where(selt, 1.0, 0.0).astype(BF16), nt, preferred_element_type=F32) > 0.5
        s = jnp.dot(q_all, kvt_ref[0, j], preferred_element_type=F32)
        v = (j * kb - qs + kb) // _BIAS_STEP
        s = s + bias_ref[jnp.where(v < 0, len(_NEAR_DELTAS), v)]
        s = jnp.concatenate([jnp.where(sel, s[h * tq:(h + 1) * tq], -jnp.inf) for h in range(A_HEADS)], axis=0)
        s_ref[j] = s
        mx = mx_ref[...]
        for c in range(kb // LANES):
            mx = jnp.maximum(mx, s[:, c * LANES:(c + 1) * LANES])
        mx_ref[...] = mx
        return carry

    lax.fori_loop(0, nkb, logit_body, 0)
    m = jnp.max(mx_ref[...], axis=1, keepdims=True)
    m = jnp.where(m == -jnp.inf, 0.0, m)
    mx_ref[...] = jnp.broadcast_to(m, mx_ref.shape)
    l_ref[...] = jnp.zeros(l_ref.shape, F32)
    acc_ref[...] = jnp.zeros(acc_ref.shape, F32)

    def pv_body(j, carry):
        mb = mx_ref[...]
        s = s_ref[j]
        ps = [jnp.exp2((s[:, c * LANES:(c + 1) * LANES] - mb) * exp2_scale) for c in range(kb // LANES)]
        lsum = l_ref[...]
        for pc in ps:
            lsum = lsum + pc
        l_ref[...] = lsum
        p = jnp.concatenate(ps, axis=1).astype(BF16)
        acc_ref[...] += jnp.dot(p, kv_ref[0, j], preferred_element_type=F32)
        return carry

    lax.fori_loop(0, nkb, pv_body, 0)
    o = acc_ref[...] / jnp.sum(l_ref[...], axis=1, keepdims=True)
    for h in range(A_HEADS):
        o_ref[:, h * LANES:(h + 1) * LANES] = o[h * tq:(h + 1) * tq].astype(o_ref.dtype)


def _attn_call(q, iq, iwt, ik2, kvt, kvb, bias, bsz, seq):
    tq, kb = _TQ, _KB
    nq = seq // tq
    nblk = seq // kb
    n_sel = min(TOPK_KEYS_MAX, seq // 4)
    jbits = int(seq).bit_length()
    row = lambda b, i: (b * nq + i, 0)
    kern = functools.partial(_attn_kernel, n_sel=float(n_sel), jbits=jbits)
    return pl.pallas_call(
        kern,
        grid=(bsz, nq),
        in_specs=[pl.BlockSpec((tq, A_HEADS * A_LATENT), row),
                  pl.BlockSpec((tq, IDX_HEADS * IDX_DIM), row),
                  pl.BlockSpec((1, IDX_HEADS, tq), lambda b, i: (b, _SM_IW // IDX_HEADS, i)),
                  pl.BlockSpec((2, 1, nblk, kb, LANES), lambda b, i: (0, b, 0, 0, 0)),
                  pl.BlockSpec((1, nblk, A_LATENT, kb), lambda b, i: (b, 0, 0, 0)),
                  pl.BlockSpec((1, nblk, kb, A_LATENT), lambda b, i: (b, 0, 0, 0)),
                  _const_spec(bias.shape)],
        out_specs=pl.BlockSpec((tq, A_HEADS * A_LATENT), row),
        out_shape=jax.ShapeDtypeStruct((bsz * seq, A_HEADS * A_LATENT), BF16),
        scratch_shapes=[pltpu.VMEM((nblk, kb, tq), I32),
                        pltpu.VMEM((nblk, A_HEADS * tq, kb), F32),
                        pltpu.VMEM((A_HEADS * tq, LANES), F32),
                        pltpu.VMEM((A_HEADS * tq, LANES), F32),
                        pltpu.VMEM((A_HEADS * tq, A_LATENT), F32)],
        compiler_params=_cparams(("arbitrary", "arbitrary")),
        name="attention",
    )(q, iq, iwt, ik2, kvt, kvb, bias)


_SSD_L = 256
_PAIRS = SSM_HEADS // 2


def _ssd_kernel(z_ref, xbc_ref, sm_ref, dtt_ref, cw_ref, cb_ref, dtb_ref, dtbt_ref, al_ref, alt_ref,
                dsk_ref, nw_ref, o_ref, ext_ref, state_ref, y_ref):
    L = _SSD_L
    hd = SSM_HEADDIM

    @pl.when(pl.program_id(1) == 0)
    def _():
        ext_ref[0:8, :] = jnp.zeros((8, SSM_CONV_DIM), F32)
        state_ref[...] = jnp.zeros(state_ref.shape, F32)

    x = xbc_ref[...]
    ext_ref[8:8 + L, :] = x
    w = cw_ref[...]
    conv = x * w[3:4] + cb_ref[...]
    for k in range(1, SSM_CONV):
        conv = conv + ext_ref[8 - k:8 - k + L, :] * w[SSM_CONV - 1 - k:SSM_CONV - k]
    ext_ref[0:8, :] = x[L - 8:L]
    act = conv * jax.nn.sigmoid(conv)
    xs = act[:, :SSM_D_INNER]
    boff = SSM_D_INNER
    coff = SSM_D_INNER + SSM_GROUPS * SSM_STATE

    def softplus(v):
        return jnp.maximum(v, 0.0) + jnp.log1p(jnp.exp(-jnp.abs(v)))

    dt = softplus(sm_ref[:, _SM_DT:_SM_DT + SSM_HEADS] + dtb_ref[...])
    dtt = softplus(dtt_ref[0] + dtbt_ref[...])
    a_col = dt * (-jnp.exp(al_ref[...]))
    a_row = dtt * (-jnp.exp(alt_ref[...]))
    ri = lax.broadcasted_iota(I32, (L, L), 0)
    ci = lax.broadcasted_iota(I32, (L, L), 1)
    causal = ci <= ri
    cs_col = jnp.dot(jnp.where(causal, 1.0, 0.0), a_col, precision=HI, preferred_element_type=F32)
    cs_row = jnp.dot(a_row, jnp.where(ri <= ci, 1.0, 0.0), precision=HI, preferred_element_type=F32)
    cs_last = cs_col[L - 1:L, :]
    lane = lax.broadcasted_iota(I32, (1, LANES), 1)
    lo = lane < hd
    sub = lax.broadcasted_iota(I32, (LANES, 1), 0)

    for g in range(SSM_GROUPS):
        bm = act[:, boff + g * SSM_STATE: boff + (g + 1) * SSM_STATE].astype(BF16)
        cm = act[:, coff + g * SSM_STATE: coff + (g + 1) * SSM_STATE].astype(BF16)
        cb = lax.dot_general(cm, bm, (((1,), (1,)), ((), ())), preferred_element_type=F32)
        for pp in range(_PAIRS // SSM_GROUPS):
            p = g * (_PAIRS // SSM_GROUPS) + pp
            h0, h1 = 2 * p, 2 * p + 1
            xp = xs[:, p * LANES:(p + 1) * LANES]
            dtl = jnp.where(lo, dt[:, h0:h0 + 1], dt[:, h1:h1 + 1])
            xdt = xp * dtl
            csl = jnp.where(lo, cs_col[:, h0:h0 + 1], cs_col[:, h1:h1 + 1])
            last = jnp.where(lo, cs_last[:, h0:h0 + 1], cs_last[:, h1:h1 + 1])
            ydiag = jnp.zeros((L, LANES), F32)
            for h, msk in ((h0, lo), (h1, jnp.logical_not(lo))):
                seg = cs_col[:, h:h + 1] - cs_row[h:h + 1, :]
                gm = (cb * jnp.exp(jnp.where(causal, seg, -jnp.inf))).astype(BF16)
                ydiag = ydiag + jnp.dot(gm, jnp.where(msk, xdt, 0.0).astype(BF16), preferred_element_type=F32)
            prev = state_ref[p]
            yoff = lax.dot_general(cm, prev.astype(BF16), (((1,), (1,)), ((), ())), preferred_element_type=F32)
            y_ref[:, p * LANES:(p + 1) * LANES] = ydiag + yoff * jnp.exp(csl) + xp * dsk_ref[:, p * LANES:(p + 1) * LANES]
            wx = (xdt * jnp.exp(last - csl)).astype(BF16)
            st = lax.dot_general(wx, bm, (((0,), (0,)), ((), ())), preferred_element_type=F32)
            cdec = jnp.where(sub < hd, jnp.exp(cs_last[:, h0:h0 + 1]), jnp.exp(cs_last[:, h1:h1 + 1]))
            state_ref[p] = prev * cdec + st

    z = z_ref[...]
    y = y_ref[...] * (z * jax.nn.sigmoid(z))
    gw = SSM_D_INNER // SSM_GROUPS
    for g in range(SSM_GROUPS):
        yg = y[:, g * gw:(g + 1) * gw]
        yg = yg * lax.rsqrt(jnp.mean(yg * yg, axis=-1, keepdims=True) + LN_EPS)
        o_ref[:, g * gw:(g + 1) * gw] = (yg * nw_ref[:, g * gw:(g + 1) * gw]).astype(o_ref.dtype)


def _ssd_call(z, xbc, small, smt, conv_w, conv_b, dt_bias, a_log, d_skip, norm_w, bsz, seq):
    L = _SSD_L
    nc = seq // L
    row = lambda b, c: (b * nc + c, 0)
    h = SSM_HEADS
    return pl.pallas_call(
        _ssd_kernel,
        grid=(bsz, nc),
        in_specs=[pl.BlockSpec((L, SSM_D_INNER), row),
                  pl.BlockSpec((L, SSM_CONV_DIM), row),
                  pl.BlockSpec((L, LANES), row),
                  pl.BlockSpec((1, h, L), lambda b, c: (b, _SM_DT // h, c)),
                  _const_spec((SSM_CONV, SSM_CONV_DIM)), _const_spec((1, SSM_CONV_DIM)),
                  _const_spec((1, h)), _const_spec((h, 1)), _const_spec((1, h)), _const_spec((h, 1)),
                  _const_spec((1, SSM_D_INNER)), _const_spec((1, SSM_D_INNER))],
        out_specs=pl.BlockSpec((L, SSM_D_INNER), row),
        out_shape=jax.ShapeDtypeStruct((bsz * seq, SSM_D_INNER), BF16),
        scratch_shapes=[pltpu.VMEM((L + 8, SSM_CONV_DIM), F32),
                        pltpu.VMEM((_PAIRS, LANES, SSM_STATE), F32),
                        pltpu.VMEM((L, SSM_D_INNER), F32)],
        compiler_params=_cparams(("arbitrary", "arbitrary")),
        name="ssd",
    )(z, xbc, small, smt, conv_w, conv_b.reshape(1, -1), dt_bias.reshape(1, h), dt_bias.reshape(h, 1),
      a_log.reshape(1, h), a_log.reshape(h, 1), jnp.repeat(d_skip, SSM_HEADDIM).reshape(1, -1),
      norm_w.reshape(1, -1))


_TM = 512


def _merge_kernel(oa_ref, ob_ref, ga_ref, gb_ref, x_ref, mod_ref, wpa_ref, wpb_ref, wo_ref, g1_ref, b1_ref,
                  wrh_ref, wrl_ref, br_ref, x1_ref, u2_ref, route_ref, gate_ref, cnt_ref, base_ref, *, alpha):
    tm = _TM
    ne = N_EXPERTS

    @pl.when(pl.program_id(0) == 0)
    def _():
        base_ref[...] = jnp.zeros(base_ref.shape, F32)

    ma = jnp.dot(oa_ref[...], wpa_ref[...], preferred_element_type=F32)
    mb = jnp.dot(ob_ref[...], wpb_ref[...], preferred_element_type=F32)
    merged = jax.nn.sigmoid(ga_ref[...]) * ma + jax.nn.sigmoid(gb_ref[...]) * mb
    t = jnp.dot(merged.astype(BF16), wo_ref[...], preferred_element_type=F32)
    x1 = _ln(alpha * x_ref[...] + mod_ref[0, 2:3, :] * t) * g1_ref[...] + b1_ref[...]
    x1_ref[...] = x1
    u2 = _ln(x1) * (1.0 + mod_ref[0, 4:5, :]) + mod_ref[0, 3:4, :]
    u2_ref[...] = u2
    nt = (((1,), (1,)), ((), ()))
    uh = u2.astype(BF16)
    ul = (u2 - uh.astype(F32)).astype(BF16)
    wh, wl = wrh_ref[...], wrl_ref[...]
    logits = (lax.dot_general(wh, uh, nt, preferred_element_type=F32)
              + lax.dot_general(wl, uh, nt, preferred_element_type=F32)
              + lax.dot_general(wh, ul, nt, preferred_element_type=F32)) + br_ref[...]
    eio = lax.broadcasted_iota(I32, (ne, tm), 0).astype(F32)
    vals, ids = [], []
    for _ in range(TOPK_EXPERTS):
        m = jnp.max(logits, axis=0, keepdims=True)
        idx = jnp.min(jnp.where(logits == m, eio, float(ne)), axis=0, keepdims=True)
        vals.append(m)
        ids.append(idx)
        logits = jnp.where(eio == idx, -jnp.inf, logits)
    es = [jnp.exp(v - vals[0]) for v in vals]
    den = es[0] + es[1] + es[2] + es[3]

    onehot = jnp.zeros((ne, tm), F32)
    for idx in ids:
        onehot = onehot + jnp.where(eio == idx, 1.0, 0.0)
    ri = lax.broadcasted_iota(I32, (tm, tm), 0)
    ci = lax.broadcasted_iota(I32, (tm, tm), 1)
    before = jnp.where(ri < ci, 1.0, 0.0).astype(BF16)
    base = base_ref[...]
    prefix = jnp.dot(onehot.astype(BF16), before, preferred_element_type=F32) + base
    sub = lax.broadcasted_iota(I32, (8, tm), 0)
    route = jnp.zeros((8, tm), F32)
    gates = jnp.zeros((8, tm), F32)
    for j in range(TOPK_EXPERTS):
        rank = jnp.sum(jnp.where(eio == ids[j], prefix, 0.0), axis=0, keepdims=True)
        route = jnp.where(sub == j, ids[j], route)
        route = jnp.where(sub == TOPK_EXPERTS + j, rank, route)
        gates = jnp.where(sub == j, es[j] / den, gates)
    route_ref[...] = route.astype(I32)
    gate_ref[...] = gates
    base = base + jnp.sum(onehot, axis=1, keepdims=True)
    base_ref[...] = base
    cnt_ref[...] = jnp.broadcast_to(base, cnt_ref.shape)


def _merge_call(o_a, o_b, g_a, g_b, x2, mod3, wpa, wpb, wo, ln_g, ln_b, w_router, b_router, seq, alpha):
    n_tok, d = x2.shape
    tm = _TM
    ne = N_EXPERTS
    row = lambda i: (i, 0)
    blk = pl.BlockSpec((tm, d), row)
    sm = pl.BlockSpec((8, tm), lambda i: (0, i))
    wrt = w_router.T
    wrh = wrt.astype(BF16)
    wrl = (wrt - wrh.astype(F32)).astype(BF16)
    return pl.pallas_call(
        functools.partial(_merge_kernel, alpha=alpha),
        grid=(n_tok // tm,),
        in_specs=[blk, blk, blk, blk, blk,
                  pl.BlockSpec((1, 6, d), lambda i: ((i * tm) // seq, 0, 0)),
                  _const_spec((d, d)), _const_spec((d, d)), _const_spec((d, d)),
                  _const_spec((1, d)), _const_spec((1, d)), _const_spec((ne, d)), _const_spec((ne, d)),
                  _const_spec((ne, 1))],
        out_specs=[blk, blk, sm, sm, pl.BlockSpec((ne, LANES), lambda i: (0, 0))],
        out_shape=[jax.ShapeDtypeStruct((n_tok, d), F32), jax.ShapeDtypeStruct((n_tok, d), F32),
                   jax.ShapeDtypeStruct((8, n_tok), I32), jax.ShapeDtypeStruct((8, n_tok), F32),
                   jax.ShapeDtypeStruct((ne, LANES), F32)],
        scratch_shapes=[pltpu.VMEM((ne, 1), F32)],
        compiler_params=_cparams(("arbitrary",)),
        name="merge",
    )(o_a, o_b, g_a, g_b, x2, mod3, wpa, wpb, wo, ln_g, ln_b, wrh, wrl, b_router.reshape(ne, 1))


_TD = 256
_TMB = 512


def _dispatch_kernel(pend_ref, dest_ref, u2_ref, xs_ref, zero_ref, sem):
    @pl.when(pl.program_id(0) == 0)
    def _():
        zero_ref[...] = jnp.zeros(zero_ref.shape, F32)
        for e in range(N_EXPERTS):
            end = pend_ref[e]
            start = pend_ref[e - 1] if e else 0

            @pl.when(end > start)
            def _():
                dst = xs_ref.at[pl.ds(pl.multiple_of(end - _TMB, _TMB), _TMB), :]
                cp = pltpu.make_async_copy(zero_ref, dst, sem)
                cp.start()
                cp.wait()

    def issue(t, carry):
        for j in range(TOPK_EXPERTS):
            d = dest_ref[t * TOPK_EXPERTS + j]
            pltpu.make_async_copy(u2_ref.at[pl.ds(t, 1), :], xs_ref.at[pl.ds(d, 1), :], sem).start()
        return carry

    lax.fori_loop(0, _TD, issue, 0)
    for _ in range(TOPK_EXPERTS):
        pltpu.make_async_copy(u2_ref, xs_ref.at[pl.ds(0, _TD), :], sem).wait()


def _dispatch_call(pends, dest_flat, u2, n_slots):
    n_tok, d = u2.shape
    grid_spec = pltpu.PrefetchScalarGridSpec(
        num_scalar_prefetch=1,
        grid=(n_tok // _TD,),
        in_specs=[pl.BlockSpec((_TD * TOPK_EXPERTS,), lambda i, pe: (i,), memory_space=pltpu.SMEM),
                  pl.BlockSpec((_TD, d), lambda i, pe: (i, 0))],
        out_specs=pl.BlockSpec(memory_space=pl.ANY),
        scratch_shapes=[pltpu.VMEM((_TMB, d), F32), pltpu.SemaphoreType.DMA(())],
    )
    return pl.pallas_call(
        _dispatch_kernel,
        grid_spec=grid_spec,
        out_shape=jax.ShapeDtypeStruct((n_slots, d), F32),
        compiler_params=_cparams(("arbitrary",)),
        name="dispatch",
    )(pends, dest_flat, u2)


def _expert_kernel(be_ref, nu_ref, xs_ref, w1_ref, b1_ref, w2_ref, b2_ref, y_ref, w1b_ref, w2b_ref):
    i = pl.program_id(0)
    f = w2_ref.shape[1]
    used = i < nu_ref[0]
    new_expert = jnp.logical_or(i == 0, be_ref[i] != be_ref[jnp.maximum(i - 1, 0)])

    @pl.when(jnp.logical_and(used, new_expert))
    def _():
        w1b_ref[...] = w1_ref[0].astype(BF16)
        w2b_ref[...] = w2_ref[0].astype(BF16)

    @pl.when(used)
    def _():
        h = jnp.dot(xs_ref[...].astype(BF16), w1b_ref[...], preferred_element_type=F32) + b1_ref[0]
        gate = jnp.minimum(h[:, :f], SWIGLU_LIMIT)
        up = jnp.clip(h[:, f:], -SWIGLU_LIMIT, SWIGLU_LIMIT)
        act = (up + 1.0) * gate * jax.nn.sigmoid(SWIGLU_ALPHA * gate)
        y_ref[...] = jnp.dot(act.astype(BF16), w2b_ref[...], preferred_element_type=F32) + b2_ref[0]

    @pl.when(i >= nu_ref[0])
    def _():
        y_ref[...] = jnp.zeros(y_ref.shape, F32)


def _expert_call(block_expert, n_used, xs, w1, b1, w2, b2):
    n_slots, d = xs.shape
    ne, _, f2 = w1.shape
    f = f2 // 2
    grid_spec = pltpu.PrefetchScalarGridSpec(
        num_scalar_prefetch=2,
        grid=(n_slots // _TMB,),
        in_specs=[pl.BlockSpec((_TMB, d), lambda i, be, nu: (i, 0)),
                  pl.BlockSpec((1, d, f2), lambda i, be, nu: (be[i], 0, 0)),
                  pl.BlockSpec((1, 1, f2), lambda i, be, nu: (be[i], 0, 0)),
                  pl.BlockSpec((1, f, d), lambda i, be, nu: (be[i], 0, 0)),
                  pl.BlockSpec((1, 1, d), lambda i, be, nu: (be[i], 0, 0))],
        out_specs=pl.BlockSpec((_TMB, d), lambda i, be, nu: (i, 0)),
        scratch_shapes=[pltpu.VMEM((d, f2), BF16), pltpu.VMEM((f, d), BF16)],
    )
    return pl.pallas_call(
        _expert_kernel,
        grid_spec=grid_spec,
        out_shape=jax.ShapeDtypeStruct((n_slots, d), F32),
        compiler_params=_cparams(("arbitrary",)),
        name="experts",
    )(block_expert, n_used, xs, w1, b1.reshape(ne, 1, f2), w2, b2.reshape(ne, 1, d))


_TC = 256


def _combine_kernel(dest_ref, gate_ref, x1_ref, mod_ref, g2_ref, b2_ref, y_hbm, o_ref, buf_ref, sem, *, alpha):
    def issue(t, carry):
        for j in range(TOPK_EXPERTS):
            d = dest_ref[t * TOPK_EXPERTS + j]
            pltpu.make_async_copy(y_hbm.at[pl.ds(d, 1), :], buf_ref.at[j, pl.ds(t, 1), :], sem).start()
        return carry

    lax.fori_loop(0, _TC, issue, 0, unroll=2)
    for j in range(TOPK_EXPERTS):
        pltpu.make_async_copy(y_hbm.at[pl.ds(0, _TC), :], buf_ref.at[j], sem).wait()
    gates = gate_ref[...]
    y = gates[:, 0:1] * buf_ref[0]
    for j in range(1, TOPK_EXPERTS):
        y = y + gates[:, j:j + 1] * buf_ref[j]
    o_ref[...] = _ln(alpha * x1_ref[...] + mod_ref[0, 5:6, :] * y) * g2_ref[...] + b2_ref[...]


def _combine_call(dest_flat, gates, x1, mod3, ln_g, ln_b, y, seq, alpha):
    n_tok, d = x1.shape
    tc = _TC
    row = lambda i: (i, 0)
    return pl.pallas_call(
        functools.partial(_combine_kernel, alpha=alpha),
        grid=(n_tok // tc,),
        in_specs=[pl.BlockSpec((tc * TOPK_EXPERTS,), lambda i: (i,), memory_space=pltpu.SMEM),
                  pl.BlockSpec((tc, TOPK_EXPERTS), row),
                  pl.BlockSpec((tc, d), row),
                  pl.BlockSpec((1, 6, d), lambda i: ((i * tc) // seq, 0, 0)),
                  _const_spec((1, d)), _const_spec((1, d)),
                  pl.BlockSpec(memory_space=pl.ANY)],
        out_specs=pl.BlockSpec((tc, d), row),
        out_shape=jax.ShapeDtypeStruct((n_tok, d), F32),
        scratch_shapes=[pltpu.VMEM((TOPK_EXPERTS, tc, d), F32), pltpu.SemaphoreType.DMA(())],
        compiler_params=_cparams(("arbitrary",)),
        name="combine",
    )(dest_flat, gates, x1, mod3, ln_g, ln_b, y)


def _permute_w_in(w):
    d = w.shape[0]
    s = np.cumsum([0, A_HEADS * A_LATENT, A_LATENT, IDX_HEADS * IDX_DIM, IDX_DIM, IDX_HEADS,
                   SSM_D_INNER, SSM_CONV_DIM, SSM_HEADS, d, d]).tolist()
    q, kv, iq, ik, iw, z, xbc, dt, ga, gb = [w[:, s[i]:s[i + 1]] for i in range(10)]
    pad1 = jnp.zeros((d, _SM_DT - _SM_IW - IDX_HEADS), w.dtype)
    pad2 = jnp.zeros((d, LANES - _SM_DT - SSM_HEADS), w.dtype)
    return jnp.concatenate([q, kv, iq, ik, iw, pad1, dt, pad2, z, xbc, ga, gb], axis=1).astype(BF16)


def _pad_lanes(v, fill=0.0):
    return jnp.pad(v.reshape(1, -1), ((0, 0), (0, LANES - v.shape[-1])), constant_values=fill)


def kernel(x, c, w_mod, b_mod, w_in, kv_norm_w, idx_k_norm_w, idx_k_norm_b, rel_bias, conv_w, conv_b, dt_bias,
           a_log, d_skip, ssm_norm_w, w_proj_a, w_proj_b, w_out, ln1_g, ln1_b, w_router, b_router, w1, b1, w2, b2,
           ln2_g, ln2_b):
    bsz, seq, d = x.shape
    depth = w_mod.shape[0]
    alpha = (2.0 * depth) ** 0.25
    n_tok = bsz * seq
    n_asg = n_tok * TOPK_EXPERTS
    n_blocks = n_asg // _TMB + N_EXPERTS
    n_slots = n_blocks * _TMB
    nblk = seq // _KB
    bias = _bias_tiles(rel_bias)
    x2 = x.reshape(n_tok, d)
    for l in range(depth):
        mod3 = _mod_call(c, w_mod[l], b_mod[l]).reshape(bsz, 6, d)
        q, kvn, iq, ik2, small, smt, z, xbc, g_a, g_b = _inproj_call(
            x2, mod3, _permute_w_in(w_in[l]), kv_norm_w[l].reshape(1, -1),
            _pad_lanes(idx_k_norm_w[l]), _pad_lanes(idx_k_norm_b[l]), seq)
        kvb = kvn.reshape(bsz, nblk, _KB, A_LATENT)
        kvt = jnp.swapaxes(kvb, 2, 3)
        o_a = _attn_call(q, iq, smt, ik2.reshape(2, bsz, nblk, _KB, LANES), kvt, kvb, bias, bsz, seq)
        o_b = _ssd_call(z, xbc, small, smt, conv_w[l], conv_b[l], dt_bias[l], a_log[l], d_skip[l], ssm_norm_w[l],
                        bsz, seq)
        x1, u2, route, gates, cnt = _merge_call(
            o_a, o_b, g_a, g_b, x2, mod3, w_proj_a[l].astype(BF16), w_proj_b[l].astype(BF16),
            w_out[l].astype(BF16), ln1_g[l].reshape(1, -1), ln1_b[l].reshape(1, -1), w_router[l], b_router[l],
            seq, alpha)
        counts = cnt[:, 0].astype(I32)
        padded = (counts + _TMB - 1) // _TMB * _TMB
        pends = jnp.cumsum(padded).astype(I32)
        pstarts = pends - padded
        eid = route[:TOPK_EXPERTS]
        onehot = eid[:, :, None] == jnp.arange(N_EXPERTS, dtype=I32)
        dest = jnp.sum(jnp.where(onehot, pstarts, 0), axis=-1) + route[TOPK_EXPERTS:2 * TOPK_EXPERTS]
        dest_flat = dest.T.reshape(n_asg)
        block_start = jnp.arange(n_blocks, dtype=I32) * _TMB
        block_expert = jnp.minimum(jnp.sum(block_start[:, None] >= pends[None, :], axis=1), N_EXPERTS - 1).astype(I32)
        n_used = (pends[-1:] // _TMB).astype(I32)
        xs = _dispatch_call(pends, dest_flat, u2, n_slots)
        y = _expert_call(block_expert, n_used, xs, w1[l], b1[l], w2[l], b2[l])
        x2 = _combine_call(dest_flat, gates[:TOPK_EXPERTS].T, x1, mod3, ln2_g[l].reshape(1, -1),
                           ln2_b[l].reshape(1, -1), y, seq, alpha)
    return x2.reshape(bsz, seq, d)
```

```python
import functools
import math

import jax
import jax.numpy as jnp
import numpy as np
from jax import lax
from jax.experimental import pallas as pl
from jax.experimental.pallas import tpu as pltpu

F32 = jnp.float32
BF16 = jnp.bfloat16
I32 = jnp.int32

CHUNK = 64
A_HEADS = 8
A_LATENT = 128
IDX_HEADS = 8
IDX_DIM = 64
TOPK_KEYS_MAX = 256
REL_BUCKETS = 32
REL_MAX_DIST = 128
SSM_D_INNER = 1024
SSM_HEADDIM = 64
SSM_HEADS = SSM_D_INNER // SSM_HEADDIM
SSM_GROUPS = 4
SSM_STATE = 128
SSM_CONV = 4
SSM_CONV_DIM = SSM_D_INNER + 2 * SSM_GROUPS * SSM_STATE
N_EXPERTS = 32
TOPK_EXPERTS = 4
SWIGLU_LIMIT = 7.0
SWIGLU_ALPHA = 1.702
LN_EPS = 1e-5

LANES = 128
INT_MIN = -2147483648
VMEM_LIMIT = 56 * 1024 * 1024

HI = lax.Precision.HIGHEST


def _cparams(sem):
    return pltpu.CompilerParams(dimension_semantics=sem, vmem_limit_bytes=VMEM_LIMIT)


def _ln(x):
    mu = jnp.mean(x, axis=-1, keepdims=True)
    xc = x - mu
    var = jnp.mean(xc * xc, axis=-1, keepdims=True)
    return xc * lax.rsqrt(var + LN_EPS)


def _const_spec(shape):
    nd = len(shape)
    return pl.BlockSpec(shape, lambda *_: (0,) * nd, pipeline_mode=pl.Buffered(1))


def _mod_kernel(c_ref, w_ref, b_ref, o_ref):
    c = c_ref[...]
    sc = c * jax.nn.sigmoid(c)
    o_ref[...] = jnp.dot(sc, w_ref[...], precision=HI, preferred_element_type=F32) + b_ref[...]


def _mod_call(c, w_mod, b_mod):
    bsz, d = c.shape
    n = w_mod.shape[1]
    tn = 1024
    return pl.pallas_call(
        _mod_kernel,
        grid=(n // tn,),
        in_specs=[pl.BlockSpec((bsz, d), lambda j: (0, 0)),
                  pl.BlockSpec((d, tn), lambda j: (0, j)),
                  pl.BlockSpec((1, tn), lambda j: (0, j))],
        out_specs=pl.BlockSpec((bsz, tn), lambda j: (0, j)),
        out_shape=jax.ShapeDtypeStruct((bsz, n), F32),
        compiler_params=_cparams(("arbitrary",)),
        name="mod",
    )(c, w_mod, b_mod.reshape(1, n))


_C_Q, _C_KV, _C_IQ, _C_SM, _C_Z, _C_XBC, _C_GA, _C_GB, _C_END = 0, 1024, 1152, 1664, 1792, 2816, 4864, 5888, 6912
_SM_IW = IDX_DIM
_SM_DT = IDX_DIM + SSM_HEADS


def _inproj_kernel(x_ref, mod_ref, w_ref, kvw_ref, ikw_ref, ikb_ref,
                   q_ref, kv_ref, iq_ref, ik2_ref, sm_ref, smt_ref, z_ref, xbc_ref, ga_ref, gb_ref):
    u = _ln(x_ref[...]) * (1.0 + mod_ref[0, 1:2, :]) + mod_ref[0, 0:1, :]
    ub = u.astype(BF16)

    def mm(a, b):
        return jnp.dot(ub, w_ref[:, a:b], preferred_element_type=F32)

    q_ref[...] = mm(_C_Q, _C_KV).astype(BF16)
    kv = mm(_C_KV, _C_IQ)
    kv = kv * lax.rsqrt(jnp.mean(kv * kv, axis=-1, keepdims=True) + LN_EPS)
    kv_ref[...] = (kv * kvw_ref[...]).astype(BF16)
    iq_ref[...] = mm(_C_IQ, _C_SM).astype(BF16)
    g = mm(_C_SM, _C_Z)
    lane = lax.broadcasted_iota(I32, g.shape, 1)
    is_ik = lane < IDX_DIM
    mu = jnp.sum(jnp.where(is_ik, g, 0.0), axis=-1, keepdims=True) * (1.0 / IDX_DIM)
    gc = g - mu
    var = jnp.sum(jnp.where(is_ik, gc * gc, 0.0), axis=-1, keepdims=True) * (1.0 / IDX_DIM)
    ik = jnp.where(is_ik, gc * lax.rsqrt(var + LN_EPS) * ikw_ref[...] + ikb_ref[...], 0.0)
    ik2_ref[0] = ik.astype(BF16)
    ik2_ref[1] = pltpu.roll(ik, IDX_DIM, axis=1).astype(BF16)
    sm = jnp.where(lane < _SM_IW + IDX_HEADS, g * (IDX_HEADS ** -0.5), g)
    sm_ref[...] = sm
    smt_ref[0] = sm.T
    z_ref[...] = mm(_C_Z, _C_XBC)
    xbc_ref[...] = mm(_C_XBC, _C_GA)
    ga_ref[...] = mm(_C_GA, _C_GB)
    gb_ref[...] = mm(_C_GB, _C_END)


def _inproj_call(x2, mod3, w_perm, kvw, ikw, ikb, seq):
    n_tok, d = x2.shape
    tm = 256
    row = lambda i: (i, 0)

    def ospec(n):
        return pl.BlockSpec((tm, n), row)

    spt = seq // tm
    sd = jax.ShapeDtypeStruct
    return pl.pallas_call(
        _inproj_kernel,
        grid=(n_tok // tm,),
        in_specs=[pl.BlockSpec((tm, d), row),
                  pl.BlockSpec((1, 6, d), lambda i: (i // spt, 0, 0)),
                  _const_spec(w_perm.shape), _const_spec((1, 128)), _const_spec((1, 128)), _const_spec((1, 128))],
        out_specs=[ospec(1024), ospec(128), ospec(512),
                   pl.BlockSpec((2, tm, LANES), lambda i: (0, i, 0)),
                   ospec(LANES),
                   pl.BlockSpec((1, LANES, tm), lambda i: (i // spt, 0, i % spt)),
                   ospec(1024), ospec(2048), ospec(1024), ospec(1024)],
        out_shape=[sd((n_tok, 1024), BF16), sd((n_tok, 128), BF16), sd((n_tok, 512), BF16),
                   sd((2, n_tok, LANES), BF16), sd((n_tok, LANES), F32), sd((n_tok // seq, LANES, seq), F32),
                   sd((n_tok, 1024), F32), sd((n_tok, 2048), F32), sd((n_tok, 1024), F32), sd((n_tok, 1024), F32)],
        compiler_params=_cparams(("arbitrary",)),
        name="inproj",
    )(x2, mod3, w_perm, kvw, ikw, ikb)


_TQ = 256
_KB = 256
_BIAS_STEP = math.gcd(_TQ, _KB)
_NEAR_DELTAS = tuple(range(-_KB, 1, _BIAS_STEP))
assert REL_MAX_DIST <= _BIAS_STEP
_PBITS = 15
_GUARDS = -2147450880


def _t5_bucket(rel):
    half = REL_BUCKETS // 2
    max_exact = half // 2
    ret = (rel > 0).astype(jnp.int32) * half
    n = jnp.abs(rel)
    nf = jnp.maximum(n, 1).astype(jnp.float32)
    large = max_exact + (jnp.log(nf / max_exact) / math.log(REL_MAX_DIST / max_exact)
                         * (half - max_exact)).astype(jnp.int32)
    large = jnp.minimum(large, half - 1)
    return ret + jnp.where(n < max_exact, n, large)


def _bias_tiles(rel_bias):
    i = jnp.arange(_TQ, dtype=jnp.int32)[:, None]
    c = jnp.arange(_KB, dtype=jnp.int32)[None, :]
    rel = [c + delta - i for delta in _NEAR_DELTAS] + [jnp.full((_TQ, _KB), -REL_MAX_DIST, jnp.int32)]
    bucket = _t5_bucket(jnp.stack(rel))
    b = jnp.zeros((len(rel), A_HEADS, _TQ, _KB), F32)
    for k in range(REL_BUCKETS):
        b = jnp.where((bucket == k)[:, None], rel_bias[k].astype(F32)[None, :, None, None], b)
    return (b * (A_LATENT ** 0.5)).reshape(len(rel), A_HEADS * _TQ, _KB)


def _attn_kernel(q_ref, iq_ref, iwt_ref, ik2_ref, kvt_ref, kv_ref, bias_ref, o_ref,
                 keys_ref, pk_ref, s_ref, mx_ref, l_ref, acc_ref, *, n_sel, jbits):
    tq, kb = _TQ, _KB
    qs = pl.program_id(1) * tq
    nkb = (qs + tq + kb - 1) // kb
    nt = (((1,), (1,)), ((), ()))

    qpos = lax.broadcasted_iota(I32, (1, tq), 1) + qs
    qend = (qpos // CHUNK + 1) * CHUNK
    krow = lax.broadcasted_iota(I32, (kb, tq), 0)
    iq = iq_ref[...]
    iwt = iwt_ref[0] * (IDX_DIM ** -0.5)

    def score_body(j, carry):
        acc = jnp.zeros((kb, tq), F32)
        for p in range(IDX_HEADS // 2):
            pair = iq[:, p * LANES:(p + 1) * LANES]
            for par in range(2):
                h = 2 * p + par
                s = lax.dot_general(ik2_ref[par, 0, j], pair, nt, preferred_element_type=F32)
                acc = acc + jnp.maximum(s, 0.0) * iwt[h:h + 1, :]
        bits = pltpu.bitcast(acc, I32)
        key = jnp.where(bits < 0, bits ^ 0x7FFFFFFF, bits)
        key = jnp.where(bits == INT_MIN, 0, key)
        key = jnp.where(krow + j * kb < qend, key, INT_MIN)
        keys_ref[j] = key
        top = (key >> (32 - _PBITS)) + (1 << (_PBITS - 1))
        pk_ref[j] = (top[:kb // 2] << 16) | top[kb // 2:] | _GUARDS
        return carry

    lax.fori_loop(0, nkb, score_body, 0)

    def count_top(cand):
        both = (cand << 16) | cand

        def body(j, acc):
            g = ((pk_ref[j] - both) >> _PBITS) & 0x00010001
            parts = [g[8 * i:8 * i + 8] for i in range(kb // 16)]
            while len(parts) > 1:
                parts = [parts[i] + parts[i + 1] for i in range(0, len(parts), 2)]
            return acc + parts[0]
        acc = lax.fori_loop(0, nkb, body, jnp.zeros((8, tq), I32))
        return jnp.sum(((acc & 0xFFFF) + (acc >> 16)).astype(F32), axis=0, keepdims=True)

    def top_body(it, tu):
        cu = tu | (jnp.int32(1) << (_PBITS - 1 - it))
        return jnp.where(count_top(cu) >= n_sel, cu, tu)

    top_bits = lax.fori_loop(0, _PBITS, top_body, jnp.zeros((1, tq), I32))

    def count(pred):
        def body(j, acc):
            m = jnp.where(pred(keys_ref[j], krow + j * kb), 1.0, 0.0)
            parts = [m[8 * i:8 * i + 8] for i in range(kb // 8)]
            while len(parts) > 1:
                parts = [parts[i] + parts[i + 1] for i in range(0, len(parts), 2)]
            return acc + parts[0]
        acc = lax.fori_loop(0, nkb, body, jnp.zeros((8, tq), F32))
        return jnp.sum(acc, axis=0, keepdims=True)

    def bit_body(it, tu):
        cu = tu | (jnp.int32(1) << (31 - it))
        cs = cu ^ INT_MIN
        cnt = count(lambda k, kidx: k >= cs)
        return jnp.where(cnt >= n_sel, cu, tu)

    thr = lax.fori_loop(_PBITS, 32, bit_body, top_bits << (32 - _PBITS)) ^ INT_MIN
    n_ge = count(lambda k, kidx: k >= thr)

    def tie_search():
        need = n_sel - count(lambda k, kidx: k > thr)

        def j_body(it, jj):
            cj = jj | (jnp.int32(1) << (jbits - 1 - it))
            f = count(lambda k, kidx: (k == thr) & (kidx < cj))
            return jnp.where(f <= need, cj, jj)

        return lax.fori_loop(0, jbits, j_body, jnp.zeros((1, tq), I32))

    jj = lax.cond(jnp.max(n_ge) > n_sel, tie_search, lambda: jnp.full((1, tq), (1 << jbits) - 1, I32))

    q = q_ref[...]
    q_all = jnp.concatenate([q[:, h * LANES:(h + 1) * LANES] for h in range(A_HEADS)], axis=0)
    eye = jnp.where(lax.broadcasted_iota(I32, (tq, tq), 0) == lax.broadcasted_iota(I32, (tq, tq), 1),
                    1.0, 0.0).astype(BF16)
    mx_ref[...] = jnp.full(mx_ref.shape, -jnp.inf, F32)
    exp2_scale = A_LATENT ** -0.5 * math.log2(math.e)

    def logit_body(j, carry):
        k = keys_ref[j]
        kidx = krow + j * kb
        selt = ((k > thr) | ((k == thr) & (kidx < jj))) & (kidx < qend)
        sel = lax.dot_general(eye, jnp.where(selt, 1.0, 0.0).astype(BF16), nt, preferred_element_type=F32) > 0.5
        s = jnp.dot(q_all, kvt_ref[0, j], preferred_element_type=F32)
        v = (j * kb - qs + kb) // _BIAS_STEP
        s = s + bias_ref[jnp.where(v < 0, len(_NEAR_DELTAS), v)]
        s = jnp.concatenate([jnp.where(sel, s[h * tq:(h + 1) * tq], -jnp.inf) for h in range(A_HEADS)], axis=0)
        s_ref[j] = s
        mx = mx_ref[...]
        for c in range(kb // LANES):
            mx = jnp.maximum(mx, s[:, c * LANES:(c + 1) * LANES])
        mx_ref[...] = mx
        return carry

    lax.fori_loop(0, nkb, logit_body, 0)
    m = jnp.max(mx_ref[...], axis=1, keepdims=True)
    m = jnp.where(m == -jnp.inf, 0.0, m)
    mx_ref[...] = jnp.broadcast_to(m, mx_ref.shape)
    l_ref[...] = jnp.zeros(l_ref.shape, F32)
    acc_ref[...] = jnp.zeros(acc_ref.shape, F32)

    def pv_body(j, carry):
        mb = mx_ref[...]
        s = s_ref[j]
        ps = [jnp.exp2((s[:, c * LANES:(c + 1) * LANES] - mb) * exp2_scale) for c in range(kb // LANES)]
        lsum = l_ref[...]
        for pc in ps:
            lsum = lsum + pc
        l_ref[...] = lsum
        p = jnp.concatenate(ps, axis=1).astype(BF16)
        acc_ref[...] += jnp.dot(p, kv_ref[0, j], preferred_element_type=F32)
        return carry

    lax.fori_loop(0, nkb, pv_body, 0)
    o = acc_ref[...] / jnp.sum(l_ref[...], axis=1, keepdims=True)
    for h in range(A_HEADS):
        o_ref[:, h * LANES:(h + 1) * LANES] = o[h * tq:(h + 1) * tq].astype(o_ref.dtype)


def _attn_call(q, iq, iwt, ik2, kvt, kvb, bias, bsz, seq):
    tq, kb = _TQ, _KB
    nq = seq // tq
    nblk = seq // kb
    n_sel = min(TOPK_KEYS_MAX, seq // 4)
    jbits = int(seq).bit_length()
    row = lambda b, i: (b * nq + i, 0)
    kern = functools.partial(_attn_kernel, n_sel=float(n_sel), jbits=jbits)
    return pl.pallas_call(
        kern,
        grid=(bsz, nq),
        in_specs=[pl.BlockSpec((tq, A_HEADS * A_LATENT), row),
                  pl.BlockSpec((tq, IDX_HEADS * IDX_DIM), row),
                  pl.BlockSpec((1, IDX_HEADS, tq), lambda b, i: (b, _SM_IW // IDX_HEADS, i)),
                  pl.BlockSpec((2, 1, nblk, kb, LANES), lambda b, i: (0, b, 0, 0, 0)),
                  pl.BlockSpec((1, nblk, A_LATENT, kb), lambda b, i: (b, 0, 0, 0)),
                  pl.BlockSpec((1, nblk, kb, A_LATENT), lambda b, i: (b, 0, 0, 0)),
                  _const_spec(bias.shape)],
        out_specs=pl.BlockSpec((tq, A_HEADS * A_LATENT), row),
        out_shape=jax.ShapeDtypeStruct((bsz * seq, A_HEADS * A_LATENT), BF16),
        scratch_shapes=[pltpu.VMEM((nblk, kb, tq), I32),
                        pltpu.VMEM((nblk, kb // 2, tq), I32),
                        pltpu.VMEM((nblk, A_HEADS * tq, kb), F32),
                        pltpu.VMEM((A_HEADS * tq, LANES), F32),
                        pltpu.VMEM((A_HEADS * tq, LANES), F32),
                        pltpu.VMEM((A_HEADS * tq, A_LATENT), F32)],
        compiler_params=_cparams(("arbitrary", "arbitrary")),
        name="attention",
    )(q, iq, iwt, ik2, kvt, kvb, bias)


_SSD_L = 256
_PAIRS = SSM_HEADS // 2


def _ssd_kernel(z_ref, xbc_ref, sm_ref, dtt_ref, cw_ref, cb_ref, dtb_ref, dtbt_ref, al_ref, alt_ref,
                dsk_ref, nw_ref, o_ref, ext_ref, state_ref, y_ref):
    L = _SSD_L
    hd = SSM_HEADDIM

    @pl.when(pl.program_id(1) == 0)
    def _():
        ext_ref[0:8, :] = jnp.zeros((8, SSM_CONV_DIM), F32)
        state_ref[...] = jnp.zeros(state_ref.shape, F32)

    x = xbc_ref[...]
    ext_ref[8:8 + L, :] = x
    w = cw_ref[...]
    conv = x * w[3:4] + cb_ref[...]
    for k in range(1, SSM_CONV):
        conv = conv + ext_ref[8 - k:8 - k + L, :] * w[SSM_CONV - 1 - k:SSM_CONV - k]
    ext_ref[0:8, :] = x[L - 8:L]
    act = conv * jax.nn.sigmoid(conv)
    xs = act[:, :SSM_D_INNER]
    boff = SSM_D_INNER
    coff = SSM_D_INNER + SSM_GROUPS * SSM_STATE

    def softplus(v):
        return jnp.maximum(v, 0.0) + jnp.log1p(jnp.exp(-jnp.abs(v)))

    dt = softplus(sm_ref[:, _SM_DT:_SM_DT + SSM_HEADS] + dtb_ref[...])
    dtt = softplus(dtt_ref[0] + dtbt_ref[...])
    a_col = dt * (-jnp.exp(al_ref[...]))
    a_row = dtt * (-jnp.exp(alt_ref[...]))
    ri = lax.broadcasted_iota(I32, (L, L), 0)
    ci = lax.broadcasted_iota(I32, (L, L), 1)
    causal = ci <= ri
    cs_col = jnp.dot(jnp.where(causal, 1.0, 0.0), a_col, precision=HI, preferred_element_type=F32)
    cs_row = jnp.dot(a_row, jnp.where(ri <= ci, 1.0, 0.0), precision=HI, preferred_element_type=F32)
    cs_last = cs_col[L - 1:L, :]
    lane = lax.broadcasted_iota(I32, (1, LANES), 1)
    lo = lane < hd
    sub = lax.broadcasted_iota(I32, (LANES, 1), 0)

    for g in range(SSM_GROUPS):
        bm = act[:, boff + g * SSM_STATE: boff + (g + 1) * SSM_STATE].astype(BF16)
        cm = act[:, coff + g * SSM_STATE: coff + (g + 1) * SSM_STATE].astype(BF16)
        cb = lax.dot_general(cm, bm, (((1,), (1,)), ((), ())), preferred_element_type=F32)
        for pp in range(_PAIRS // SSM_GROUPS):
            p = g * (_PAIRS // SSM_GROUPS) + pp
            h0, h1 = 2 * p, 2 * p + 1
            xp = xs[:, p * LANES:(p + 1) * LANES]
            dtl = jnp.where(lo, dt[:, h0:h0 + 1], dt[:, h1:h1 + 1])
            xdt = xp * dtl
            csl = jnp.where(lo, cs_col[:, h0:h0 + 1], cs_col[:, h1:h1 + 1])
            last = jnp.where(lo, cs_last[:, h0:h0 + 1], cs_last[:, h1:h1 + 1])
            ydiag = jnp.zeros((L, LANES), F32)
            for h, msk in ((h0, lo), (h1, jnp.logical_not(lo))):
                seg = cs_col[:, h:h + 1] - cs_row[h:h + 1, :]
                gm = (cb * jnp.exp(jnp.where(causal, seg, -jnp.inf))).astype(BF16)
                ydiag = ydiag + jnp.dot(gm, jnp.where(msk, xdt, 0.0).astype(BF16), preferred_element_type=F32)
            prev = state_ref[p]
            yoff = lax.dot_general(cm, prev.astype(BF16), (((1,), (1,)), ((), ())), preferred_element_type=F32)
            y_ref[:, p * LANES:(p + 1) * LANES] = ydiag + yoff * jnp.exp(csl) + xp * dsk_ref[:, p * LANES:(p + 1) * LANES]
            wx = (xdt * jnp.exp(last - csl)).astype(BF16)
            st = lax.dot_general(wx, bm, (((0,), (0,)), ((), ())), preferred_element_type=F32)
            cdec = jnp.where(sub < hd, jnp.exp(cs_last[:, h0:h0 + 1]), jnp.exp(cs_last[:, h1:h1 + 1]))
            state_ref[p] = prev * cdec + st

    z = z_ref[...]
    y = y_ref[...] * (z * jax.nn.sigmoid(z))
    gw = SSM_D_INNER // SSM_GROUPS
    for g in range(SSM_GROUPS):
        yg = y[:, g * gw:(g + 1) * gw]
        yg = yg * lax.rsqrt(jnp.mean(yg * yg, axis=-1, keepdims=True) + LN_EPS)
        o_ref[:, g * gw:(g + 1) * gw] = (yg * nw_ref[:, g * gw:(g + 1) * gw]).astype(o_ref.dtype)


def _ssd_call(z, xbc, small, smt, conv_w, conv_b, dt_bias, a_log, d_skip, norm_w, bsz, seq):
    L = _SSD_L
    nc = seq // L
    row = lambda b, c: (b * nc + c, 0)
    h = SSM_HEADS
    return pl.pallas_call(
        _ssd_kernel,
        grid=(bsz, nc),
        in_specs=[pl.BlockSpec((L, SSM_D_INNER), row),
                  pl.BlockSpec((L, SSM_CONV_DIM), row),
                  pl.BlockSpec((L, LANES), row),
                  pl.BlockSpec((1, h, L), lambda b, c: (b, _SM_DT // h, c)),
                  _const_spec((SSM_CONV, SSM_CONV_DIM)), _const_spec((1, SSM_CONV_DIM)),
                  _const_spec((1, h)), _const_spec((h, 1)), _const_spec((1, h)), _const_spec((h, 1)),
                  _const_spec((1, SSM_D_INNER)), _const_spec((1, SSM_D_INNER))],
        out_specs=pl.BlockSpec((L, SSM_D_INNER), row),
        out_shape=jax.ShapeDtypeStruct((bsz * seq, SSM_D_INNER), BF16),
        scratch_shapes=[pltpu.VMEM((L + 8, SSM_CONV_DIM), F32),
                        pltpu.VMEM((_PAIRS, LANES, SSM_STATE), F32),
                        pltpu.VMEM((L, SSM_D_INNER), F32)],
        compiler_params=_cparams(("arbitrary", "arbitrary")),
        name="ssd",
    )(z, xbc, small, smt, conv_w, conv_b.reshape(1, -1), dt_bias.reshape(1, h), dt_bias.reshape(h, 1),
      a_log.reshape(1, h), a_log.reshape(h, 1), jnp.repeat(d_skip, SSM_HEADDIM).reshape(1, -1),
      norm_w.reshape(1, -1))


_TM = 512


def _merge_kernel(oa_ref, ob_ref, ga_ref, gb_ref, x_ref, mod_ref, wpa_ref, wpb_ref, wo_ref, g1_ref, b1_ref,
                  wrh_ref, wrl_ref, br_ref, x1_ref, u2_ref, route_ref, gate_ref, cnt_ref, base_ref, *, alpha):
    tm = _TM
    ne = N_EXPERTS

    @pl.when(pl.program_id(0) == 0)
    def _():
        base_ref[...] = jnp.zeros(base_ref.shape, F32)

    ma = jnp.dot(oa_ref[...], wpa_ref[...], preferred_element_type=F32)
    mb = jnp.dot(ob_ref[...], wpb_ref[...], preferred_element_type=F32)
    merged = jax.nn.sigmoid(ga_ref[...]) * ma + jax.nn.sigmoid(gb_ref[...]) * mb
    t = jnp.dot(merged.astype(BF16), wo_ref[...], preferred_element_type=F32)
    x1 = _ln(alpha * x_ref[...] + mod_ref[0, 2:3, :] * t) * g1_ref[...] + b1_ref[...]
    x1_ref[...] = x1
    u2 = _ln(x1) * (1.0 + mod_ref[0, 4:5, :]) + mod_ref[0, 3:4, :]
    u2_ref[...] = u2
    nt = (((1,), (1,)), ((), ()))
    uh = u2.astype(BF16)
    ul = (u2 - uh.astype(F32)).astype(BF16)
    wh, wl = wrh_ref[...], wrl_ref[...]
    logits = (lax.dot_general(wh, uh, nt, preferred_element_type=F32)
              + lax.dot_general(wl, uh, nt, preferred_element_type=F32)
              + lax.dot_general(wh, ul, nt, preferred_element_type=F32)) + br_ref[...]
    eio = lax.broadcasted_iota(I32, (ne, tm), 0).astype(F32)
    vals, ids = [], []
    for _ in range(TOPK_EXPERTS):
        m = jnp.max(logits, axis=0, keepdims=True)
        idx = jnp.min(jnp.where(logits == m, eio, float(ne)), axis=0, keepdims=True)
        vals.append(m)
        ids.append(idx)
        logits = jnp.where(eio == idx, -jnp.inf, logits)
    es = [jnp.exp(v - vals[0]) for v in vals]
    den = es[0] + es[1] + es[2] + es[3]

    onehot = jnp.zeros((ne, tm), F32)
    for idx in ids:
        onehot = onehot + jnp.where(eio == idx, 1.0, 0.0)
    ri = lax.broadcasted_iota(I32, (tm, tm), 0)
    ci = lax.broadcasted_iota(I32, (tm, tm), 1)
    before = jnp.where(ri < ci, 1.0, 0.0).astype(BF16)
    base = base_ref[...]
    prefix = jnp.dot(onehot.astype(BF16), before, preferred_element_type=F32) + base
    sub = lax.broadcasted_iota(I32, (8, tm), 0)
    route = jnp.zeros((8, tm), F32)
    gates = jnp.zeros((8, tm), F32)
    for j in range(TOPK_EXPERTS):
        rank = jnp.sum(jnp.where(eio == ids[j], prefix, 0.0), axis=0, keepdims=True)
        route = jnp.where(sub == j, ids[j], route)
        route = jnp.where(sub == TOPK_EXPERTS + j, rank, route)
        gates = jnp.where(sub == j, es[j] / den, gates)
    route_ref[...] = route.astype(I32)
    gate_ref[...] = gates
    base = base + jnp.sum(onehot, axis=1, keepdims=True)
    base_ref[...] = base
    cnt_ref[...] = jnp.broadcast_to(base, cnt_ref.shape)


def _merge_call(o_a, o_b, g_a, g_b, x2, mod3, wpa, wpb, wo, ln_g, ln_b, w_router, b_router, seq, alpha):
    n_tok, d = x2.shape
    tm = _TM
    ne = N_EXPERTS
    row = lambda i: (i, 0)
    blk = pl.BlockSpec((tm, d), row)
    sm = pl.BlockSpec((8, tm), lambda i: (0, i))
    wrt = w_router.T
    wrh = wrt.astype(BF16)
    wrl = (wrt - wrh.astype(F32)).astype(BF16)
    return pl.pallas_call(
        functools.partial(_merge_kernel, alpha=alpha),
        grid=(n_tok // tm,),
        in_specs=[blk, blk, blk, blk, blk,
                  pl.BlockSpec((1, 6, d), lambda i: ((i * tm) // seq, 0, 0)),
                  _const_spec((d, d)), _const_spec((d, d)), _const_spec((d, d)),
                  _const_spec((1, d)), _const_spec((1, d)), _const_spec((ne, d)), _const_spec((ne, d)),
                  _const_spec((ne, 1))],
        out_specs=[blk, blk, sm, sm, pl.BlockSpec((ne, LANES), lambda i: (0, 0))],
        out_shape=[jax.ShapeDtypeStruct((n_tok, d), F32), jax.ShapeDtypeStruct((n_tok, d), F32),
                   jax.ShapeDtypeStruct((8, n_tok), I32), jax.ShapeDtypeStruct((8, n_tok), F32),
                   jax.ShapeDtypeStruct((ne, LANES), F32)],
        scratch_shapes=[pltpu.VMEM((ne, 1), F32)],
        compiler_params=_cparams(("arbitrary",)),
        name="merge",
    )(o_a, o_b, g_a, g_b, x2, mod3, wpa, wpb, wo, ln_g, ln_b, wrh, wrl, b_router.reshape(ne, 1))


_TD = 256
_TMB = 512


def _dispatch_kernel(pend_ref, dest_ref, u2_ref, xs_ref, zero_ref, sem):
    @pl.when(pl.program_id(0) == 0)
    def _():
        zero_ref[...] = jnp.zeros(zero_ref.shape, F32)
        for e in range(N_EXPERTS):
            end = pend_ref[e]
            start = pend_ref[e - 1] if e else 0

            @pl.when(end > start)
            def _():
                dst = xs_ref.at[pl.ds(pl.multiple_of(end - _TMB, _TMB), _TMB), :]
                cp = pltpu.make_async_copy(zero_ref, dst, sem)
                cp.start()
                cp.wait()

    def issue(t, carry):
        for j in range(TOPK_EXPERTS):
            d = dest_ref[t * TOPK_EXPERTS + j]
            pltpu.make_async_copy(u2_ref.at[pl.ds(t, 1), :], xs_ref.at[pl.ds(d, 1), :], sem).start()
        return carry

    lax.fori_loop(0, _TD, issue, 0)
    for _ in range(TOPK_EXPERTS):
        pltpu.make_async_copy(u2_ref, xs_ref.at[pl.ds(0, _TD), :], sem).wait()


def _dispatch_call(pends, dest_flat, u2, n_slots):
    n_tok, d = u2.shape
    grid_spec = pltpu.PrefetchScalarGridSpec(
        num_scalar_prefetch=1,
        grid=(n_tok // _TD,),
        in_specs=[pl.BlockSpec((_TD * TOPK_EXPERTS,), lambda i, pe: (i,), memory_space=pltpu.SMEM),
                  pl.BlockSpec((_TD, d), lambda i, pe: (i, 0))],
        out_specs=pl.BlockSpec(memory_space=pl.ANY),
        scratch_shapes=[pltpu.VMEM((_TMB, d), F32), pltpu.SemaphoreType.DMA(())],
    )
    return pl.pallas_call(
        _dispatch_kernel,
        grid_spec=grid_spec,
        out_shape=jax.ShapeDtypeStruct((n_slots, d), F32),
        compiler_params=_cparams(("arbitrary",)),
        name="dispatch",
    )(pends, dest_flat, u2)


def _expert_kernel(be_ref, nu_ref, xs_ref, w1_ref, b1_ref, w2_ref, b2_ref, y_ref, w1b_ref, w2b_ref):
    i = pl.program_id(0)
    f = w2_ref.shape[1]
    used = i < nu_ref[0]
    new_expert = jnp.logical_or(i == 0, be_ref[i] != be_ref[jnp.maximum(i - 1, 0)])

    @pl.when(jnp.logical_and(used, new_expert))
    def _():
        w1b_ref[...] = w1_ref[0].astype(BF16)
        w2b_ref[...] = w2_ref[0].astype(BF16)

    @pl.when(used)
    def _():
        h = jnp.dot(xs_ref[...].astype(BF16), w1b_ref[...], preferred_element_type=F32) + b1_ref[0]
        gate = jnp.minimum(h[:, :f], SWIGLU_LIMIT)
        up = jnp.clip(h[:, f:], -SWIGLU_LIMIT, SWIGLU_LIMIT)
        act = (up + 1.0) * gate * jax.nn.sigmoid(SWIGLU_ALPHA * gate)
        y_ref[...] = jnp.dot(act.astype(BF16), w2b_ref[...], preferred_element_type=F32) + b2_ref[0]

    @pl.when(i >= nu_ref[0])
    def _():
        y_ref[...] = jnp.zeros(y_ref.shape, F32)


def _expert_call(block_expert, n_used, xs, w1, b1, w2, b2):
    n_slots, d = xs.shape
    ne, _, f2 = w1.shape
    f = f2 // 2
    grid_spec = pltpu.PrefetchScalarGridSpec(
        num_scalar_prefetch=2,
        grid=(n_slots // _TMB,),
        in_specs=[pl.BlockSpec((_TMB, d), lambda i, be, nu: (i, 0)),
                  pl.BlockSpec((1, d, f2), lambda i, be, nu: (be[i], 0, 0)),
                  pl.BlockSpec((1, 1, f2), lambda i, be, nu: (be[i], 0, 0)),
                  pl.BlockSpec((1, f, d), lambda i, be, nu: (be[i], 0, 0)),
                  pl.BlockSpec((1, 1, d), lambda i, be, nu: (be[i], 0, 0))],
        out_specs=pl.BlockSpec((_TMB, d), lambda i, be, nu: (i, 0)),
        scratch_shapes=[pltpu.VMEM((d, f2), BF16), pltpu.VMEM((f, d), BF16)],
    )
    return pl.pallas_call(
        _expert_kernel,
        grid_spec=grid_spec,
        out_shape=jax.ShapeDtypeStruct((n_slots, d), F32),
        compiler_params=_cparams(("arbitrary",)),
        name="experts",
    )(block_expert, n_used, xs, w1, b1.reshape(ne, 1, f2), w2, b2.reshape(ne, 1, d))


_TC = 256


def _combine_kernel(dest_ref, gate_ref, x1_ref, mod_ref, g2_ref, b2_ref, y_hbm, o_ref, buf_ref, sem, *, alpha):
    def issue(t, carry):
        for j in range(TOPK_EXPERTS):
            d = dest_ref[t * TOPK_EXPERTS + j]
            pltpu.make_async_copy(y_hbm.at[pl.ds(d, 1), :], buf_ref.at[j, pl.ds(t, 1), :], sem).start()
        return carry

    lax.fori_loop(0, _TC, issue, 0, unroll=2)
    for j in range(TOPK_EXPERTS):
        pltpu.make_async_copy(y_hbm.at[pl.ds(0, _TC), :], buf_ref.at[j], sem).wait()
    gates = gate_ref[...]
    y = gates[:, 0:1] * buf_ref[0]
    for j in range(1, TOPK_EXPERTS):
        y = y + gates[:, j:j + 1] * buf_ref[j]
    o_ref[...] = _ln(alpha * x1_ref[...] + mod_ref[0, 5:6, :] * y) * g2_ref[...] + b2_ref[...]


def _combine_call(dest_flat, gates, x1, mod3, ln_g, ln_b, y, seq, alpha):
    n_tok, d = x1.shape
    tc = _TC
    row = lambda i: (i, 0)
    return pl.pallas_call(
        functools.partial(_combine_kernel, alpha=alpha),
        grid=(n_tok // tc,),
        in_specs=[pl.BlockSpec((tc * TOPK_EXPERTS,), lambda i: (i,), memory_space=pltpu.SMEM),
                  pl.BlockSpec((tc, TOPK_EXPERTS), row),
                  pl.BlockSpec((tc, d), row),
                  pl.BlockSpec((1, 6, d), lambda i: ((i * tc) // seq, 0, 0)),
                  _const_spec((1, d)), _const_spec((1, d)),
                  pl.BlockSpec(memory_space=pl.ANY)],
        out_specs=pl.BlockSpec((tc, d), row),
        out_shape=jax.ShapeDtypeStruct((n_tok, d), F32),
        scratch_shapes=[pltpu.VMEM((TOPK_EXPERTS, tc, d), F32), pltpu.SemaphoreType.DMA(())],
        compiler_params=_cparams(("arbitrary",)),
        name="combine",
    )(dest_flat, gates, x1, mod3, ln_g, ln_b, y)


def _permute_w_in(w):
    d = w.shape[0]
    s = np.cumsum([0, A_HEADS * A_LATENT, A_LATENT, IDX_HEADS * IDX_DIM, IDX_DIM, IDX_HEADS,
                   SSM_D_INNER, SSM_CONV_DIM, SSM_HEADS, d, d]).tolist()
    q, kv, iq, ik, iw, z, xbc, dt, ga, gb = [w[:, s[i]:s[i + 1]] for i in range(10)]
    pad1 = jnp.zeros((d, _SM_DT - _SM_IW - IDX_HEADS), w.dtype)
    pad2 = jnp.zeros((d, LANES - _SM_DT - SSM_HEADS), w.dtype)
    return jnp.concatenate([q, kv, iq, ik, iw, pad1, dt, pad2, z, xbc, ga, gb], axis=1).astype(BF16)


def _pad_lanes(v, fill=0.0):
    return jnp.pad(v.reshape(1, -1), ((0, 0), (0, LANES - v.shape[-1])), constant_values=fill)


def kernel(x, c, w_mod, b_mod, w_in, kv_norm_w, idx_k_norm_w, idx_k_norm_b, rel_bias, conv_w, conv_b, dt_bias,
           a_log, d_skip, ssm_norm_w, w_proj_a, w_proj_b, w_out, ln1_g, ln1_b, w_router, b_router, w1, b1, w2, b2,
           ln2_g, ln2_b):
    bsz, seq, d = x.shape
    depth = w_mod.shape[0]
    alpha = (2.0 * depth) ** 0.25
    n_tok = bsz * seq
    n_asg = n_tok * TOPK_EXPERTS
    n_blocks = n_asg // _TMB + N_EXPERTS
    n_slots = n_blocks * _TMB
    nblk = seq // _KB
    bias = _bias_tiles(rel_bias)
    x2 = x.reshape(n_tok, d)
    for l in range(depth):
        mod3 = _mod_call(c, w_mod[l], b_mod[l]).reshape(bsz, 6, d)
        q, kvn, iq, ik2, small, smt, z, xbc, g_a, g_b = _inproj_call(
            x2, mod3, _permute_w_in(w_in[l]), kv_norm_w[l].reshape(1, -1),
            _pad_lanes(idx_k_norm_w[l]), _pad_lanes(idx_k_norm_b[l]), seq)
        kvb = kvn.reshape(bsz, nblk, _KB, A_LATENT)
        kvt = jnp.swapaxes(kvb, 2, 3)
        o_a = _attn_call(q, iq, smt, ik2.reshape(2, bsz, nblk, _KB, LANES), kvt, kvb, bias, bsz, seq)
        o_b = _ssd_call(z, xbc, small, smt, conv_w[l], conv_b[l], dt_bias[l], a_log[l], d_skip[l], ssm_norm_w[l],
                        bsz, seq)
        x1, u2, route, gates, cnt = _merge_call(
            o_a, o_b, g_a, g_b, x2, mod3, w_proj_a[l].astype(BF16), w_proj_b[l].astype(BF16),
            w_out[l].astype(BF16), ln1_g[l].reshape(1, -1), ln1_b[l].reshape(1, -1), w_router[l], b_router[l],
            seq, alpha)
        counts = cnt[:, 0].astype(I32)
        padded = (counts + _TMB - 1) // _TMB * _TMB
        pends = jnp.cumsum(padded).astype(I32)
        pstarts = pends - padded
        eid = route[:TOPK_EXPERTS]
        onehot = eid[:, :, None] == jnp.arange(N_EXPERTS, dtype=I32)
        dest = jnp.sum(jnp.where(onehot, pstarts, 0), axis=-1) + route[TOPK_EXPERTS:2 * TOPK_EXPERTS]
        dest_flat = dest.T.reshape(n_asg)
        block_start = jnp.arange(n_blocks, dtype=I32) * _TMB
        block_expert = jnp.minimum(jnp.sum(block_start[:, None] >= pends[None, :], axis=1), N_EXPERTS - 1).astype(I32)
        n_used = (pends[-1:] // _TMB).astype(I32)
        xs = _dispatch_call(pends, dest_flat, u2, n_slots)
        y = _expert_call(block_expert, n_used, xs, w1[l], b1[l], w2[l], b2[l])
        x2 = _combine_call(dest_flat, gates[:TOPK_EXPERTS].T, x1, mod3, ln2_g[l].reshape(1, -1),
                           ln2_b[l].reshape(1, -1), y, seq, alpha)
    return x2.reshape(bsz, seq, d)
```

```python
import functools
import math

import jax
import jax.numpy as jnp
import numpy as np
from jax import lax
from jax.experimental import pallas as pl
from jax.experimental.pallas import tpu as pltpu

F32 = jnp.float32
BF16 = jnp.bfloat16
I32 = jnp.int32

CHUNK = 64
A_HEADS = 8
A_LATENT = 128
IDX_HEADS = 8
IDX_DIM = 64
TOPK_KEYS_MAX = 256
REL_BUCKETS = 32
REL_MAX_DIST = 128
SSM_D_INNER = 1024
SSM_HEADDIM = 64
SSM_HEADS = SSM_D_INNER // SSM_HEADDIM
SSM_GROUPS = 4
SSM_STATE = 128
SSM_CONV = 4
SSM_CONV_DIM = SSM_D_INNER + 2 * SSM_GROUPS * SSM_STATE
N_EXPERTS = 32
TOPK_EXPERTS = 4
SWIGLU_LIMIT = 7.0
SWIGLU_ALPHA = 1.702
LN_EPS = 1e-5

LANES = 128
INT_MIN = -2147483648
VMEM_LIMIT = 56 * 1024 * 1024

HI = lax.Precision.HIGHEST


def _cparams(sem):
    return pltpu.CompilerParams(dimension_semantics=sem, vmem_limit_bytes=VMEM_LIMIT)


def _ln(x):
    mu = jnp.mean(x, axis=-1, keepdims=True)
    xc = x - mu
    var = jnp.mean(xc * xc, axis=-1, keepdims=True)
    return xc * lax.rsqrt(var + LN_EPS)


def _const_spec(shape):
    nd = len(shape)
    return pl.BlockSpec(shape, lambda *_: (0,) * nd, pipeline_mode=pl.Buffered(1))


def _mod_kernel(c_ref, w_ref, b_ref, o_ref):
    c = c_ref[...]
    sc = c * jax.nn.sigmoid(c)
    o_ref[...] = jnp.dot(sc, w_ref[...], precision=HI, preferred_element_type=F32) + b_ref[...]


def _mod_call(c, w_mod, b_mod):
    bsz, d = c.shape
    n = w_mod.shape[1]
    tn = 1024
    return pl.pallas_call(
        _mod_kernel,
        grid=(n // tn,),
        in_specs=[pl.BlockSpec((bsz, d), lambda j: (0, 0)),
                  pl.BlockSpec((d, tn), lambda j: (0, j)),
                  pl.BlockSpec((1, tn), lambda j: (0, j))],
        out_specs=pl.BlockSpec((bsz, tn), lambda j: (0, j)),
        out_shape=jax.ShapeDtypeStruct((bsz, n), F32),
        compiler_params=_cparams(("arbitrary",)),
        name="mod",
    )(c, w_mod, b_mod.reshape(1, n))


_C_Q, _C_KV, _C_IQ, _C_SM, _C_Z, _C_XBC, _C_GA, _C_GB, _C_END = 0, 1024, 1152, 1664, 1792, 2816, 4864, 5888, 6912
_SM_IW = IDX_DIM
_SM_DT = IDX_DIM + SSM_HEADS


def _inproj_kernel(x_ref, mod_ref, w_ref, kvw_ref, ikw_ref, ikb_ref,
                   q_ref, kv_ref, iq_ref, ik2_ref, sm_ref, smt_ref, z_ref, xbc_ref, ga_ref, gb_ref):
    u = _ln(x_ref[...]) * (1.0 + mod_ref[0, 1:2, :]) + mod_ref[0, 0:1, :]
    ub = u.astype(BF16)

    def mm(a, b):
        return jnp.dot(ub, w_ref[:, a:b], preferred_element_type=F32)

    q_ref[...] = mm(_C_Q, _C_KV).astype(BF16)
    kv = mm(_C_KV, _C_IQ)
    kv = kv * lax.rsqrt(jnp.mean(kv * kv, axis=-1, keepdims=True) + LN_EPS)
    kv_ref[...] = (kv * kvw_ref[...]).astype(BF16)
    iq_ref[...] = mm(_C_IQ, _C_SM).astype(BF16)
    g = mm(_C_SM, _C_Z)
    lane = lax.broadcasted_iota(I32, g.shape, 1)
    is_ik = lane < IDX_DIM
    mu = jnp.sum(jnp.where(is_ik, g, 0.0), axis=-1, keepdims=True) * (1.0 / IDX_DIM)
    gc = g - mu
    var = jnp.sum(jnp.where(is_ik, gc * gc, 0.0), axis=-1, keepdims=True) * (1.0 / IDX_DIM)
    ik = jnp.where(is_ik, gc * lax.rsqrt(var + LN_EPS) * ikw_ref[...] + ikb_ref[...], 0.0)
    ik2_ref[0] = ik.astype(BF16)
    ik2_ref[1] = pltpu.roll(ik, IDX_DIM, axis=1).astype(BF16)
    sm = jnp.where(lane < _SM_IW + IDX_HEADS, g * (IDX_HEADS ** -0.5), g)
    sm_ref[...] = sm
    smt_ref[0] = sm.T
    z_ref[...] = mm(_C_Z, _C_XBC)
    xbc_ref[...] = mm(_C_XBC, _C_GA)
    ga_ref[...] = mm(_C_GA, _C_GB)
    gb_ref[...] = mm(_C_GB, _C_END)


def _inproj_call(x2, mod3, w_perm, kvw, ikw, ikb, seq):
    n_tok, d = x2.shape
    tm = 256
    row = lambda i: (i, 0)

    def ospec(n):
        return pl.BlockSpec((tm, n), row)

    spt = seq // tm
    sd = jax.ShapeDtypeStruct
    return pl.pallas_call(
        _inproj_kernel,
        grid=(n_tok // tm,),
        in_specs=[pl.BlockSpec((tm, d), row),
                  pl.BlockSpec((1, 6, d), lambda i: (i // spt, 0, 0)),
                  _const_spec(w_perm.shape), _const_spec((1, 128)), _const_spec((1, 128)), _const_spec((1, 128))],
        out_specs=[ospec(1024), ospec(128), ospec(512),
                   pl.BlockSpec((2, tm, LANES), lambda i: (0, i, 0)),
                   ospec(LANES),
                   pl.BlockSpec((1, LANES, tm), lambda i: (i // spt, 0, i % spt)),
                   ospec(1024), ospec(2048), ospec(1024), ospec(1024)],
        out_shape=[sd((n_tok, 1024), BF16), sd((n_tok, 128), BF16), sd((n_tok, 512), BF16),
                   sd((2, n_tok, LANES), BF16), sd((n_tok, LANES), F32), sd((n_tok // seq, LANES, seq), F32),
                   sd((n_tok, 1024), F32), sd((n_tok, 2048), F32), sd((n_tok, 1024), F32), sd((n_tok, 1024), F32)],
        compiler_params=_cparams(("arbitrary",)),
        name="inproj",
    )(x2, mod3, w_perm, kvw, ikw, ikb)


_TQ = 256
_KB = 256
_BIAS_STEP = math.gcd(_TQ, _KB)
_NEAR_DELTAS = tuple(range(-_KB, 1, _BIAS_STEP))
assert REL_MAX_DIST <= _BIAS_STEP
_PBITS = 15
_GUARDS = -2147450880


def _t5_bucket(rel):
    half = REL_BUCKETS // 2
    max_exact = half // 2
    ret = (rel > 0).astype(jnp.int32) * half
    n = jnp.abs(rel)
    nf = jnp.maximum(n, 1).astype(jnp.float32)
    large = max_exact + (jnp.log(nf / max_exact) / math.log(REL_MAX_DIST / max_exact)
                         * (half - max_exact)).astype(jnp.int32)
    large = jnp.minimum(large, half - 1)
    return ret + jnp.where(n < max_exact, n, large)


def _bias_tiles(rel_bias):
    i = jnp.arange(_TQ, dtype=jnp.int32)[:, None]
    c = jnp.arange(_KB, dtype=jnp.int32)[None, :]
    rel = [c + delta - i for delta in _NEAR_DELTAS] + [jnp.full((_TQ, _KB), -REL_MAX_DIST, jnp.int32)]
    bucket = _t5_bucket(jnp.stack(rel))
    b = jnp.zeros((len(rel), A_HEADS, _TQ, _KB), F32)
    for k in range(REL_BUCKETS):
        b = jnp.where((bucket == k)[:, None], rel_bias[k].astype(F32)[None, :, None, None], b)
    return (b * (A_LATENT ** 0.5)).reshape(len(rel), A_HEADS * _TQ, _KB)


def _attn_kernel(q_ref, iq_ref, iwt_ref, ik2_ref, kv_ref, bias_ref, o_ref,
                 keys_ref, pk_ref, s_ref, mx_ref, l_ref, acc_ref, *, n_sel, jbits):
    tq, kb = _TQ, _KB
    qs = pl.program_id(1) * tq
    nkb = (qs + tq + kb - 1) // kb
    nt = (((1,), (1,)), ((), ()))

    qpos = lax.broadcasted_iota(I32, (1, tq), 1) + qs
    qend = (qpos // CHUNK + 1) * CHUNK
    krow = lax.broadcasted_iota(I32, (kb, tq), 0)
    iq = iq_ref[...]
    iwt = iwt_ref[0] * (IDX_DIM ** -0.5)

    def score_body(j, carry):
        acc = jnp.zeros((kb, tq), F32)
        for p in range(IDX_HEADS // 2):
            pair = iq[:, p * LANES:(p + 1) * LANES]
            for par in range(2):
                h = 2 * p + par
                s = lax.dot_general(ik2_ref[par, 0, j], pair, nt, preferred_element_type=F32)
                acc = acc + jnp.maximum(s, 0.0) * iwt[h:h + 1, :]
        bits = pltpu.bitcast(acc, I32)
        key = jnp.where(bits < 0, bits ^ 0x7FFFFFFF, bits)
        key = jnp.where(bits == INT_MIN, 0, key)
        key = jnp.where(krow + j * kb < qend, key, INT_MIN)
        keys_ref[j] = key
        top = (key >> (32 - _PBITS)) + (1 << (_PBITS - 1))
        pk_ref[j] = (top[:kb // 2] << 16) | top[kb // 2:] | _GUARDS
        return carry

    lax.fori_loop(0, nkb, score_body, 0)

    def count_top(cand):
        both = (cand << 16) | cand

        def body(j, acc):
            g = ((pk_ref[j] - both) >> _PBITS) & 0x00010001
            parts = [g[8 * i:8 * i + 8] for i in range(kb // 16)]
            while len(parts) > 1:
                parts = [parts[i] + parts[i + 1] for i in range(0, len(parts), 2)]
            return acc + parts[0]
        acc = lax.fori_loop(0, nkb, body, jnp.zeros((8, tq), I32))
        return jnp.sum(((acc & 0xFFFF) + (acc >> 16)).astype(F32), axis=0, keepdims=True)

    def top_body(it, tu):
        cu = tu | (jnp.int32(1) << (_PBITS - 1 - it))
        return jnp.where(count_top(cu) >= n_sel, cu, tu)

    top_bits = lax.fori_loop(0, _PBITS, top_body, jnp.zeros((1, tq), I32))

    def count(pred):
        def body(j, acc):
            m = jnp.where(pred(keys_ref[j], krow + j * kb), 1.0, 0.0)
            parts = [m[8 * i:8 * i + 8] for i in range(kb // 8)]
            while len(parts) > 1:
                parts = [parts[i] + parts[i + 1] for i in range(0, len(parts), 2)]
            return acc + parts[0]
        acc = lax.fori_loop(0, nkb, body, jnp.zeros((8, tq), F32))
        return jnp.sum(acc, axis=0, keepdims=True)

    def bit_body(it, tu):
        cu = tu | (jnp.int32(1) << (31 - it))
        cs = cu ^ INT_MIN
        cnt = count(lambda k, kidx: k >= cs)
        return jnp.where(cnt >= n_sel, cu, tu)

    thr = lax.fori_loop(_PBITS, 32, bit_body, top_bits << (32 - _PBITS)) ^ INT_MIN
    n_ge = count(lambda k, kidx: k >= thr)

    def tie_search():
        need = n_sel - count(lambda k, kidx: k > thr)

        def j_body(it, jj):
            cj = jj | (jnp.int32(1) << (jbits - 1 - it))
            f = count(lambda k, kidx: (k == thr) & (kidx < cj))
            return jnp.where(f <= need, cj, jj)

        return lax.fori_loop(0, jbits, j_body, jnp.zeros((1, tq), I32))

    jj = lax.cond(jnp.max(n_ge) > n_sel, tie_search, lambda: jnp.full((1, tq), (1 << jbits) - 1, I32))

    q = q_ref[...]
    q_all = jnp.concatenate([q[:, h * LANES:(h + 1) * LANES] for h in range(A_HEADS)], axis=0)
    eye = jnp.where(lax.broadcasted_iota(I32, (tq, tq), 0) == lax.broadcasted_iota(I32, (tq, tq), 1),
                    1.0, 0.0).astype(BF16)
    mx_ref[...] = jnp.full(mx_ref.shape, -jnp.inf, F32)
    exp2_scale = A_LATENT ** -0.5 * math.log2(math.e)

    def logit_body(j, carry):
        k = keys_ref[j]
        kidx = krow + j * kb
        selt = ((k > thr) | ((k == thr) & (kidx < jj))) & (kidx < qend)
        sel = lax.dot_general(eye, jnp.where(selt, 1.0, 0.0).astype(BF16), nt, preferred_element_type=F32) > 0.5
        s = lax.dot_general(q_all, kv_ref[0, j], nt, preferred_element_type=F32)
        v = (j * kb - qs + kb) // _BIAS_STEP
        s = s + bias_ref[jnp.where(v < 0, len(_NEAR_DELTAS), v)]
        s = jnp.concatenate([jnp.where(sel, s[h * tq:(h + 1) * tq], -jnp.inf) for h in range(A_HEADS)], axis=0)
        s_ref[j] = s
        mx = mx_ref[...]
        for c in range(kb // LANES):
            mx = jnp.maximum(mx, s[:, c * LANES:(c + 1) * LANES])
        mx_ref[...] = mx
        return carry

    lax.fori_loop(0, nkb, logit_body, 0)
    m = jnp.max(mx_ref[...], axis=1, keepdims=True)
    m = jnp.where(m == -jnp.inf, 0.0, m)
    mx_ref[...] = jnp.broadcast_to(m, mx_ref.shape)
    l_ref[...] = jnp.zeros(l_ref.shape, F32)
    acc_ref[...] = jnp.zeros(acc_ref.shape, F32)

    def pv_body(j, carry):
        mb = mx_ref[...]
        s = s_ref[j]
        ps = [jnp.exp2((s[:, c * LANES:(c + 1) * LANES] - mb) * exp2_scale) for c in range(kb // LANES)]
        lsum = l_ref[...]
        for pc in ps:
            lsum = lsum + pc
        l_ref[...] = lsum
        p = jnp.concatenate(ps, axis=1).astype(BF16)
        acc_ref[...] += jnp.dot(p, kv_ref[0, j], preferred_element_type=F32)
        return carry

    lax.fori_loop(0, nkb, pv_body, 0)
    o = acc_ref[...] / jnp.sum(l_ref[...], axis=1, keepdims=True)
    for h in range(A_HEADS):
        o_ref[:, h * LANES:(h + 1) * LANES] = o[h * tq:(h + 1) * tq].astype(o_ref.dtype)


def _attn_call(q, iq, iwt, ik2, kvb, bias, bsz, seq):
    tq, kb = _TQ, _KB
    nq = seq // tq
    nblk = seq // kb
    n_sel = min(TOPK_KEYS_MAX, seq // 4)
    jbits = int(seq).bit_length()
    row = lambda b, i: (b * nq + i, 0)
    kern = functools.partial(_attn_kernel, n_sel=float(n_sel), jbits=jbits)
    return pl.pallas_call(
        kern,
        grid=(bsz, nq),
        in_specs=[pl.BlockSpec((tq, A_HEADS * A_LATENT), row),
                  pl.BlockSpec((tq, IDX_HEADS * IDX_DIM), row),
                  pl.BlockSpec((1, IDX_HEADS, tq), lambda b, i: (b, _SM_IW // IDX_HEADS, i)),
                  pl.BlockSpec((2, 1, nblk, kb, LANES), lambda b, i: (0, b, 0, 0, 0)),
                  pl.BlockSpec((1, nblk, kb, A_LATENT), lambda b, i: (b, 0, 0, 0)),
                  _const_spec(bias.shape)],
        out_specs=pl.BlockSpec((tq, A_HEADS * A_LATENT), row),
        out_shape=jax.ShapeDtypeStruct((bsz * seq, A_HEADS * A_LATENT), BF16),
        scratch_shapes=[pltpu.VMEM((nblk, kb, tq), I32),
                        pltpu.VMEM((nblk, kb // 2, tq), I32),
                        pltpu.VMEM((nblk, A_HEADS * tq, kb), F32),
                        pltpu.VMEM((A_HEADS * tq, LANES), F32),
                        pltpu.VMEM((A_HEADS * tq, LANES), F32),
                        pltpu.VMEM((A_HEADS * tq, A_LATENT), F32)],
        compiler_params=_cparams(("arbitrary", "arbitrary")),
        name="attention",
    )(q, iq, iwt, ik2, kvb, bias)


_SSD_L = 256
_PAIRS = SSM_HEADS // 2


def _ssd_kernel(z_ref, xbc_ref, sm_ref, dtt_ref, cw_ref, cb_ref, dtb_ref, dtbt_ref, al_ref, alt_ref,
                dsk_ref, nw_ref, o_ref, ext_ref, state_ref, y_ref):
    L = _SSD_L
    hd = SSM_HEADDIM

    @pl.when(pl.program_id(1) == 0)
    def _():
        ext_ref[0:8, :] = jnp.zeros((8, SSM_CONV_DIM), F32)
        state_ref[...] = jnp.zeros(state_ref.shape, F32)

    x = xbc_ref[...]
    ext_ref[8:8 + L, :] = x
    w = cw_ref[...]
    conv = x * w[3:4] + cb_ref[...]
    for k in range(1, SSM_CONV):
        conv = conv + ext_ref[8 - k:8 - k + L, :] * w[SSM_CONV - 1 - k:SSM_CONV - k]
    ext_ref[0:8, :] = x[L - 8:L]
    act = conv * jax.nn.sigmoid(conv)
    xs = act[:, :SSM_D_INNER]
    boff = SSM_D_INNER
    coff = SSM_D_INNER + SSM_GROUPS * SSM_STATE

    def softplus(v):
        return jnp.maximum(v, 0.0) + jnp.log1p(jnp.exp(-jnp.abs(v)))

    dt = softplus(sm_ref[:, _SM_DT:_SM_DT + SSM_HEADS] + dtb_ref[...])
    dtt = softplus(dtt_ref[0] + dtbt_ref[...])
    a_col = dt * (-jnp.exp(al_ref[...]))
    a_row = dtt * (-jnp.exp(alt_ref[...]))
    ri = lax.broadcasted_iota(I32, (L, L), 0)
    ci = lax.broadcasted_iota(I32, (L, L), 1)
    causal = ci <= ri
    cs_col = jnp.dot(jnp.where(causal, 1.0, 0.0), a_col, precision=HI, preferred_element_type=F32)
    cs_row = jnp.dot(a_row, jnp.where(ri <= ci, 1.0, 0.0), precision=HI, preferred_element_type=F32)
    cs_last = cs_col[L - 1:L, :]
    lane = lax.broadcasted_iota(I32, (1, LANES), 1)
    lo = lane < hd
    sub = lax.broadcasted_iota(I32, (LANES, 1), 0)

    for g in range(SSM_GROUPS):
        bm = act[:, boff + g * SSM_STATE: boff + (g + 1) * SSM_STATE].astype(BF16)
        cm = act[:, coff + g * SSM_STATE: coff + (g + 1) * SSM_STATE].astype(BF16)
        cb = lax.dot_general(cm, bm, (((1,), (1,)), ((), ())), preferred_element_type=F32)
        for pp in range(_PAIRS // SSM_GROUPS):
            p = g * (_PAIRS // SSM_GROUPS) + pp
            h0, h1 = 2 * p, 2 * p + 1
            xp = xs[:, p * LANES:(p + 1) * LANES]
            dtl = jnp.where(lo, dt[:, h0:h0 + 1], dt[:, h1:h1 + 1])
            xdt = xp * dtl
            csl = jnp.where(lo, cs_col[:, h0:h0 + 1], cs_col[:, h1:h1 + 1])
            last = jnp.where(lo, cs_last[:, h0:h0 + 1], cs_last[:, h1:h1 + 1])
            ydiag = jnp.zeros((L, LANES), F32)
            for h, msk in ((h0, lo), (h1, jnp.logical_not(lo))):
                seg = cs_col[:, h:h + 1] - cs_row[h:h + 1, :]
                gm = (cb * jnp.exp(jnp.where(causal, seg, -jnp.inf))).astype(BF16)
                ydiag = ydiag + jnp.dot(gm, jnp.where(msk, xdt, 0.0).astype(BF16), preferred_element_type=F32)
            prev = state_ref[p]
            yoff = lax.dot_general(cm, prev.astype(BF16), (((1,), (1,)), ((), ())), preferred_element_type=F32)
            y_ref[:, p * LANES:(p + 1) * LANES] = ydiag + yoff * jnp.exp(csl) + xp * dsk_ref[:, p * LANES:(p + 1) * LANES]
            wx = (xdt * jnp.exp(last - csl)).astype(BF16)
            st = lax.dot_general(wx, bm, (((0,), (0,)), ((), ())), preferred_element_type=F32)
            cdec = jnp.where(sub < hd, jnp.exp(cs_last[:, h0:h0 + 1]), jnp.exp(cs_last[:, h1:h1 + 1]))
            state_ref[p] = prev * cdec + st

    z = z_ref[...]
    y = y_ref[...] * (z * jax.nn.sigmoid(z))
    gw = SSM_D_INNER // SSM_GROUPS
    for g in range(SSM_GROUPS):
        yg = y[:, g * gw:(g + 1) * gw]
        yg = yg * lax.rsqrt(jnp.mean(yg * yg, axis=-1, keepdims=True) + LN_EPS)
        o_ref[:, g * gw:(g + 1) * gw] = (yg * nw_ref[:, g * gw:(g + 1) * gw]).astype(o_ref.dtype)


def _ssd_call(z, xbc, small, smt, conv_w, conv_b, dt_bias, a_log, d_skip, norm_w, bsz, seq):
    L = _SSD_L
    nc = seq // L
    row = lambda b, c: (b * nc + c, 0)
    h = SSM_HEADS
    return pl.pallas_call(
        _ssd_kernel,
        grid=(bsz, nc),
        in_specs=[pl.BlockSpec((L, SSM_D_INNER), row),
                  pl.BlockSpec((L, SSM_CONV_DIM), row),
                  pl.BlockSpec((L, LANES), row),
                  pl.BlockSpec((1, h, L), lambda b, c: (b, _SM_DT // h, c)),
                  _const_spec((SSM_CONV, SSM_CONV_DIM)), _const_spec((1, SSM_CONV_DIM)),
                  _const_spec((1, h)), _const_spec((h, 1)), _const_spec((1, h)), _const_spec((h, 1)),
                  _const_spec((1, SSM_D_INNER)), _const_spec((1, SSM_D_INNER))],
        out_specs=pl.BlockSpec((L, SSM_D_INNER), row),
        out_shape=jax.ShapeDtypeStruct((bsz * seq, SSM_D_INNER), BF16),
        scratch_shapes=[pltpu.VMEM((L + 8, SSM_CONV_DIM), F32),
                        pltpu.VMEM((_PAIRS, LANES, SSM_STATE), F32),
                        pltpu.VMEM((L, SSM_D_INNER), F32)],
        compiler_params=_cparams(("arbitrary", "arbitrary")),
        name="ssd",
    )(z, xbc, small, smt, conv_w, conv_b.reshape(1, -1), dt_bias.reshape(1, h), dt_bias.reshape(h, 1),
      a_log.reshape(1, h), a_log.reshape(h, 1), jnp.repeat(d_skip, SSM_HEADDIM).reshape(1, -1),
      norm_w.reshape(1, -1))


_TM = 512


def _merge_kernel(oa_ref, ob_ref, ga_ref, gb_ref, x_ref, mod_ref, wpa_ref, wpb_ref, wo_ref, g1_ref, b1_ref,
                  wrh_ref, wrl_ref, br_ref, x1_ref, u2_ref, route_ref, gate_ref, cnt_ref, base_ref, *, alpha):
    tm = _TM
    ne = N_EXPERTS

    @pl.when(pl.program_id(0) == 0)
    def _():
        base_ref[...] = jnp.zeros(base_ref.shape, F32)

    ma = jnp.dot(oa_ref[...], wpa_ref[...], preferred_element_type=F32)
    mb = jnp.dot(ob_ref[...], wpb_ref[...], preferred_element_type=F32)
    merged = jax.nn.sigmoid(ga_ref[...]) * ma + jax.nn.sigmoid(gb_ref[...]) * mb
    t = jnp.dot(merged.astype(BF16), wo_ref[...], preferred_element_type=F32)
    x1 = _ln(alpha * x_ref[...] + mod_ref[0, 2:3, :] * t) * g1_ref[...] + b1_ref[...]
    x1_ref[...] = x1
    u2 = _ln(x1) * (1.0 + mod_ref[0, 4:5, :]) + mod_ref[0, 3:4, :]
    u2_ref[...] = u2
    nt = (((1,), (1,)), ((), ()))
    uh = u2.astype(BF16)
    ul = (u2 - uh.astype(F32)).astype(BF16)
    wh, wl = wrh_ref[...], wrl_ref[...]
    logits = (lax.dot_general(wh, uh, nt, preferred_element_type=F32)
              + lax.dot_general(wl, uh, nt, preferred_element_type=F32)
              + lax.dot_general(wh, ul, nt, preferred_element_type=F32)) + br_ref[...]
    eio = lax.broadcasted_iota(I32, (ne, tm), 0).astype(F32)
    vals, ids = [], []
    for _ in range(TOPK_EXPERTS):
        m = jnp.max(logits, axis=0, keepdims=True)
        idx = jnp.min(jnp.where(logits == m, eio, float(ne)), axis=0, keepdims=True)
        vals.append(m)
        ids.append(idx)
        logits = jnp.where(eio == idx, -jnp.inf, logits)
    es = [jnp.exp(v - vals[0]) for v in vals]
    den = es[0] + es[1] + es[2] + es[3]

    onehot = jnp.zeros((ne, tm), F32)
    for idx in ids:
        onehot = onehot + jnp.where(eio == idx, 1.0, 0.0)
    ri = lax.broadcasted_iota(I32, (tm, tm), 0)
    ci = lax.broadcasted_iota(I32, (tm, tm), 1)
    before = jnp.where(ri < ci, 1.0, 0.0).astype(BF16)
    base = base_ref[...]
    prefix = jnp.dot(onehot.astype(BF16), before, preferred_element_type=F32) + base
    sub = lax.broadcasted_iota(I32, (8, tm), 0)
    route = jnp.zeros((8, tm), F32)
    gates = jnp.zeros((8, tm), F32)
    for j in range(TOPK_EXPERTS):
        rank = jnp.sum(jnp.where(eio == ids[j], prefix, 0.0), axis=0, keepdims=True)
        route = jnp.where(sub == j, ids[j], route)
        route = jnp.where(sub == TOPK_EXPERTS + j, rank, route)
        gates = jnp.where(sub == j, es[j] / den, gates)
    route_ref[...] = route.astype(I32)
    gate_ref[...] = gates
    base = base + jnp.sum(onehot, axis=1, keepdims=True)
    base_ref[...] = base
    cnt_ref[...] = jnp.broadcast_to(base, cnt_ref.shape)


def _merge_call(o_a, o_b, g_a, g_b, x2, mod3, wpa, wpb, wo, ln_g, ln_b, w_router, b_router, seq, alpha):
    n_tok, d = x2.shape
    tm = _TM
    ne = N_EXPERTS
    row = lambda i: (i, 0)
    blk = pl.BlockSpec((tm, d), row)
    sm = pl.BlockSpec((8, tm), lambda i: (0, i))
    wrt = w_router.T
    wrh = wrt.astype(BF16)
    wrl = (wrt - wrh.astype(F32)).astype(BF16)
    return pl.pallas_call(
        functools.partial(_merge_kernel, alpha=alpha),
        grid=(n_tok // tm,),
        in_specs=[blk, blk, blk, blk, blk,
                  pl.BlockSpec((1, 6, d), lambda i: ((i * tm) // seq, 0, 0)),
                  _const_spec((d, d)), _const_spec((d, d)), _const_spec((d, d)),
                  _const_spec((1, d)), _const_spec((1, d)), _const_spec((ne, d)), _const_spec((ne, d)),
                  _const_spec((ne, 1))],
        out_specs=[blk, blk, sm, sm, pl.BlockSpec((ne, LANES), lambda i: (0, 0))],
        out_shape=[jax.ShapeDtypeStruct((n_tok, d), F32), jax.ShapeDtypeStruct((n_tok, d), F32),
                   jax.ShapeDtypeStruct((8, n_tok), I32), jax.ShapeDtypeStruct((8, n_tok), F32),
                   jax.ShapeDtypeStruct((ne, LANES), F32)],
        scratch_shapes=[pltpu.VMEM((ne, 1), F32)],
        compiler_params=_cparams(("arbitrary",)),
        name="merge",
    )(o_a, o_b, g_a, g_b, x2, mod3, wpa, wpb, wo, ln_g, ln_b, wrh, wrl, b_router.reshape(ne, 1))


_TD = 512
_TMB = 512


def _dispatch_kernel(pend_ref, dest_ref, u2_ref, xs_ref, zero_ref, sem):
    @pl.when(pl.program_id(0) == 0)
    def _():
        zero_ref[...] = jnp.zeros(zero_ref.shape, F32)
        for e in range(N_EXPERTS):
            end = pend_ref[e]
            start = pend_ref[e - 1] if e else 0

            @pl.when(end > start)
            def _():
                dst = xs_ref.at[pl.ds(pl.multiple_of(end - _TMB, _TMB), _TMB), :]
                cp = pltpu.make_async_copy(zero_ref, dst, sem)
                cp.start()
                cp.wait()

    def issue(t, carry):
        for j in range(TOPK_EXPERTS):
            d = dest_ref[t * TOPK_EXPERTS + j]
            pltpu.make_async_copy(u2_ref.at[pl.ds(t, 1), :], xs_ref.at[pl.ds(d, 1), :], sem).start()
        return carry

    lax.fori_loop(0, _TD, issue, 0)
    for _ in range(TOPK_EXPERTS):
        pltpu.make_async_copy(u2_ref, xs_ref.at[pl.ds(0, _TD), :], sem).wait()


def _dispatch_call(pends, dest_flat, u2, n_slots):
    n_tok, d = u2.shape
    grid_spec = pltpu.PrefetchScalarGridSpec(
        num_scalar_prefetch=1,
        grid=(n_tok // _TD,),
        in_specs=[pl.BlockSpec((_TD * TOPK_EXPERTS,), lambda i, pe: (i,), memory_space=pltpu.SMEM),
                  pl.BlockSpec((_TD, d), lambda i, pe: (i, 0))],
        out_specs=pl.BlockSpec(memory_space=pl.ANY),
        scratch_shapes=[pltpu.VMEM((_TMB, d), F32), pltpu.SemaphoreType.DMA(())],
    )
    return pl.pallas_call(
        _dispatch_kernel,
        grid_spec=grid_spec,
        out_shape=jax.ShapeDtypeStruct((n_slots, d), F32),
        compiler_params=_cparams(("arbitrary",)),
        name="dispatch",
    )(pends, dest_flat, u2)


def _expert_kernel(be_ref, nu_ref, xs_ref, w1_ref, b1_ref, w2_ref, b2_ref, y_ref, w1b_ref, w2b_ref):
    i = pl.program_id(0)
    f = w2_ref.shape[1]
    used = i < nu_ref[0]
    new_expert = jnp.logical_or(i == 0, be_ref[i] != be_ref[jnp.maximum(i - 1, 0)])

    @pl.when(jnp.logical_and(used, new_expert))
    def _():
        w1b_ref[...] = w1_ref[0].astype(BF16)
        w2b_ref[...] = w2_ref[0].astype(BF16)

    @pl.when(used)
    def _():
        h = jnp.dot(xs_ref[...].astype(BF16), w1b_ref[...], preferred_element_type=F32) + b1_ref[0]
        gate = jnp.minimum(h[:, :f], SWIGLU_LIMIT)
        up = jnp.clip(h[:, f:], -SWIGLU_LIMIT, SWIGLU_LIMIT)
        act = (up + 1.0) * gate * jax.nn.sigmoid(SWIGLU_ALPHA * gate)
        y_ref[...] = jnp.dot(act.astype(BF16), w2b_ref[...], preferred_element_type=F32) + b2_ref[0]

    @pl.when(i >= nu_ref[0])
    def _():
        y_ref[...] = jnp.zeros(y_ref.shape, F32)


def _expert_call(block_expert, n_used, xs, w1, b1, w2, b2):
    n_slots, d = xs.shape
    ne, _, f2 = w1.shape
    f = f2 // 2
    grid_spec = pltpu.PrefetchScalarGridSpec(
        num_scalar_prefetch=2,
        grid=(n_slots // _TMB,),
        in_specs=[pl.BlockSpec((_TMB, d), lambda i, be, nu: (i, 0)),
                  pl.BlockSpec((1, d, f2), lambda i, be, nu: (be[i], 0, 0)),
                  pl.BlockSpec((1, 1, f2), lambda i, be, nu: (be[i], 0, 0)),
                  pl.BlockSpec((1, f, d), lambda i, be, nu: (be[i], 0, 0)),
                  pl.BlockSpec((1, 1, d), lambda i, be, nu: (be[i], 0, 0))],
        out_specs=pl.BlockSpec((_TMB, d), lambda i, be, nu: (i, 0)),
        scratch_shapes=[pltpu.VMEM((d, f2), BF16), pltpu.VMEM((f, d), BF16)],
    )
    return pl.pallas_call(
        _expert_kernel,
        grid_spec=grid_spec,
        out_shape=jax.ShapeDtypeStruct((n_slots, d), F32),
        compiler_params=_cparams(("arbitrary",)),
        name="experts",
    )(block_expert, n_used, xs, w1, b1.reshape(ne, 1, f2), w2, b2.reshape(ne, 1, d))


_TC = 512


def _combine_kernel(dest_ref, gate_ref, x1_ref, mod_ref, g2_ref, b2_ref, y_hbm, o_ref, buf_ref, sem, *, alpha):
    def issue(t, carry):
        for j in range(TOPK_EXPERTS):
            d = dest_ref[t * TOPK_EXPERTS + j]
            pltpu.make_async_copy(y_hbm.at[pl.ds(d, 1), :], buf_ref.at[j, pl.ds(t, 1), :], sem).start()
        return carry

    lax.fori_loop(0, _TC, issue, 0, unroll=2)
    for j in range(TOPK_EXPERTS):
        pltpu.make_async_copy(y_hbm.at[pl.ds(0, _TC), :], buf_ref.at[j], sem).wait()
    gates = gate_ref[...]
    y = gates[:, 0:1] * buf_ref[0]
    for j in range(1, TOPK_EXPERTS):
        y = y + gates[:, j:j + 1] * buf_ref[j]
    o_ref[...] = _ln(alpha * x1_ref[...] + mod_ref[0, 5:6, :] * y) * g2_ref[...] + b2_ref[...]


def _combine_call(dest_flat, gates, x1, mod3, ln_g, ln_b, y, seq, alpha):
    n_tok, d = x1.shape
    tc = _TC
    row = lambda i: (i, 0)
    return pl.pallas_call(
        functools.partial(_combine_kernel, alpha=alpha),
        grid=(n_tok // tc,),
        in_specs=[pl.BlockSpec((tc * TOPK_EXPERTS,), lambda i: (i,), memory_space=pltpu.SMEM),
                  pl.BlockSpec((tc, TOPK_EXPERTS), row),
                  pl.BlockSpec((tc, d), row),
                  pl.BlockSpec((1, 6, d), lambda i: ((i * tc) // seq, 0, 0)),
                  _const_spec((1, d)), _const_spec((1, d)),
                  pl.BlockSpec(memory_space=pl.ANY)],
        out_specs=pl.BlockSpec((tc, d), row),
        out_shape=jax.ShapeDtypeStruct((n_tok, d), F32),
        scratch_shapes=[pltpu.VMEM((TOPK_EXPERTS, tc, d), F32), pltpu.SemaphoreType.DMA(())],
        compiler_params=_cparams(("arbitrary",)),
        name="combine",
    )(dest_flat, gates, x1, mod3, ln_g, ln_b, y)


def _permute_w_in(w):
    d = w.shape[0]
    s = np.cumsum([0, A_HEADS * A_LATENT, A_LATENT, IDX_HEADS * IDX_DIM, IDX_DIM, IDX_HEADS,
                   SSM_D_INNER, SSM_CONV_DIM, SSM_HEADS, d, d]).tolist()
    q, kv, iq, ik, iw, z, xbc, dt, ga, gb = [w[:, s[i]:s[i + 1]] for i in range(10)]
    pad1 = jnp.zeros((d, _SM_DT - _SM_IW - IDX_HEADS), w.dtype)
    pad2 = jnp.zeros((d, LANES - _SM_DT - SSM_HEADS), w.dtype)
    return jnp.concatenate([q, kv, iq, ik, iw, pad1, dt, pad2, z, xbc, ga, gb], axis=1).astype(BF16)


def _pad_lanes(v, fill=0.0):
    return jnp.pad(v.reshape(1, -1), ((0, 0), (0, LANES - v.shape[-1])), constant_values=fill)


def kernel(x, c, w_mod, b_mod, w_in, kv_norm_w, idx_k_norm_w, idx_k_norm_b, rel_bias, conv_w, conv_b, dt_bias,
           a_log, d_skip, ssm_norm_w, w_proj_a, w_proj_b, w_out, ln1_g, ln1_b, w_router, b_router, w1, b1, w2, b2,
           ln2_g, ln2_b):
    bsz, seq, d = x.shape
    depth = w_mod.shape[0]
    alpha = (2.0 * depth) ** 0.25
    n_tok = bsz * seq
    n_asg = n_tok * TOPK_EXPERTS
    n_blocks = n_asg // _TMB + N_EXPERTS
    n_slots = n_blocks * _TMB
    nblk = seq // _KB
    bias = _bias_tiles(rel_bias)
    x2 = x.reshape(n_tok, d)
    for l in range(depth):
        mod3 = _mod_call(c, w_mod[l], b_mod[l]).reshape(bsz, 6, d)
        q, kvn, iq, ik2, small, smt, z, xbc, g_a, g_b = _inproj_call(
            x2, mod3, _permute_w_in(w_in[l]), kv_norm_w[l].reshape(1, -1),
            _pad_lanes(idx_k_norm_w[l]), _pad_lanes(idx_k_norm_b[l]), seq)
        kvb = kvn.reshape(bsz, nblk, _KB, A_LATENT)
        o_a = _attn_call(q, iq, smt, ik2.reshape(2, bsz, nblk, _KB, LANES), kvb, bias, bsz, seq)
        o_b = _ssd_call(z, xbc, small, smt, conv_w[l], conv_b[l], dt_bias[l], a_log[l], d_skip[l], ssm_norm_w[l],
                        bsz, seq)
        x1, u2, route, gates, cnt = _merge_call(
            o_a, o_b, g_a, g_b, x2, mod3, w_proj_a[l].astype(BF16), w_proj_b[l].astype(BF16),
            w_out[l].astype(BF16), ln1_g[l].reshape(1, -1), ln1_b[l].reshape(1, -1), w_router[l], b_router[l],
            seq, alpha)
        counts = cnt[:, 0].astype(I32)
        padded = (counts + _TMB - 1) // _TMB * _TMB
        pends = jnp.cumsum(padded).astype(I32)
        pstarts = pends - padded
        eid = route[:TOPK_EXPERTS]
        onehot = eid[:, :, None] == jnp.arange(N_EXPERTS, dtype=I32)
        dest = jnp.sum(jnp.where(onehot, pstarts, 0), axis=-1) + route[TOPK_EXPERTS:2 * TOPK_EXPERTS]
        dest_flat = dest.T.reshape(n_asg)
        block_start = jnp.arange(n_blocks, dtype=I32) * _TMB
        block_expert = jnp.minimum(jnp.sum(block_start[:, None] >= pends[None, :], axis=1), N_EXPERTS - 1).astype(I32)
        n_used = (pends[-1:] // _TMB).astype(I32)
        xs = _dispatch_call(pends, dest_flat, u2, n_slots)
        y = _expert_call(block_expert, n_used, xs, w1[l], b1[l], w2[l], b2[l])
        x2 = _combine_call(dest_flat, gates[:TOPK_EXPERTS].T, x1, mod3, ln2_g[l].reshape(1, -1),
                           ln2_b[l].reshape(1, -1), y, seq, alpha)
    return x2.reshape(bsz, seq, d)
```

```python
import functools
import math

import jax
import jax.numpy as jnp
import numpy as np
from jax import lax
from jax.experimental import pallas as pl
from jax.experimental.pallas import tpu as pltpu

F32 = jnp.float32
BF16 = jnp.bfloat16
I32 = jnp.int32

CHUNK = 64
A_HEADS = 8
A_LATENT = 128
IDX_HEADS = 8
IDX_DIM = 64
TOPK_KEYS_MAX = 256
REL_BUCKETS = 32
REL_MAX_DIST = 128
SSM_D_INNER = 1024
SSM_HEADDIM = 64
SSM_HEADS = SSM_D_INNER // SSM_HEADDIM
SSM_GROUPS = 4
SSM_STATE = 128
SSM_CONV = 4
SSM_CONV_DIM = SSM_D_INNER + 2 * SSM_GROUPS * SSM_STATE
N_EXPERTS = 32
TOPK_EXPERTS = 4
SWIGLU_LIMIT = 7.0
SWIGLU_ALPHA = 1.702
LN_EPS = 1e-5

LANES = 128
INT_MIN = -2147483648
VMEM_LIMIT = 56 * 1024 * 1024

HI = lax.Precision.HIGHEST


def _cparams(sem):
    return pltpu.CompilerParams(dimension_semantics=sem, vmem_limit_bytes=VMEM_LIMIT)


def _ln(x):
    mu = jnp.mean(x, axis=-1, keepdims=True)
    xc = x - mu
    var = jnp.mean(xc * xc, axis=-1, keepdims=True)
    return xc * lax.rsqrt(var + LN_EPS)


def _const_spec(shape):
    nd = len(shape)
    return pl.BlockSpec(shape, lambda *_: (0,) * nd, pipeline_mode=pl.Buffered(1))


def _mod_kernel(c_ref, w_ref, b_ref, o_ref):
    c = c_ref[...]
    sc = c * jax.nn.sigmoid(c)
    o_ref[...] = jnp.dot(sc, w_ref[...], precision=HI, preferred_element_type=F32) + b_ref[...]


def _mod_call(c, w_mod, b_mod):
    bsz, d = c.shape
    n = w_mod.shape[1]
    tn = 1024
    return pl.pallas_call(
        _mod_kernel,
        grid=(n // tn,),
        in_specs=[pl.BlockSpec((bsz, d), lambda j: (0, 0)),
                  pl.BlockSpec((d, tn), lambda j: (0, j)),
                  pl.BlockSpec((1, tn), lambda j: (0, j))],
        out_specs=pl.BlockSpec((bsz, tn), lambda j: (0, j)),
        out_shape=jax.ShapeDtypeStruct((bsz, n), F32),
        compiler_params=_cparams(("arbitrary",)),
        name="mod",
    )(c, w_mod, b_mod.reshape(1, n))


_C_Q, _C_KV, _C_IQ, _C_SM, _C_Z, _C_XBC, _C_GA, _C_GB, _C_END = 0, 1024, 1152, 1664, 1792, 2816, 4864, 5888, 6912
_SM_IW = IDX_DIM
_SM_DT = IDX_DIM + SSM_HEADS


def _inproj_kernel(x_ref, mod_ref, w_ref, kvw_ref, ikw_ref, ikb_ref,
                   q_ref, kv_ref, iq_ref, ik2_ref, sm_ref, smt_ref, z_ref, xbc_ref, ga_ref, gb_ref):
    u = _ln(x_ref[...]) * (1.0 + mod_ref[0, 1:2, :]) + mod_ref[0, 0:1, :]
    ub = u.astype(BF16)

    def mm(a, b):
        return jnp.dot(ub, w_ref[:, a:b], preferred_element_type=F32)

    q_ref[...] = mm(_C_Q, _C_KV).astype(BF16)
    kv = mm(_C_KV, _C_IQ)
    kv = kv * lax.rsqrt(jnp.mean(kv * kv, axis=-1, keepdims=True) + LN_EPS)
    kv_ref[...] = (kv * kvw_ref[...]).astype(BF16)
    iq_ref[...] = mm(_C_IQ, _C_SM).astype(BF16)
    g = mm(_C_SM, _C_Z)
    lane = lax.broadcasted_iota(I32, g.shape, 1)
    is_ik = lane < IDX_DIM
    mu = jnp.sum(jnp.where(is_ik, g, 0.0), axis=-1, keepdims=True) * (1.0 / IDX_DIM)
    gc = g - mu
    var = jnp.sum(jnp.where(is_ik, gc * gc, 0.0), axis=-1, keepdims=True) * (1.0 / IDX_DIM)
    ik = jnp.where(is_ik, gc * lax.rsqrt(var + LN_EPS) * ikw_ref[...] + ikb_ref[...], 0.0)
    ik2_ref[0] = ik.astype(BF16)
    ik2_ref[1] = pltpu.roll(ik, IDX_DIM, axis=1).astype(BF16)
    sm = jnp.where(lane < _SM_IW + IDX_HEADS, g * (IDX_HEADS ** -0.5), g)
    sm_ref[...] = sm
    smt_ref[0] = sm.T
    z_ref[...] = mm(_C_Z, _C_XBC)
    xbc_ref[...] = mm(_C_XBC, _C_GA)
    ga_ref[...] = mm(_C_GA, _C_GB)
    gb_ref[...] = mm(_C_GB, _C_END)


def _inproj_call(x2, mod3, w_perm, kvw, ikw, ikb, seq):
    n_tok, d = x2.shape
    tm = 512
    row = lambda i: (i, 0)

    def ospec(n):
        return pl.BlockSpec((tm, n), row)

    spt = seq // tm
    sd = jax.ShapeDtypeStruct
    return pl.pallas_call(
        _inproj_kernel,
        grid=(n_tok // tm,),
        in_specs=[pl.BlockSpec((tm, d), row),
                  pl.BlockSpec((1, 6, d), lambda i: (i // spt, 0, 0)),
                  _const_spec(w_perm.shape), _const_spec((1, 128)), _const_spec((1, 128)), _const_spec((1, 128))],
        out_specs=[ospec(1024), ospec(128), ospec(512),
                   pl.BlockSpec((2, tm, LANES), lambda i: (0, i, 0)),
                   ospec(LANES),
                   pl.BlockSpec((1, LANES, tm), lambda i: (i // spt, 0, i % spt)),
                   ospec(1024), ospec(2048), ospec(1024), ospec(1024)],
        out_shape=[sd((n_tok, 1024), BF16), sd((n_tok, 128), BF16), sd((n_tok, 512), BF16),
                   sd((2, n_tok, LANES), BF16), sd((n_tok, LANES), F32), sd((n_tok // seq, LANES, seq), F32),
                   sd((n_tok, 1024), F32), sd((n_tok, 2048), F32), sd((n_tok, 1024), F32), sd((n_tok, 1024), F32)],
        compiler_params=_cparams(("arbitrary",)),
        name="inproj",
    )(x2, mod3, w_perm, kvw, ikw, ikb)


_TQ = 256
_KB = 256
_BIAS_STEP = math.gcd(_TQ, _KB)
_NEAR_DELTAS = tuple(range(-_KB, 1, _BIAS_STEP))
assert REL_MAX_DIST <= _BIAS_STEP
_PBITS = 15
_GUARDS = -2147450880


def _t5_bucket(rel):
    half = REL_BUCKETS // 2
    max_exact = half // 2
    ret = (rel > 0).astype(jnp.int32) * half
    n = jnp.abs(rel)
    nf = jnp.maximum(n, 1).astype(jnp.float32)
    large = max_exact + (jnp.log(nf / max_exact) / math.log(REL_MAX_DIST / max_exact)
                         * (half - max_exact)).astype(jnp.int32)
    large = jnp.minimum(large, half - 1)
    return ret + jnp.where(n < max_exact, n, large)


def _bias_tiles(rel_bias):
    i = jnp.arange(_TQ, dtype=jnp.int32)[:, None]
    c = jnp.arange(_KB, dtype=jnp.int32)[None, :]
    rel = [c + delta - i for delta in _NEAR_DELTAS] + [jnp.full((_TQ, _KB), -REL_MAX_DIST, jnp.int32)]
    bucket = _t5_bucket(jnp.stack(rel))
    b = jnp.zeros((len(rel), A_HEADS, _TQ, _KB), F32)
    for k in range(REL_BUCKETS):
        b = jnp.where((bucket == k)[:, None], rel_bias[k].astype(F32)[None, :, None, None], b)
    return (b * (A_LATENT ** 0.5)).reshape(len(rel), A_HEADS * _TQ, _KB)


def _attn_kernel(q_ref, iq_ref, iwt_ref, ik2_ref, kv_ref, bias_ref, o_ref,
                 keys_ref, pk_ref, s_ref, mx_ref, l_ref, acc_ref, *, n_sel, jbits):
    tq, kb = _TQ, _KB
    qs = pl.program_id(1) * tq
    nkb = (qs + tq + kb - 1) // kb
    nt = (((1,), (1,)), ((), ()))

    qpos = lax.broadcasted_iota(I32, (1, tq), 1) + qs
    qend = (qpos // CHUNK + 1) * CHUNK
    krow = lax.broadcasted_iota(I32, (kb, tq), 0)
    iq = iq_ref[...]
    iwt = iwt_ref[0] * (IDX_DIM ** -0.5)

    def score_body(j, carry):
        acc = jnp.zeros((kb, tq), F32)
        for p in range(IDX_HEADS // 2):
            pair = iq[:, p * LANES:(p + 1) * LANES]
            for par in range(2):
                h = 2 * p + par
                s = lax.dot_general(ik2_ref[par, 0, j], pair, nt, preferred_element_type=F32)
                acc = acc + jnp.maximum(s, 0.0) * iwt[h:h + 1, :]
        bits = pltpu.bitcast(acc, I32)
        key = jnp.where(bits < 0, bits ^ 0x7FFFFFFF, bits)
        key = jnp.where(bits == INT_MIN, 0, key)
        key = jnp.where(krow + j * kb < qend, key, INT_MIN)
        keys_ref[j] = key
        top = (key >> (32 - _PBITS)) + (1 << (_PBITS - 1))
        pk_ref[j] = (top[:kb // 2] << 16) | top[kb // 2:] | _GUARDS
        return carry

    lax.fori_loop(0, nkb, score_body, 0)

    def count_top(cand):
        both = (cand << 16) | cand

        def body(j, acc):
            g = ((pk_ref[j] - both) >> _PBITS) & 0x00010001
            parts = [g[8 * i:8 * i + 8] for i in range(kb // 16)]
            while len(parts) > 1:
                parts = [parts[i] + parts[i + 1] for i in range(0, len(parts), 2)]
            return acc + parts[0]
        acc = lax.fori_loop(0, nkb, body, jnp.zeros((8, tq), I32))
        return jnp.sum(((acc & 0xFFFF) + (acc >> 16)).astype(F32), axis=0, keepdims=True)

    def top_body(it, tu):
        cu = tu | (jnp.int32(1) << (_PBITS - 1 - it))
        return jnp.where(count_top(cu) >= n_sel, cu, tu)

    top_bits = lax.fori_loop(0, _PBITS, top_body, jnp.zeros((1, tq), I32))

    def count(pred):
        def body(j, acc):
            m = jnp.where(pred(keys_ref[j], krow + j * kb), 1.0, 0.0)
            parts = [m[8 * i:8 * i + 8] for i in range(kb // 8)]
            while len(parts) > 1:
                parts = [parts[i] + parts[i + 1] for i in range(0, len(parts), 2)]
            return acc + parts[0]
        acc = lax.fori_loop(0, nkb, body, jnp.zeros((8, tq), F32))
        return jnp.sum(acc, axis=0, keepdims=True)

    def bit_body(it, tu):
        cu = tu | (jnp.int32(1) << (31 - it))
        cs = cu ^ INT_MIN
        cnt = count(lambda k, kidx: k >= cs)
        return jnp.where(cnt >= n_sel, cu, tu)

    thr = lax.fori_loop(_PBITS, 32, bit_body, top_bits << (32 - _PBITS)) ^ INT_MIN
    n_ge = count(lambda k, kidx: k >= thr)

    def tie_search():
        need = n_sel - count(lambda k, kidx: k > thr)

        def j_body(it, jj):
            cj = jj | (jnp.int32(1) << (jbits - 1 - it))
            f = count(lambda k, kidx: (k == thr) & (kidx < cj))
            return jnp.where(f <= need, cj, jj)

        return lax.fori_loop(0, jbits, j_body, jnp.zeros((1, tq), I32))

    jj = lax.cond(jnp.max(n_ge) > n_sel, tie_search, lambda: jnp.full((1, tq), (1 << jbits) - 1, I32))

    q = q_ref[...]
    q_all = jnp.concatenate([q[:, h * LANES:(h + 1) * LANES] for h in range(A_HEADS)], axis=0)
    eye = jnp.where(lax.broadcasted_iota(I32, (tq, tq), 0) == lax.broadcasted_iota(I32, (tq, tq), 1),
                    1.0, 0.0).astype(BF16)
    mx_ref[...] = jnp.full(mx_ref.shape, -jnp.inf, F32)
    exp2_scale = A_LATENT ** -0.5 * math.log2(math.e)

    def logit_body(j, carry):
        k = keys_ref[j]
        kidx = krow + j * kb
        selt = ((k > thr) | ((k == thr) & (kidx < jj))) & (kidx < qend)
        sel = lax.dot_general(eye, jnp.where(selt, 1.0, 0.0).astype(BF16), nt, preferred_element_type=F32) > 0.5
        s = lax.dot_general(q_all, kv_ref[0, j], nt, preferred_element_type=F32)
        v = (j * kb - qs + kb) // _BIAS_STEP
        s = s + bias_ref[jnp.where(v < 0, len(_NEAR_DELTAS), v)]
        s = jnp.concatenate([jnp.where(sel, s[h * tq:(h + 1) * tq], -jnp.inf) for h in range(A_HEADS)], axis=0)
        s_ref[j] = s
        mx = mx_ref[...]
        for c in range(kb // LANES):
            mx = jnp.maximum(mx, s[:, c * LANES:(c + 1) * LANES])
        mx_ref[...] = mx
        return carry

    lax.fori_loop(0, nkb, logit_body, 0)
    m = jnp.max(mx_ref[...], axis=1, keepdims=True)
    m = jnp.where(m == -jnp.inf, 0.0, m)
    mx_ref[...] = jnp.broadcast_to(m, mx_ref.shape)
    l_ref[...] = jnp.zeros(l_ref.shape, F32)
    acc_ref[...] = jnp.zeros(acc_ref.shape, F32)

    def pv_body(j, carry):
        mb = mx_ref[...]
        s = s_ref[j]
        ps = [jnp.exp2((s[:, c * LANES:(c + 1) * LANES] - mb) * exp2_scale) for c in range(kb // LANES)]
        lsum = l_ref[...]
        for pc in ps:
            lsum = lsum + pc
        l_ref[...] = lsum
        p = jnp.concatenate(ps, axis=1).astype(BF16)
        acc_ref[...] += jnp.dot(p, kv_ref[0, j], preferred_element_type=F32)
        return carry

    lax.fori_loop(0, nkb, pv_body, 0)
    o = acc_ref[...] / jnp.sum(l_ref[...], axis=1, keepdims=True)
    for h in range(A_HEADS):
        o_ref[:, h * LANES:(h + 1) * LANES] = o[h * tq:(h + 1) * tq].astype(o_ref.dtype)


def _attn_call(q, iq, iwt, ik2, kvb, bias, bsz, seq):
    tq, kb = _TQ, _KB
    nq = seq // tq
    nblk = seq // kb
    n_sel = min(TOPK_KEYS_MAX, seq // 4)
    jbits = int(seq).bit_length()
    row = lambda b, i: (b * nq + i, 0)
    kern = functools.partial(_attn_kernel, n_sel=float(n_sel), jbits=jbits)
    return pl.pallas_call(
        kern,
        grid=(bsz, nq),
        in_specs=[pl.BlockSpec((tq, A_HEADS * A_LATENT), row),
                  pl.BlockSpec((tq, IDX_HEADS * IDX_DIM), row),
                  pl.BlockSpec((1, IDX_HEADS, tq), lambda b, i: (b, _SM_IW // IDX_HEADS, i)),
                  pl.BlockSpec((2, 1, nblk, kb, LANES), lambda b, i: (0, b, 0, 0, 0)),
                  pl.BlockSpec((1, nblk, kb, A_LATENT), lambda b, i: (b, 0, 0, 0)),
                  _const_spec(bias.shape)],
        out_specs=pl.BlockSpec((tq, A_HEADS * A_LATENT), row),
        out_shape=jax.ShapeDtypeStruct((bsz * seq, A_HEADS * A_LATENT), BF16),
        scratch_shapes=[pltpu.VMEM((nblk, kb, tq), I32),
                        pltpu.VMEM((nblk, kb // 2, tq), I32),
                        pltpu.VMEM((nblk, A_HEADS * tq, kb), F32),
                        pltpu.VMEM((A_HEADS * tq, LANES), F32),
                        pltpu.VMEM((A_HEADS * tq, LANES), F32),
                        pltpu.VMEM((A_HEADS * tq, A_LATENT), F32)],
        compiler_params=_cparams(("arbitrary", "arbitrary")),
        name="attention",
    )(q, iq, iwt, ik2, kvb, bias)


_SSD_L = 256
_PAIRS = SSM_HEADS // 2


def _ssd_kernel(z_ref, xbc_ref, sm_ref, dtt_ref, cw_ref, cb_ref, dtb_ref, dtbt_ref, al_ref, alt_ref,
                dsk_ref, nw_ref, o_ref, ext_ref, state_ref, y_ref):
    L = _SSD_L
    hd = SSM_HEADDIM

    @pl.when(pl.program_id(1) == 0)
    def _():
        ext_ref[0:8, :] = jnp.zeros((8, SSM_CONV_DIM), F32)
        state_ref[...] = jnp.zeros(state_ref.shape, F32)

    x = xbc_ref[...]
    ext_ref[8:8 + L, :] = x
    w = cw_ref[...]
    conv = x * w[3:4] + cb_ref[...]
    for k in range(1, SSM_CONV):
        conv = conv + ext_ref[8 - k:8 - k + L, :] * w[SSM_CONV - 1 - k:SSM_CONV - k]
    ext_ref[0:8, :] = x[L - 8:L]
    act = conv * jax.nn.sigmoid(conv)
    xs = act[:, :SSM_D_INNER]
    boff = SSM_D_INNER
    coff = SSM_D_INNER + SSM_GROUPS * SSM_STATE

    def softplus(v):
        return jnp.maximum(v, 0.0) + jnp.log1p(jnp.exp(-jnp.abs(v)))

    dt = softplus(sm_ref[:, _SM_DT:_SM_DT + SSM_HEADS] + dtb_ref[...])
    dtt = softplus(dtt_ref[0] + dtbt_ref[...])
    a_col = dt * (-jnp.exp(al_ref[...]))
    a_row = dtt * (-jnp.exp(alt_ref[...]))
    ri = lax.broadcasted_iota(I32, (L, L), 0)
    ci = lax.broadcasted_iota(I32, (L, L), 1)
    causal = ci <= ri
    cs_col = jnp.dot(jnp.where(causal, 1.0, 0.0), a_col, precision=HI, preferred_element_type=F32)
    cs_row = jnp.dot(a_row, jnp.where(ri <= ci, 1.0, 0.0), precision=HI, preferred_element_type=F32)
    cs_last = cs_col[L - 1:L, :]
    lane = lax.broadcasted_iota(I32, (1, LANES), 1)
    lo = lane < hd
    sub = lax.broadcasted_iota(I32, (LANES, 1), 0)

    for g in range(SSM_GROUPS):
        bm = act[:, boff + g * SSM_STATE: boff + (g + 1) * SSM_STATE].astype(BF16)
        cm = act[:, coff + g * SSM_STATE: coff + (g + 1) * SSM_STATE].astype(BF16)
        cb = lax.dot_general(cm, bm, (((1,), (1,)), ((), ())), preferred_element_type=F32)
        for pp in range(_PAIRS // SSM_GROUPS):
            p = g * (_PAIRS // SSM_GROUPS) + pp
            h0, h1 = 2 * p, 2 * p + 1
            xp = xs[:, p * LANES:(p + 1) * LANES]
            dtl = jnp.where(lo, dt[:, h0:h0 + 1], dt[:, h1:h1 + 1])
            xdt = xp * dtl
            csl = jnp.where(lo, cs_col[:, h0:h0 + 1], cs_col[:, h1:h1 + 1])
            last = jnp.where(lo, cs_last[:, h0:h0 + 1], cs_last[:, h1:h1 + 1])
            ydiag = jnp.zeros((L, LANES), F32)
            for h, msk in ((h0, lo), (h1, jnp.logical_not(lo))):
                seg = cs_col[:, h:h + 1] - cs_row[h:h + 1, :]
                gm = (cb * jnp.exp(jnp.where(causal, seg, -jnp.inf))).astype(BF16)
                ydiag = ydiag + jnp.dot(gm, jnp.where(msk, xdt, 0.0).astype(BF16), preferred_element_type=F32)
            prev = state_ref[p]
            yoff = lax.dot_general(cm, prev.astype(BF16), (((1,), (1,)), ((), ())), preferred_element_type=F32)
            y_ref[:, p * LANES:(p + 1) * LANES] = ydiag + yoff * jnp.exp(csl) + xp * dsk_ref[:, p * LANES:(p + 1) * LANES]
            wx = (xdt * jnp.exp(last - csl)).astype(BF16)
            st = lax.dot_general(wx, bm, (((0,), (0,)), ((), ())), preferred_element_type=F32)
            cdec = jnp.where(sub < hd, jnp.exp(cs_last[:, h0:h0 + 1]), jnp.exp(cs_last[:, h1:h1 + 1]))
            state_ref[p] = prev * cdec + st

    z = z_ref[...]
    y = y_ref[...] * (z * jax.nn.sigmoid(z))
    gw = SSM_D_INNER // SSM_GROUPS
    for g in range(SSM_GROUPS):
        yg = y[:, g * gw:(g + 1) * gw]
        yg = yg * lax.rsqrt(jnp.mean(yg * yg, axis=-1, keepdims=True) + LN_EPS)
        o_ref[:, g * gw:(g + 1) * gw] = (yg * nw_ref[:, g * gw:(g + 1) * gw]).astype(o_ref.dtype)


def _ssd_call(z, xbc, small, smt, conv_w, conv_b, dt_bias, a_log, d_skip, norm_w, bsz, seq):
    L = _SSD_L
    nc = seq // L
    row = lambda b, c: (b * nc + c, 0)
    h = SSM_HEADS
    return pl.pallas_call(
        _ssd_kernel,
        grid=(bsz, nc),
        in_specs=[pl.BlockSpec((L, SSM_D_INNER), row),
                  pl.BlockSpec((L, SSM_CONV_DIM), row),
                  pl.BlockSpec((L, LANES), row),
                  pl.BlockSpec((1, h, L), lambda b, c: (b, _SM_DT // h, c)),
                  _const_spec((SSM_CONV, SSM_CONV_DIM)), _const_spec((1, SSM_CONV_DIM)),
                  _const_spec((1, h)), _const_spec((h, 1)), _const_spec((1, h)), _const_spec((h, 1)),
                  _const_spec((1, SSM_D_INNER)), _const_spec((1, SSM_D_INNER))],
        out_specs=pl.BlockSpec((L, SSM_D_INNER), row),
        out_shape=jax.ShapeDtypeStruct((bsz * seq, SSM_D_INNER), BF16),
        scratch_shapes=[pltpu.VMEM((L + 8, SSM_CONV_DIM), F32),
                        pltpu.VMEM((_PAIRS, LANES, SSM_STATE), F32),
                        pltpu.VMEM((L, SSM_D_INNER), F32)],
        compiler_params=_cparams(("arbitrary", "arbitrary")),
        name="ssd",
    )(z, xbc, small, smt, conv_w, conv_b.reshape(1, -1), dt_bias.reshape(1, h), dt_bias.reshape(h, 1),
      a_log.reshape(1, h), a_log.reshape(h, 1), jnp.repeat(d_skip, SSM_HEADDIM).reshape(1, -1),
      norm_w.reshape(1, -1))


_TM = 512


def _merge_kernel(oa_ref, ob_ref, ga_ref, gb_ref, x_ref, mod_ref, wpa_ref, wpb_ref, wo_ref, g1_ref, b1_ref,
                  wrh_ref, wrl_ref, br_ref, x1_ref, u2_ref, route_ref, gate_ref, cnt_ref, base_ref, *, alpha):
    tm = _TM
    ne = N_EXPERTS

    @pl.when(pl.program_id(0) == 0)
    def _():
        base_ref[...] = jnp.zeros(base_ref.shape, F32)

    ma = jnp.dot(oa_ref[...], wpa_ref[...], preferred_element_type=F32)
    mb = jnp.dot(ob_ref[...], wpb_ref[...], preferred_element_type=F32)
    merged = jax.nn.sigmoid(ga_ref[...]) * ma + jax.nn.sigmoid(gb_ref[...]) * mb
    t = jnp.dot(merged.astype(BF16), wo_ref[...], preferred_element_type=F32)
    x1 = _ln(alpha * x_ref[...] + mod_ref[0, 2:3, :] * t) * g1_ref[...] + b1_ref[...]
    x1_ref[...] = x1
    u2 = _ln(x1) * (1.0 + mod_ref[0, 4:5, :]) + mod_ref[0, 3:4, :]
    u2_ref[...] = u2
    nt = (((1,), (1,)), ((), ()))
    uh = u2.astype(BF16)
    ul = (u2 - uh.astype(F32)).astype(BF16)
    wh, wl = wrh_ref[...], wrl_ref[...]
    logits = (lax.dot_general(wh, uh, nt, preferred_element_type=F32)
              + lax.dot_general(wl, uh, nt, preferred_element_type=F32)
              + lax.dot_general(wh, ul, nt, preferred_element_type=F32)) + br_ref[...]
    eio = lax.broadcasted_iota(I32, (ne, tm), 0).astype(F32)
    vals, ids = [], []
    for _ in range(TOPK_EXPERTS):
        m = jnp.max(logits, axis=0, keepdims=True)
        idx = jnp.min(jnp.where(logits == m, eio, float(ne)), axis=0, keepdims=True)
        vals.append(m)
        ids.append(idx)
        logits = jnp.where(eio == idx, -jnp.inf, logits)
    es = [jnp.exp(v - vals[0]) for v in vals]
    den = es[0] + es[1] + es[2] + es[3]

    onehot = jnp.zeros((ne, tm), F32)
    for idx in ids:
        onehot = onehot + jnp.where(eio == idx, 1.0, 0.0)
    ri = lax.broadcasted_iota(I32, (tm, tm), 0)
    ci = lax.broadcasted_iota(I32, (tm, tm), 1)
    before = jnp.where(ri < ci, 1.0, 0.0).astype(BF16)
    base = base_ref[...]
    prefix = jnp.dot(onehot.astype(BF16), before, preferred_element_type=F32) + base
    sub = lax.broadcasted_iota(I32, (8, tm), 0)
    route = jnp.zeros((8, tm), F32)
    gates = jnp.zeros((8, tm), F32)
    for j in range(TOPK_EXPERTS):
        rank = jnp.sum(jnp.where(eio == ids[j], prefix, 0.0), axis=0, keepdims=True)
        route = jnp.where(sub == j, ids[j], route)
        route = jnp.where(sub == TOPK_EXPERTS + j, rank, route)
        gates = jnp.where(sub == j, es[j] / den, gates)
    route_ref[...] = route.astype(I32)
    gate_ref[...] = gates
    base = base + jnp.sum(onehot, axis=1, keepdims=True)
    base_ref[...] = base
    cnt_ref[...] = jnp.broadcast_to(base, cnt_ref.shape)


def _merge_call(o_a, o_b, g_a, g_b, x2, mod3, wpa, wpb, wo, ln_g, ln_b, w_router, b_router, seq, alpha):
    n_tok, d = x2.shape
    tm = _TM
    ne = N_EXPERTS
    row = lambda i: (i, 0)
    blk = pl.BlockSpec((tm, d), row)
    sm = pl.BlockSpec((8, tm), lambda i: (0, i))
    wrt = w_router.T
    wrh = wrt.astype(BF16)
    wrl = (wrt - wrh.astype(F32)).astype(BF16)
    return pl.pallas_call(
        functools.partial(_merge_kernel, alpha=alpha),
        grid=(n_tok // tm,),
        in_specs=[blk, blk, blk, blk, blk,
                  pl.BlockSpec((1, 6, d), lambda i: ((i * tm) // seq, 0, 0)),
                  _const_spec((d, d)), _const_spec((d, d)), _const_spec((d, d)),
                  _const_spec((1, d)), _const_spec((1, d)), _const_spec((ne, d)), _const_spec((ne, d)),
                  _const_spec((ne, 1))],
        out_specs=[blk, blk, sm, sm, pl.BlockSpec((ne, LANES), lambda i: (0, 0))],
        out_shape=[jax.ShapeDtypeStruct((n_tok, d), F32), jax.ShapeDtypeStruct((n_tok, d), F32),
                   jax.ShapeDtypeStruct((8, n_tok), I32), jax.ShapeDtypeStruct((8, n_tok), F32),
                   jax.ShapeDtypeStruct((ne, LANES), F32)],
        scratch_shapes=[pltpu.VMEM((ne, 1), F32)],
        compiler_params=_cparams(("arbitrary",)),
        name="merge",
    )(o_a, o_b, g_a, g_b, x2, mod3, wpa, wpb, wo, ln_g, ln_b, wrh, wrl, b_router.reshape(ne, 1))


_TD = 2048
_TMB = 512


def _dispatch_kernel(pend_ref, dest_ref, u2_ref, xs_ref, zero_ref, sem):
    @pl.when(pl.program_id(0) == 0)
    def _():
        zero_ref[...] = jnp.zeros(zero_ref.shape, F32)
        for e in range(N_EXPERTS):
            end = pend_ref[e]
            start = pend_ref[e - 1] if e else 0

            @pl.when(end > start)
            def _():
                dst = xs_ref.at[pl.ds(pl.multiple_of(end - _TMB, _TMB), _TMB), :]
                cp = pltpu.make_async_copy(zero_ref, dst, sem)
                cp.start()
                cp.wait()

    def issue(t, carry):
        for j in range(TOPK_EXPERTS):
            d = dest_ref[t * TOPK_EXPERTS + j]
            pltpu.make_async_copy(u2_ref.at[pl.ds(t, 1), :], xs_ref.at[pl.ds(d, 1), :], sem).start()
        return carry

    lax.fori_loop(0, _TD, issue, 0)
    for _ in range(TOPK_EXPERTS):
        pltpu.make_async_copy(u2_ref, xs_ref.at[pl.ds(0, _TD), :], sem).wait()


def _dispatch_call(pends, dest_flat, u2, n_slots):
    n_tok, d = u2.shape
    grid_spec = pltpu.PrefetchScalarGridSpec(
        num_scalar_prefetch=1,
        grid=(n_tok // _TD,),
        in_specs=[pl.BlockSpec((_TD * TOPK_EXPERTS,), lambda i, pe: (i,), memory_space=pltpu.SMEM),
                  pl.BlockSpec((_TD, d), lambda i, pe: (i, 0))],
        out_specs=pl.BlockSpec(memory_space=pl.ANY),
        scratch_shapes=[pltpu.VMEM((_TMB, d), F32), pltpu.SemaphoreType.DMA(())],
    )
    return pl.pallas_call(
        _dispatch_kernel,
        grid_spec=grid_spec,
        out_shape=jax.ShapeDtypeStruct((n_slots, d), F32),
        compiler_params=_cparams(("arbitrary",)),
        name="dispatch",
    )(pends, dest_flat, u2)


def _expert_kernel(be_ref, nu_ref, xs_ref, w1_ref, b1_ref, w2_ref, b2_ref, y_ref, w1b_ref, w2b_ref):
    i = pl.program_id(0)
    f = w2_ref.shape[1]
    used = i < nu_ref[0]
    new_expert = jnp.logical_or(i == 0, be_ref[i] != be_ref[jnp.maximum(i - 1, 0)])

    @pl.when(jnp.logical_and(used, new_expert))
    def _():
        w1b_ref[...] = w1_ref[0].astype(BF16)
        w2b_ref[...] = w2_ref[0].astype(BF16)

    @pl.when(used)
    def _():
        h = jnp.dot(xs_ref[...].astype(BF16), w1b_ref[...], preferred_element_type=F32) + b1_ref[0]
        gate = jnp.minimum(h[:, :f], SWIGLU_LIMIT)
        up = jnp.clip(h[:, f:], -SWIGLU_LIMIT, SWIGLU_LIMIT)
        act = (up + 1.0) * gate * jax.nn.sigmoid(SWIGLU_ALPHA * gate)
        y_ref[...] = jnp.dot(act.astype(BF16), w2b_ref[...], preferred_element_type=F32) + b2_ref[0]

    @pl.when(i >= nu_ref[0])
    def _():
        y_ref[...] = jnp.zeros(y_ref.shape, F32)


def _expert_call(block_expert, n_used, xs, w1, b1, w2, b2):
    n_slots, d = xs.shape
    ne, _, f2 = w1.shape
    f = f2 // 2
    grid_spec = pltpu.PrefetchScalarGridSpec(
        num_scalar_prefetch=2,
        grid=(n_slots // _TMB,),
        in_specs=[pl.BlockSpec((_TMB, d), lambda i, be, nu: (i, 0)),
                  pl.BlockSpec((1, d, f2), lambda i, be, nu: (be[i], 0, 0)),
                  pl.BlockSpec((1, 1, f2), lambda i, be, nu: (be[i], 0, 0)),
                  pl.BlockSpec((1, f, d), lambda i, be, nu: (be[i], 0, 0)),
                  pl.BlockSpec((1, 1, d), lambda i, be, nu: (be[i], 0, 0))],
        out_specs=pl.BlockSpec((_TMB, d), lambda i, be, nu: (i, 0)),
        scratch_shapes=[pltpu.VMEM((d, f2), BF16), pltpu.VMEM((f, d), BF16)],
    )
    return pl.pallas_call(
        _expert_kernel,
        grid_spec=grid_spec,
        out_shape=jax.ShapeDtypeStruct((n_slots, d), F32),
        compiler_params=_cparams(("arbitrary",)),
        name="experts",
    )(block_expert, n_used, xs, w1, b1.reshape(ne, 1, f2), w2, b2.reshape(ne, 1, d))


_TC = 1024


def _combine_kernel(dest_ref, gate_ref, x1_ref, mod_ref, g2_ref, b2_ref, y_hbm, o_ref, buf_ref, sem, *, alpha):
    def issue(t, carry):
        for j in range(TOPK_EXPERTS):
            d = dest_ref[t * TOPK_EXPERTS + j]
            pltpu.make_async_copy(y_hbm.at[pl.ds(d, 1), :], buf_ref.at[j, pl.ds(t, 1), :], sem).start()
        return carry

    lax.fori_loop(0, _TC, issue, 0, unroll=2)
    for j in range(TOPK_EXPERTS):
        pltpu.make_async_copy(y_hbm.at[pl.ds(0, _TC), :], buf_ref.at[j], sem).wait()
    gates = gate_ref[...]
    y = gates[:, 0:1] * buf_ref[0]
    for j in range(1, TOPK_EXPERTS):
        y = y + gates[:, j:j + 1] * buf_ref[j]
    o_ref[...] = _ln(alpha * x1_ref[...] + mod_ref[0, 5:6, :] * y) * g2_ref[...] + b2_ref[...]


def _combine_call(dest_flat, gates, x1, mod3, ln_g, ln_b, y, seq, alpha):
    n_tok, d = x1.shape
    tc = _TC
    row = lambda i: (i, 0)
    return pl.pallas_call(
        functools.partial(_combine_kernel, alpha=alpha),
        grid=(n_tok // tc,),
        in_specs=[pl.BlockSpec((tc * TOPK_EXPERTS,), lambda i: (i,), memory_space=pltpu.SMEM),
                  pl.BlockSpec((tc, TOPK_EXPERTS), row),
                  pl.BlockSpec((tc, d), row),
                  pl.BlockSpec((1, 6, d), lambda i: ((i * tc) // seq, 0, 0)),
                  _const_spec((1, d)), _const_spec((1, d)),
                  pl.BlockSpec(memory_space=pl.ANY)],
        out_specs=pl.BlockSpec((tc, d), row),
        out_shape=jax.ShapeDtypeStruct((n_tok, d), F32),
        scratch_shapes=[pltpu.VMEM((TOPK_EXPERTS, tc, d), F32), pltpu.SemaphoreType.DMA(())],
        compiler_params=_cparams(("arbitrary",)),
        name="combine",
    )(dest_flat, gates, x1, mod3, ln_g, ln_b, y)


def _permute_w_in(w):
    d = w.shape[0]
    s = np.cumsum([0, A_HEADS * A_LATENT, A_LATENT, IDX_HEADS * IDX_DIM, IDX_DIM, IDX_HEADS,
                   SSM_D_INNER, SSM_CONV_DIM, SSM_HEADS, d, d]).tolist()
    q, kv, iq, ik, iw, z, xbc, dt, ga, gb = [w[:, s[i]:s[i + 1]] for i in range(10)]
    pad1 = jnp.zeros((d, _SM_DT - _SM_IW - IDX_HEADS), w.dtype)
    pad2 = jnp.zeros((d, LANES - _SM_DT - SSM_HEADS), w.dtype)
    return jnp.concatenate([q, kv, iq, ik, iw, pad1, dt, pad2, z, xbc, ga, gb], axis=1).astype(BF16)


def _pad_lanes(v, fill=0.0):
    return jnp.pad(v.reshape(1, -1), ((0, 0), (0, LANES - v.shape[-1])), constant_values=fill)


def kernel(x, c, w_mod, b_mod, w_in, kv_norm_w, idx_k_norm_w, idx_k_norm_b, rel_bias, conv_w, conv_b, dt_bias,
           a_log, d_skip, ssm_norm_w, w_proj_a, w_proj_b, w_out, ln1_g, ln1_b, w_router, b_router, w1, b1, w2, b2,
           ln2_g, ln2_b):
    bsz, seq, d = x.shape
    depth = w_mod.shape[0]
    alpha = (2.0 * depth) ** 0.25
    n_tok = bsz * seq
    n_asg = n_tok * TOPK_EXPERTS
    n_blocks = n_asg // _TMB + N_EXPERTS
    n_slots = n_blocks * _TMB
    nblk = seq // _KB
    bias = _bias_tiles(rel_bias)
    x2 = x.reshape(n_tok, d)
    for l in range(depth):
        mod3 = _mod_call(c, w_mod[l], b_mod[l]).reshape(bsz, 6, d)
        q, kvn, iq, ik2, small, smt, z, xbc, g_a, g_b = _inproj_call(
            x2, mod3, _permute_w_in(w_in[l]), kv_norm_w[l].reshape(1, -1),
            _pad_lanes(idx_k_norm_w[l]), _pad_lanes(idx_k_norm_b[l]), seq)
        kvb = kvn.reshape(bsz, nblk, _KB, A_LATENT)
        o_a = _attn_call(q, iq, smt, ik2.reshape(2, bsz, nblk, _KB, LANES), kvb, bias, bsz, seq)
        o_b = _ssd_call(z, xbc, small, smt, conv_w[l], conv_b[l], dt_bias[l], a_log[l], d_skip[l], ssm_norm_w[l],
                        bsz, seq)
        x1, u2, route, gates, cnt = _merge_call(
            o_a, o_b, g_a, g_b, x2, mod3, w_proj_a[l].astype(BF16), w_proj_b[l].astype(BF16),
            w_out[l].astype(BF16), ln1_g[l].reshape(1, -1), ln1_b[l].reshape(1, -1), w_router[l], b_router[l],
            seq, alpha)
        counts = cnt[:, 0].astype(I32)
        padded = (counts + _TMB - 1) // _TMB * _TMB
        pends = jnp.cumsum(padded).astype(I32)
        pstarts = pends - padded
        eid = route[:TOPK_EXPERTS]
        onehot = eid[:, :, None] == jnp.arange(N_EXPERTS, dtype=I32)
        dest = jnp.sum(jnp.where(onehot, pstarts, 0), axis=-1) + route[TOPK_EXPERTS:2 * TOPK_EXPERTS]
        dest_flat = dest.T.reshape(n_asg)
        block_start = jnp.arange(n_blocks, dtype=I32) * _TMB
        block_expert = jnp.minimum(jnp.sum(block_start[:, None] >= pends[None, :], axis=1), N_EXPERTS - 1).astype(I32)
        n_used = (pends[-1:] // _TMB).astype(I32)
        xs = _dispatch_call(pends, dest_flat, u2, n_slots)
        y = _expert_call(block_expert, n_used, xs, w1[l], b1[l], w2[l], b2[l])
        x2 = _combine_call(dest_flat, gates[:TOPK_EXPERTS].T, x1, mod3, ln2_g[l].reshape(1, -1),
                           ln2_b[l].reshape(1, -1), y, seq, alpha)
    return x2.reshape(bsz, seq, d)
```

```python
import functools
import math

import jax
import jax.numpy as jnp
import numpy as np
from jax import lax
from jax.experimental import pallas as pl
from jax.experimental.pallas import tpu as pltpu

F32 = jnp.float32
BF16 = jnp.bfloat16
I32 = jnp.int32

CHUNK = 64
A_HEADS = 8
A_LATENT = 128
IDX_HEADS = 8
IDX_DIM = 64
TOPK_KEYS_MAX = 256
REL_BUCKETS = 32
REL_MAX_DIST = 128
SSM_D_INNER = 1024
SSM_HEADDIM = 64
SSM_HEADS = SSM_D_INNER // SSM_HEADDIM
SSM_GROUPS = 4
SSM_STATE = 128
SSM_CONV = 4
SSM_CONV_DIM = SSM_D_INNER + 2 * SSM_GROUPS * SSM_STATE
N_EXPERTS = 32
TOPK_EXPERTS = 4
SWIGLU_LIMIT = 7.0
SWIGLU_ALPHA = 1.702
LN_EPS = 1e-5

LANES = 128
INT_MIN = -2147483648
VMEM_LIMIT = 56 * 1024 * 1024

HI = lax.Precision.HIGHEST


def _cparams(sem):
    return pltpu.CompilerParams(dimension_semantics=sem, vmem_limit_bytes=VMEM_LIMIT)


def _ln(x):
    mu = jnp.mean(x, axis=-1, keepdims=True)
    xc = x - mu
    var = jnp.mean(xc * xc, axis=-1, keepdims=True)
    return xc * lax.rsqrt(var + LN_EPS)


def _const_spec(shape):
    nd = len(shape)
    return pl.BlockSpec(shape, lambda *_: (0,) * nd, pipeline_mode=pl.Buffered(1))


def _mod_kernel(c_ref, w_ref, b_ref, o_ref):
    c = c_ref[...]
    sc = c * jax.nn.sigmoid(c)
    o_ref[...] = jnp.dot(sc, w_ref[...], precision=HI, preferred_element_type=F32) + b_ref[...]


def _mod_call(c, w_mod, b_mod):
    bsz, d = c.shape
    n = w_mod.shape[1]
    tn = 1024
    return pl.pallas_call(
        _mod_kernel,
        grid=(n // tn,),
        in_specs=[pl.BlockSpec((bsz, d), lambda j: (0, 0)),
                  pl.BlockSpec((d, tn), lambda j: (0, j)),
                  pl.BlockSpec((1, tn), lambda j: (0, j))],
        out_specs=pl.BlockSpec((bsz, tn), lambda j: (0, j)),
        out_shape=jax.ShapeDtypeStruct((bsz, n), F32),
        compiler_params=_cparams(("arbitrary",)),
        name="mod",
    )(c, w_mod, b_mod.reshape(1, n))


_C_Q, _C_KV, _C_IQ, _C_SM, _C_Z, _C_XBC, _C_GA, _C_GB, _C_END = 0, 1024, 1152, 1664, 1792, 2816, 4864, 5888, 6912
_SM_IW = IDX_DIM
_SM_DT = IDX_DIM + SSM_HEADS


def _inproj_kernel(x_ref, mod_ref, w_ref, kvw_ref, ikw_ref, ikb_ref,
                   q_ref, kv_ref, iq_ref, ik2_ref, sm_ref, smt_ref, z_ref, xbc_ref, ga_ref, gb_ref):
    u = _ln(x_ref[...]) * (1.0 + mod_ref[0, 1:2, :]) + mod_ref[0, 0:1, :]
    ub = u.astype(BF16)

    def mm(a, b):
        return jnp.dot(ub, w_ref[:, a:b], preferred_element_type=F32)

    q_ref[...] = mm(_C_Q, _C_KV).astype(BF16)
    kv = mm(_C_KV, _C_IQ)
    kv = kv * lax.rsqrt(jnp.mean(kv * kv, axis=-1, keepdims=True) + LN_EPS)
    kv_ref[...] = (kv * kvw_ref[...]).astype(BF16)
    iq_ref[...] = mm(_C_IQ, _C_SM).astype(BF16)
    g = mm(_C_SM, _C_Z)
    lane = lax.broadcasted_iota(I32, g.shape, 1)
    is_ik = lane < IDX_DIM
    mu = jnp.sum(jnp.where(is_ik, g, 0.0), axis=-1, keepdims=True) * (1.0 / IDX_DIM)
    gc = g - mu
    var = jnp.sum(jnp.where(is_ik, gc * gc, 0.0), axis=-1, keepdims=True) * (1.0 / IDX_DIM)
    ik = jnp.where(is_ik, gc * lax.rsqrt(var + LN_EPS) * ikw_ref[...] + ikb_ref[...], 0.0)
    ik2_ref[0] = ik.astype(BF16)
    ik2_ref[1] = pltpu.roll(ik, IDX_DIM, axis=1).astype(BF16)
    sm = jnp.where(lane < _SM_IW + IDX_HEADS, g * (IDX_HEADS ** -0.5), g)
    sm_ref[...] = sm
    smt_ref[0] = sm.T
    z_ref[...] = mm(_C_Z, _C_XBC)
    xbc_ref[...] = mm(_C_XBC, _C_GA)
    ga_ref[...] = mm(_C_GA, _C_GB)
    gb_ref[...] = mm(_C_GB, _C_END)


def _inproj_call(x2, mod3, w_perm, kvw, ikw, ikb, seq):
    n_tok, d = x2.shape
    tm = 512
    row = lambda i: (i, 0)

    def ospec(n):
        return pl.BlockSpec((tm, n), row)

    spt = seq // tm
    sd = jax.ShapeDtypeStruct
    return pl.pallas_call(
        _inproj_kernel,
        grid=(n_tok // tm,),
        in_specs=[pl.BlockSpec((tm, d), row),
                  pl.BlockSpec((1, 6, d), lambda i: (i // spt, 0, 0)),
                  _const_spec(w_perm.shape), _const_spec((1, 128)), _const_spec((1, 128)), _const_spec((1, 128))],
        out_specs=[ospec(1024), ospec(128), ospec(512),
                   pl.BlockSpec((2, tm, LANES), lambda i: (0, i, 0)),
                   ospec(LANES),
                   pl.BlockSpec((1, LANES, tm), lambda i: (i // spt, 0, i % spt)),
                   ospec(1024), ospec(2048), ospec(1024), ospec(1024)],
        out_shape=[sd((n_tok, 1024), BF16), sd((n_tok, 128), BF16), sd((n_tok, 512), BF16),
                   sd((2, n_tok, LANES), BF16), sd((n_tok, LANES), F32), sd((n_tok // seq, LANES, seq), F32),
                   sd((n_tok, 1024), F32), sd((n_tok, 2048), F32), sd((n_tok, 1024), F32), sd((n_tok, 1024), F32)],
        compiler_params=_cparams(("arbitrary",)),
        name="inproj",
    )(x2, mod3, w_perm, kvw, ikw, ikb)


_TQ = 256
_KB = 256
_BIAS_STEP = math.gcd(_TQ, _KB)
_NEAR_DELTAS = tuple(range(-_KB, 1, _BIAS_STEP))
assert REL_MAX_DIST <= _BIAS_STEP
_PBITS = 15
_GUARDS = -2147450880


def _t5_bucket(rel):
    half = REL_BUCKETS // 2
    max_exact = half // 2
    ret = (rel > 0).astype(jnp.int32) * half
    n = jnp.abs(rel)
    nf = jnp.maximum(n, 1).astype(jnp.float32)
    large = max_exact + (jnp.log(nf / max_exact) / math.log(REL_MAX_DIST / max_exact)
                         * (half - max_exact)).astype(jnp.int32)
    large = jnp.minimum(large, half - 1)
    return ret + jnp.where(n < max_exact, n, large)


def _bias_tiles(rel_bias):
    i = jnp.arange(_TQ, dtype=jnp.int32)[:, None]
    c = jnp.arange(_KB, dtype=jnp.int32)[None, :]
    rel = [c + delta - i for delta in _NEAR_DELTAS] + [jnp.full((_TQ, _KB), -REL_MAX_DIST, jnp.int32)]
    bucket = _t5_bucket(jnp.stack(rel))
    b = jnp.zeros((len(rel), A_HEADS, _TQ, _KB), F32)
    for k in range(REL_BUCKETS):
        b = jnp.where((bucket == k)[:, None], rel_bias[k].astype(F32)[None, :, None, None], b)
    return (b * (A_LATENT ** 0.5)).reshape(len(rel), A_HEADS * _TQ, _KB)


def _attn_kernel(q_ref, iq_ref, iwt_ref, ik2_ref, kv_ref, bias_ref, o_ref,
                 keys_ref, pk_ref, s_ref, mx_ref, l_ref, acc_ref, *, n_sel, jbits):
    tq, kb = _TQ, _KB
    qs = pl.program_id(1) * tq
    nkb = (qs + tq + kb - 1) // kb
    nt = (((1,), (1,)), ((), ()))

    qpos = lax.broadcasted_iota(I32, (1, tq), 1) + qs
    qend = (qpos // CHUNK + 1) * CHUNK
    krow = lax.broadcasted_iota(I32, (kb, tq), 0)
    iq = iq_ref[...]
    iwt = iwt_ref[0] * (IDX_DIM ** -0.5)

    def score_body(j, carry):
        acc = jnp.zeros((kb, tq), F32)
        for p in range(IDX_HEADS // 2):
            pair = iq[:, p * LANES:(p + 1) * LANES]
            for par in range(2):
                h = 2 * p + par
                s = lax.dot_general(ik2_ref[par, 0, j], pair, nt, preferred_element_type=F32)
                acc = acc + jnp.maximum(s, 0.0) * iwt[h:h + 1, :]
        bits = pltpu.bitcast(acc, I32)
        key = jnp.where(bits < 0, bits ^ 0x7FFFFFFF, bits)
        key = jnp.where(bits == INT_MIN, 0, key)
        key = jnp.where(krow + j * kb < qend, key, INT_MIN)
        keys_ref[j] = key
        top = (key >> (32 - _PBITS)) + (1 << (_PBITS - 1))
        pk_ref[j] = (top[:kb // 2] << 16) | top[kb // 2:] | _GUARDS
        return carry

    lax.fori_loop(0, nkb, score_body, 0)

    def count_top(cand):
        both = (cand << 16) | cand

        def body(j, acc):
            g = ((pk_ref[j] - both) >> _PBITS) & 0x00010001
            parts = [g[8 * i:8 * i + 8] for i in range(kb // 16)]
            while len(parts) > 1:
                parts = [parts[i] + parts[i + 1] for i in range(0, len(parts), 2)]
            return acc + parts[0]
        acc = lax.fori_loop(0, nkb, body, jnp.zeros((8, tq), I32))
        return jnp.sum(((acc & 0xFFFF) + (acc >> 16)).astype(F32), axis=0, keepdims=True)

    def top_body(it, tu):
        cu = tu | (jnp.int32(1) << (_PBITS - 1 - it))
        return jnp.where(count_top(cu) >= n_sel, cu, tu)

    top_bits = lax.fori_loop(0, _PBITS, top_body, jnp.zeros((1, tq), I32))

    def count(pred):
        def body(j, acc):
            m = jnp.where(pred(keys_ref[j], krow + j * kb), 1.0, 0.0)
            parts = [m[8 * i:8 * i + 8] for i in range(kb // 8)]
            while len(parts) > 1:
                parts = [parts[i] + parts[i + 1] for i in range(0, len(parts), 2)]
            return acc + parts[0]
        acc = lax.fori_loop(0, nkb, body, jnp.zeros((8, tq), F32))
        return jnp.sum(acc, axis=0, keepdims=True)

    def bit_body(it, tu):
        cu = tu | (jnp.int32(1) << (31 - it))
        cs = cu ^ INT_MIN
        cnt = count(lambda k, kidx: k >= cs)
        return jnp.where(cnt >= n_sel, cu, tu)

    thr = lax.fori_loop(_PBITS, 32, bit_body, top_bits << (32 - _PBITS)) ^ INT_MIN
    n_ge = count(lambda k, kidx: k >= thr)

    def tie_search():
        need = n_sel - count(lambda k, kidx: k > thr)

        def j_body(it, jj):
            cj = jj | (jnp.int32(1) << (jbits - 1 - it))
            f = count(lambda k, kidx: (k == thr) & (kidx < cj))
            return jnp.where(f <= need, cj, jj)

        return lax.fori_loop(0, jbits, j_body, jnp.zeros((1, tq), I32))

    jj = lax.cond(jnp.max(n_ge) > n_sel, tie_search, lambda: jnp.full((1, tq), (1 << jbits) - 1, I32))

    q = q_ref[...]
    q_all = jnp.concatenate([q[:, h * LANES:(h + 1) * LANES] for h in range(A_HEADS)], axis=0)
    eye = jnp.where(lax.broadcasted_iota(I32, (tq, tq), 0) == lax.broadcasted_iota(I32, (tq, tq), 1),
                    1.0, 0.0).astype(BF16)
    mx_ref[...] = jnp.full(mx_ref.shape, -jnp.inf, F32)
    exp2_scale = A_LATENT ** -0.5 * math.log2(math.e)

    def logit_body(j, carry):
        k = keys_ref[j]
        kidx = krow + j * kb
        selt = ((k > thr) | ((k == thr) & (kidx < jj))) & (kidx < qend)
        sel = lax.dot_general(eye, jnp.where(selt, 1.0, 0.0).astype(BF16), nt, preferred_element_type=F32) > 0.5
        s = lax.dot_general(q_all, kv_ref[0, j], nt, preferred_element_type=F32)
        v = (j * kb - qs + kb) // _BIAS_STEP
        s = s + bias_ref[jnp.where(v < 0, len(_NEAR_DELTAS), v)]
        s = jnp.concatenate([jnp.where(sel, s[h * tq:(h + 1) * tq], -jnp.inf) for h in range(A_HEADS)], axis=0)
        s_ref[j] = s
        mx = mx_ref[...]
        for c in range(kb // LANES):
            mx = jnp.maximum(mx, s[:, c * LANES:(c + 1) * LANES])
        mx_ref[...] = mx
        return carry

    lax.fori_loop(0, nkb, logit_body, 0)
    m = jnp.max(mx_ref[...], axis=1, keepdims=True)
    m = jnp.where(m == -jnp.inf, 0.0, m)
    mx_ref[...] = jnp.broadcast_to(m, mx_ref.shape)
    l_ref[...] = jnp.zeros(l_ref.shape, F32)
    acc_ref[...] = jnp.zeros(acc_ref.shape, F32)

    def pv_body(j, carry):
        mb = mx_ref[...]
        s = s_ref[j]
        ps = [jnp.exp2((s[:, c * LANES:(c + 1) * LANES] - mb) * exp2_scale) for c in range(kb // LANES)]
        lsum = l_ref[...]
        for pc in ps:
            lsum = lsum + pc
        l_ref[...] = lsum
        p = jnp.concatenate(ps, axis=1).astype(BF16)
        acc_ref[...] += jnp.dot(p, kv_ref[0, j], preferred_element_type=F32)
        return carry

    lax.fori_loop(0, nkb, pv_body, 0)
    o = acc_ref[...] / jnp.sum(l_ref[...], axis=1, keepdims=True)
    for h in range(A_HEADS):
        o_ref[:, h * LANES:(h + 1) * LANES] = o[h * tq:(h + 1) * tq].astype(o_ref.dtype)


def _attn_call(q, iq, iwt, ik2, kvb, bias, bsz, seq):
    tq, kb = _TQ, _KB
    nq = seq // tq
    nblk = seq // kb
    n_sel = min(TOPK_KEYS_MAX, seq // 4)
    jbits = int(seq).bit_length()
    row = lambda b, i: (b * nq + i, 0)
    kern = functools.partial(_attn_kernel, n_sel=float(n_sel), jbits=jbits)
    return pl.pallas_call(
        kern,
        grid=(bsz, nq),
        in_specs=[pl.BlockSpec((tq, A_HEADS * A_LATENT), row),
                  pl.BlockSpec((tq, IDX_HEADS * IDX_DIM), row),
                  pl.BlockSpec((1, IDX_HEADS, tq), lambda b, i: (b, _SM_IW // IDX_HEADS, i)),
                  pl.BlockSpec((2, 1, nblk, kb, LANES), lambda b, i: (0, b, 0, 0, 0)),
                  pl.BlockSpec((1, nblk, kb, A_LATENT), lambda b, i: (b, 0, 0, 0)),
                  _const_spec(bias.shape)],
        out_specs=pl.BlockSpec((tq, A_HEADS * A_LATENT), row),
        out_shape=jax.ShapeDtypeStruct((bsz * seq, A_HEADS * A_LATENT), BF16),
        scratch_shapes=[pltpu.VMEM((nblk, kb, tq), I32),
                        pltpu.VMEM((nblk, kb // 2, tq), I32),
                        pltpu.VMEM((nblk, A_HEADS * tq, kb), F32),
                        pltpu.VMEM((A_HEADS * tq, LANES), F32),
                        pltpu.VMEM((A_HEADS * tq, LANES), F32),
                        pltpu.VMEM((A_HEADS * tq, A_LATENT), F32)],
        compiler_params=_cparams(("arbitrary", "arbitrary")),
        name="attention",
    )(q, iq, iwt, ik2, kvb, bias)


_SSD_L = 256
_PAIRS = SSM_HEADS // 2


def _ssd_kernel(z_ref, xbc_ref, sm_ref, dtt_ref, cw_ref, cb_ref, dtb_ref, dtbt_ref, al_ref, alt_ref,
                dsk_ref, nw_ref, o_ref, ext_ref, state_ref, y_ref):
    L = _SSD_L
    hd = SSM_HEADDIM

    @pl.when(pl.program_id(1) == 0)
    def _():
        ext_ref[0:8, :] = jnp.zeros((8, SSM_CONV_DIM), F32)
        state_ref[...] = jnp.zeros(state_ref.shape, F32)

    x = xbc_ref[...]
    ext_ref[8:8 + L, :] = x
    w = cw_ref[...]
    conv = x * w[3:4] + cb_ref[...]
    for k in range(1, SSM_CONV):
        conv = conv + ext_ref[8 - k:8 - k + L, :] * w[SSM_CONV - 1 - k:SSM_CONV - k]
    ext_ref[0:8, :] = x[L - 8:L]
    act = conv * jax.nn.sigmoid(conv)
    xs = act[:, :SSM_D_INNER]
    boff = SSM_D_INNER
    coff = SSM_D_INNER + SSM_GROUPS * SSM_STATE

    def softplus(v):
        return jnp.maximum(v, 0.0) + jnp.log1p(jnp.exp(-jnp.abs(v)))

    dt = softplus(sm_ref[:, _SM_DT:_SM_DT + SSM_HEADS] + dtb_ref[...])
    dtt = softplus(dtt_ref[0] + dtbt_ref[...])
    a_col = dt * (-jnp.exp(al_ref[...]))
    a_row = dtt * (-jnp.exp(alt_ref[...]))
    ri = lax.broadcasted_iota(I32, (L, L), 0)
    ci = lax.broadcasted_iota(I32, (L, L), 1)
    causal = ci <= ri
    cs_col = jnp.dot(jnp.where(causal, 1.0, 0.0), a_col, precision=HI, preferred_element_type=F32)
    cs_row = jnp.dot(a_row, jnp.where(ri <= ci, 1.0, 0.0), precision=HI, preferred_element_type=F32)
    cs_last = cs_col[L - 1:L, :]
    lane = lax.broadcasted_iota(I32, (1, LANES), 1)
    lo = lane < hd
    sub = lax.broadcasted_iota(I32, (LANES, 1), 0)

    for g in range(SSM_GROUPS):
        bm = act[:, boff + g * SSM_STATE: boff + (g + 1) * SSM_STATE].astype(BF16)
        cm = act[:, coff + g * SSM_STATE: coff + (g + 1) * SSM_STATE].astype(BF16)
        cb = lax.dot_general(cm, bm, (((1,), (1,)), ((), ())), preferred_element_type=F32)
        for pp in range(_PAIRS // SSM_GROUPS):
            p = g * (_PAIRS // SSM_GROUPS) + pp
            h0, h1 = 2 * p, 2 * p + 1
            xp = xs[:, p * LANES:(p + 1) * LANES]
            dtl = jnp.where(lo, dt[:, h0:h0 + 1], dt[:, h1:h1 + 1])
            xdt = xp * dtl
            csl = jnp.where(lo, cs_col[:, h0:h0 + 1], cs_col[:, h1:h1 + 1])
            last = jnp.where(lo, cs_last[:, h0:h0 + 1], cs_last[:, h1:h1 + 1])
            ydiag = jnp.zeros((L, LANES), F32)
            for h, msk in ((h0, lo), (h1, jnp.logical_not(lo))):
                seg = cs_col[:, h:h + 1] - cs_row[h:h + 1, :]
                gm = (cb * jnp.exp(jnp.where(causal, seg, -jnp.inf))).astype(BF16)
                ydiag = ydiag + jnp.dot(gm, jnp.where(msk, xdt, 0.0).astype(BF16), preferred_element_type=F32)
            prev = state_ref[p]
            yoff = lax.dot_general(cm, prev.astype(BF16), (((1,), (1,)), ((), ())), preferred_element_type=F32)
            y_ref[:, p * LANES:(p + 1) * LANES] = ydiag + yoff * jnp.exp(csl) + xp * dsk_ref[:, p * LANES:(p + 1) * LANES]
            wx = (xdt * jnp.exp(last - csl)).astype(BF16)
            st = lax.dot_general(wx, bm, (((0,), (0,)), ((), ())), preferred_element_type=F32)
            cdec = jnp.where(sub < hd, jnp.exp(cs_last[:, h0:h0 + 1]), jnp.exp(cs_last[:, h1:h1 + 1]))
            state_ref[p] = prev * cdec + st

    z = z_ref[...]
    y = y_ref[...] * (z * jax.nn.sigmoid(z))
    gw = SSM_D_INNER // SSM_GROUPS
    for g in range(SSM_GROUPS):
        yg = y[:, g * gw:(g + 1) * gw]
        yg = yg * lax.rsqrt(jnp.mean(yg * yg, axis=-1, keepdims=True) + LN_EPS)
        o_ref[:, g * gw:(g + 1) * gw] = (yg * nw_ref[:, g * gw:(g + 1) * gw]).astype(o_ref.dtype)


def _ssd_call(z, xbc, small, smt, conv_w, conv_b, dt_bias, a_log, d_skip, norm_w, bsz, seq):
    L = _SSD_L
    nc = seq // L
    row = lambda b, c: (b * nc + c, 0)
    h = SSM_HEADS
    return pl.pallas_call(
        _ssd_kernel,
        grid=(bsz, nc),
        in_specs=[pl.BlockSpec((L, SSM_D_INNER), row),
                  pl.BlockSpec((L, SSM_CONV_DIM), row),
                  pl.BlockSpec((L, LANES), row),
                  pl.BlockSpec((1, h, L), lambda b, c: (b, _SM_DT // h, c)),
                  _const_spec((SSM_CONV, SSM_CONV_DIM)), _const_spec((1, SSM_CONV_DIM)),
                  _const_spec((1, h)), _const_spec((h, 1)), _const_spec((1, h)), _const_spec((h, 1)),
                  _const_spec((1, SSM_D_INNER)), _const_spec((1, SSM_D_INNER))],
        out_specs=pl.BlockSpec((L, SSM_D_INNER), row),
        out_shape=jax.ShapeDtypeStruct((bsz * seq, SSM_D_INNER), BF16),
        scratch_shapes=[pltpu.VMEM((L + 8, SSM_CONV_DIM), F32),
                        pltpu.VMEM((_PAIRS, LANES, SSM_STATE), F32),
                        pltpu.VMEM((L, SSM_D_INNER), F32)],
        compiler_params=_cparams(("arbitrary", "arbitrary")),
        name="ssd",
    )(z, xbc, small, smt, conv_w, conv_b.reshape(1, -1), dt_bias.reshape(1, h), dt_bias.reshape(h, 1),
      a_log.reshape(1, h), a_log.reshape(h, 1), jnp.repeat(d_skip, SSM_HEADDIM).reshape(1, -1),
      norm_w.reshape(1, -1))


_TM = 512


def _merge_kernel(oa_ref, ob_ref, ga_ref, gb_ref, x_ref, mod_ref, wpa_ref, wpb_ref, wo_ref, g1_ref, b1_ref,
                  wrh_ref, wrl_ref, br_ref, x1_ref, u2_ref, route_ref, gate_ref, cnt_ref, base_ref, *, alpha):
    tm = _TM
    ne = N_EXPERTS

    @pl.when(pl.program_id(0) == 0)
    def _():
        base_ref[...] = jnp.zeros(base_ref.shape, F32)

    ma = jnp.dot(oa_ref[...], wpa_ref[...], preferred_element_type=F32)
    mb = jnp.dot(ob_ref[...], wpb_ref[...], preferred_element_type=F32)
    merged = jax.nn.sigmoid(ga_ref[...]) * ma + jax.nn.sigmoid(gb_ref[...]) * mb
    t = jnp.dot(merged.astype(BF16), wo_ref[...], preferred_element_type=F32)
    x1 = _ln(alpha * x_ref[...] + mod_ref[0, 2:3, :] * t) * g1_ref[...] + b1_ref[...]
    x1_ref[...] = x1
    u2 = _ln(x1) * (1.0 + mod_ref[0, 4:5, :]) + mod_ref[0, 3:4, :]
    u2_ref[...] = u2
    nt = (((1,), (1,)), ((), ()))
    uh = u2.astype(BF16)
    ul = (u2 - uh.astype(F32)).astype(BF16)
    wh, wl = wrh_ref[...], wrl_ref[...]
    logits = (lax.dot_general(wh, uh, nt, preferred_element_type=F32)
              + lax.dot_general(wl, uh, nt, preferred_element_type=F32)
              + lax.dot_general(wh, ul, nt, preferred_element_type=F32)) + br_ref[...]
    eio = lax.broadcasted_iota(I32, (ne, tm), 0).astype(F32)
    vals, ids = [], []
    for _ in range(TOPK_EXPERTS):
        m = jnp.max(logits, axis=0, keepdims=True)
        idx = jnp.min(jnp.where(logits == m, eio, float(ne)), axis=0, keepdims=True)
        vals.append(m)
        ids.append(idx)
        logits = jnp.where(eio == idx, -jnp.inf, logits)
    es = [jnp.exp(v - vals[0]) for v in vals]
    den = es[0] + es[1] + es[2] + es[3]

    onehot = jnp.zeros((ne, tm), F32)
    for idx in ids:
        onehot = onehot + jnp.where(eio == idx, 1.0, 0.0)
    ri = lax.broadcasted_iota(I32, (tm, tm), 0)
    ci = lax.broadcasted_iota(I32, (tm, tm), 1)
    before = jnp.where(ri < ci, 1.0, 0.0).astype(BF16)
    base = base_ref[...]
    prefix = jnp.dot(onehot.astype(BF16), before, preferred_element_type=F32) + base
    sub = lax.broadcasted_iota(I32, (8, tm), 0)
    route = jnp.zeros((8, tm), F32)
    gates = jnp.zeros((8, tm), F32)
    for j in range(TOPK_EXPERTS):
        rank = jnp.sum(jnp.where(eio == ids[j], prefix, 0.0), axis=0, keepdims=True)
        route = jnp.where(sub == j, ids[j], route)
        route = jnp.where(sub == TOPK_EXPERTS + j, rank, route)
        gates = jnp.where(sub == j, es[j] / den, gates)
    route_ref[...] = route.astype(I32)
    gate_ref[...] = gates
    base = base + jnp.sum(onehot, axis=1, keepdims=True)
    base_ref[...] = base
    cnt_ref[...] = jnp.broadcast_to(base, cnt_ref.shape)


def _merge_call(o_a, o_b, g_a, g_b, x2, mod3, wpa, wpb, wo, ln_g, ln_b, w_router, b_router, seq, alpha):
    n_tok, d = x2.shape
    tm = _TM
    ne = N_EXPERTS
    row = lambda i: (i, 0)
    blk = pl.BlockSpec((tm, d), row)
    sm = pl.BlockSpec((8, tm), lambda i: (0, i))
    wrt = w_router.T
    wrh = wrt.astype(BF16)
    wrl = (wrt - wrh.astype(F32)).astype(BF16)
    return pl.pallas_call(
        functools.partial(_merge_kernel, alpha=alpha),
        grid=(n_tok // tm,),
        in_specs=[blk, blk, blk, blk, blk,
                  pl.BlockSpec((1, 6, d), lambda i: ((i * tm) // seq, 0, 0)),
                  _const_spec((d, d)), _const_spec((d, d)), _const_spec((d, d)),
                  _const_spec((1, d)), _const_spec((1, d)), _const_spec((ne, d)), _const_spec((ne, d)),
                  _const_spec((ne, 1))],
        out_specs=[blk, blk, sm, sm, pl.BlockSpec((ne, LANES), lambda i: (0, 0))],
        out_shape=[jax.ShapeDtypeStruct((n_tok, d), F32), jax.ShapeDtypeStruct((n_tok, d), F32),
                   jax.ShapeDtypeStruct((8, n_tok), I32), jax.ShapeDtypeStruct((8, n_tok), F32),
                   jax.ShapeDtypeStruct((ne, LANES), F32)],
        scratch_shapes=[pltpu.VMEM((ne, 1), F32)],
        compiler_params=_cparams(("arbitrary",)),
        name="merge",
    )(o_a, o_b, g_a, g_b, x2, mod3, wpa, wpb, wo, ln_g, ln_b, wrh, wrl, b_router.reshape(ne, 1))


_TD = 2048
_TMB = 512


def _dispatch_kernel(pend_ref, dest_ref, u2_ref, xs_ref, zero_ref, sem):
    @pl.when(pl.program_id(0) == 0)
    def _():
        zero_ref[...] = jnp.zeros(zero_ref.shape, F32)
        for e in range(N_EXPERTS):
            end = pend_ref[e]
            start = pend_ref[e - 1] if e else 0

            @pl.when(end > start)
            def _():
                dst = xs_ref.at[pl.ds(pl.multiple_of(end - _TMB, _TMB), _TMB), :]
                cp = pltpu.make_async_copy(zero_ref, dst, sem)
                cp.start()
                cp.wait()

    def issue(t, carry):
        for j in range(TOPK_EXPERTS):
            d = dest_ref[t * TOPK_EXPERTS + j]
            pltpu.make_async_copy(u2_ref.at[pl.ds(t, 1), :], xs_ref.at[pl.ds(d, 1), :], sem).start()
        return carry

    lax.fori_loop(0, _TD, issue, 0)
    for _ in range(TOPK_EXPERTS):
        pltpu.make_async_copy(u2_ref, xs_ref.at[pl.ds(0, _TD), :], sem).wait()


def _dispatch_call(pends, dest_flat, u2, n_slots):
    n_tok, d = u2.shape
    grid_spec = pltpu.PrefetchScalarGridSpec(
        num_scalar_prefetch=1,
        grid=(n_tok // _TD,),
        in_specs=[pl.BlockSpec((_TD * TOPK_EXPERTS,), lambda i, pe: (i,), memory_space=pltpu.SMEM),
                  pl.BlockSpec((_TD, d), lambda i, pe: (i, 0))],
        out_specs=pl.BlockSpec(memory_space=pl.ANY),
        scratch_shapes=[pltpu.VMEM((_TMB, d), F32), pltpu.SemaphoreType.DMA(())],
    )
    return pl.pallas_call(
        _dispatch_kernel,
        grid_spec=grid_spec,
        out_shape=jax.ShapeDtypeStruct((n_slots, d), F32),
        compiler_params=_cparams(("arbitrary",)),
        name="dispatch",
    )(pends, dest_flat, u2)


def _expert_kernel(be_ref, nu_ref, xs_ref, w1_ref, b1_ref, w2_ref, b2_ref, y_ref, w1b_ref, w2b_ref):
    i = pl.program_id(0)
    f = w2_ref.shape[1]
    used = i < nu_ref[0]
    new_expert = jnp.logical_or(i == 0, be_ref[i] != be_ref[jnp.maximum(i - 1, 0)])

    @pl.when(jnp.logical_and(used, new_expert))
    def _():
        w1b_ref[...] = w1_ref[0].astype(BF16)
        w2b_ref[...] = w2_ref[0].astype(BF16)

    @pl.when(used)
    def _():
        h = jnp.dot(xs_ref[...].astype(BF16), w1b_ref[...], preferred_element_type=F32) + b1_ref[0]
        gate = jnp.minimum(h[:, :f], SWIGLU_LIMIT)
        up = jnp.clip(h[:, f:], -SWIGLU_LIMIT, SWIGLU_LIMIT)
        act = (up + 1.0) * gate * jax.nn.sigmoid(SWIGLU_ALPHA * gate)
        y_ref[...] = jnp.dot(act.astype(BF16), w2b_ref[...], preferred_element_type=F32) + b2_ref[0]

    @pl.when(i >= nu_ref[0])
    def _():
        y_ref[...] = jnp.zeros(y_ref.shape, F32)


def _expert_call(block_expert, n_used, xs, w1, b1, w2, b2):
    n_slots, d = xs.shape
    ne, _, f2 = w1.shape
    f = f2 // 2
    grid_spec = pltpu.PrefetchScalarGridSpec(
        num_scalar_prefetch=2,
        grid=(n_slots // _TMB,),
        in_specs=[pl.BlockSpec((_TMB, d), lambda i, be, nu: (i, 0)),
                  pl.BlockSpec((1, d, f2), lambda i, be, nu: (be[i], 0, 0)),
                  pl.BlockSpec((1, 1, f2), lambda i, be, nu: (be[i], 0, 0)),
                  pl.BlockSpec((1, f, d), lambda i, be, nu: (be[i], 0, 0)),
                  pl.BlockSpec((1, 1, d), lambda i, be, nu: (be[i], 0, 0))],
        out_specs=pl.BlockSpec((_TMB, d), lambda i, be, nu: (i, 0)),
        scratch_shapes=[pltpu.VMEM((d, f2), BF16), pltpu.VMEM((f, d), BF16)],
    )
    return pl.pallas_call(
        _expert_kernel,
        grid_spec=grid_spec,
        out_shape=jax.ShapeDtypeStruct((n_slots, d), F32),
        compiler_params=_cparams(("arbitrary",)),
        name="experts",
    )(block_expert, n_used, xs, w1, b1.reshape(ne, 1, f2), w2, b2.reshape(ne, 1, d))


_TC = 1024


def _combine_kernel(dest_ref, gate_ref, x1_ref, mod_ref, g2_ref, b2_ref, y_hbm, o_ref, buf_ref, sem, *, alpha):
    def issue(tb, carry):
        row0 = pl.multiple_of(tb * 8, 8)
        for r in range(8):
            for j in range(TOPK_EXPERTS):
                d = dest_ref[(row0 + r) * TOPK_EXPERTS + j]
                pltpu.make_async_copy(y_hbm.at[pl.ds(d, 1), :], buf_ref.at[j, pl.ds(row0 + r, 1), :], sem).start()
        return carry

    lax.fori_loop(0, _TC // 8, issue, 0)
    for j in range(TOPK_EXPERTS):
        pltpu.make_async_copy(y_hbm.at[pl.ds(0, _TC), :], buf_ref.at[j], sem).wait()
    gates = gate_ref[...]
    y = gates[:, 0:1] * buf_ref[0]
    for j in range(1, TOPK_EXPERTS):
        y = y + gates[:, j:j + 1] * buf_ref[j]
    o_ref[...] = _ln(alpha * x1_ref[...] + mod_ref[0, 5:6, :] * y) * g2_ref[...] + b2_ref[...]


def _combine_call(dest_flat, gates, x1, mod3, ln_g, ln_b, y, seq, alpha):
    n_tok, d = x1.shape
    tc = _TC
    row = lambda i: (i, 0)
    return pl.pallas_call(
        functools.partial(_combine_kernel, alpha=alpha),
        grid=(n_tok // tc,),
        in_specs=[pl.BlockSpec((tc * TOPK_EXPERTS,), lambda i: (i,), memory_space=pltpu.SMEM),
                  pl.BlockSpec((tc, TOPK_EXPERTS), row),
                  pl.BlockSpec((tc, d), row),
                  pl.BlockSpec((1, 6, d), lambda i: ((i * tc) // seq, 0, 0)),
                  _const_spec((1, d)), _const_spec((1, d)),
                  pl.BlockSpec(memory_space=pl.ANY)],
        out_specs=pl.BlockSpec((tc, d), row),
        out_shape=jax.ShapeDtypeStruct((n_tok, d), F32),
        scratch_shapes=[pltpu.VMEM((TOPK_EXPERTS, tc, d), F32), pltpu.SemaphoreType.DMA(())],
        compiler_params=_cparams(("arbitrary",)),
        name="combine",
    )(dest_flat, gates, x1, mod3, ln_g, ln_b, y)


def _permute_w_in(w):
    d = w.shape[0]
    s = np.cumsum([0, A_HEADS * A_LATENT, A_LATENT, IDX_HEADS * IDX_DIM, IDX_DIM, IDX_HEADS,
                   SSM_D_INNER, SSM_CONV_DIM, SSM_HEADS, d, d]).tolist()
    q, kv, iq, ik, iw, z, xbc, dt, ga, gb = [w[:, s[i]:s[i + 1]] for i in range(10)]
    pad1 = jnp.zeros((d, _SM_DT - _SM_IW - IDX_HEADS), w.dtype)
    pad2 = jnp.zeros((d, LANES - _SM_DT - SSM_HEADS), w.dtype)
    return jnp.concatenate([q, kv, iq, ik, iw, pad1, dt, pad2, z, xbc, ga, gb], axis=1).astype(BF16)


def _pad_lanes(v, fill=0.0):
    return jnp.pad(v.reshape(1, -1), ((0, 0), (0, LANES - v.shape[-1])), constant_values=fill)


def kernel(x, c, w_mod, b_mod, w_in, kv_norm_w, idx_k_norm_w, idx_k_norm_b, rel_bias, conv_w, conv_b, dt_bias,
           a_log, d_skip, ssm_norm_w, w_proj_a, w_proj_b, w_out, ln1_g, ln1_b, w_router, b_router, w1, b1, w2, b2,
           ln2_g, ln2_b):
    bsz, seq, d = x.shape
    depth = w_mod.shape[0]
    alpha = (2.0 * depth) ** 0.25
    n_tok = bsz * seq
    n_asg = n_tok * TOPK_EXPERTS
    n_blocks = n_asg // _TMB + N_EXPERTS
    n_slots = n_blocks * _TMB
    nblk = seq // _KB
    bias = _bias_tiles(rel_bias)
    x2 = x.reshape(n_tok, d)
    for l in range(depth):
        mod3 = _mod_call(c, w_mod[l], b_mod[l]).reshape(bsz, 6, d)
        q, kvn, iq, ik2, small, smt, z, xbc, g_a, g_b = _inproj_call(
            x2, mod3, _permute_w_in(w_in[l]), kv_norm_w[l].reshape(1, -1),
            _pad_lanes(idx_k_norm_w[l]), _pad_lanes(idx_k_norm_b[l]), seq)
        kvb = kvn.reshape(bsz, nblk, _KB, A_LATENT)
        o_a = _attn_call(q, iq, smt, ik2.reshape(2, bsz, nblk, _KB, LANES), kvb, bias, bsz, seq)
        o_b = _ssd_call(z, xbc, small, smt, conv_w[l], conv_b[l], dt_bias[l], a_log[l], d_skip[l], ssm_norm_w[l],
                        bsz, seq)
        x1, u2, route, gates, cnt = _merge_call(
            o_a, o_b, g_a, g_b, x2, mod3, w_proj_a[l].astype(BF16), w_proj_b[l].astype(BF16),
            w_out[l].astype(BF16), ln1_g[l].reshape(1, -1), ln1_b[l].reshape(1, -1), w_router[l], b_router[l],
            seq, alpha)
        counts = cnt[:, 0].astype(I32)
        padded = (counts + _TMB - 1) // _TMB * _TMB
        pends = jnp.cumsum(padded).astype(I32)
        pstarts = pends - padded
        eid = route[:TOPK_EXPERTS]
        onehot = eid[:, :, None] == jnp.arange(N_EXPERTS, dtype=I32)
        dest = jnp.sum(jnp.where(onehot, pstarts, 0), axis=-1) + route[TOPK_EXPERTS:2 * TOPK_EXPERTS]
        dest_flat = dest.T.reshape(n_asg)
        block_start = jnp.arange(n_blocks, dtype=I32) * _TMB
        block_expert = jnp.minimum(jnp.sum(block_start[:, None] >= pends[None, :], axis=1), N_EXPERTS - 1).astype(I32)
        n_used = (pends[-1:] // _TMB).astype(I32)
        xs = _dispatch_call(pends, dest_flat, u2, n_slots)
        y = _expert_call(block_expert, n_used, xs, w1[l], b1[l], w2[l], b2[l])
        x2 = _combine_call(dest_flat, gates[:TOPK_EXPERTS].T, x1, mod3, ln2_g[l].reshape(1, -1),
                           ln2_b[l].reshape(1, -1), y, seq, alpha)
    return x2.reshape(bsz, seq, d)
```

```python
import functools
import math

import jax
import jax.numpy as jnp
import numpy as np
from jax import lax
from jax.experimental import pallas as pl
from jax.experimental.pallas import tpu as pltpu

F32 = jnp.float32
BF16 = jnp.bfloat16
I32 = jnp.int32

CHUNK = 64
A_HEADS = 8
A_LATENT = 128
IDX_HEADS = 8
IDX_DIM = 64
TOPK_KEYS_MAX = 256
REL_BUCKETS = 32
REL_MAX_DIST = 128
SSM_D_INNER = 1024
SSM_HEADDIM = 64
SSM_HEADS = SSM_D_INNER // SSM_HEADDIM
SSM_GROUPS = 4
SSM_STATE = 128
SSM_CONV = 4
SSM_CONV_DIM = SSM_D_INNER + 2 * SSM_GROUPS * SSM_STATE
N_EXPERTS = 32
TOPK_EXPERTS = 4
SWIGLU_LIMIT = 7.0
SWIGLU_ALPHA = 1.702
LN_EPS = 1e-5

LANES = 128
INT_MIN = -2147483648
VMEM_LIMIT = 56 * 1024 * 1024

HI = lax.Precision.HIGHEST


def _cparams(sem):
    return pltpu.CompilerParams(dimension_semantics=sem, vmem_limit_bytes=VMEM_LIMIT)


def _ln(x):
    mu = jnp.mean(x, axis=-1, keepdims=True)
    xc = x - mu
    var = jnp.mean(xc * xc, axis=-1, keepdims=True)
    return xc * lax.rsqrt(var + LN_EPS)


def _const_spec(shape):
    nd = len(shape)
    return pl.BlockSpec(shape, lambda *_: (0,) * nd, pipeline_mode=pl.Buffered(1))


def _mod_kernel(c_ref, w_ref, b_ref, o_ref):
    c = c_ref[...]
    sc = c * jax.nn.sigmoid(c)
    o_ref[...] = jnp.dot(sc, w_ref[...], precision=HI, preferred_element_type=F32) + b_ref[...]


def _mod_call(c, w_mod, b_mod):
    bsz, d = c.shape
    n = w_mod.shape[1]
    tn = 1024
    return pl.pallas_call(
        _mod_kernel,
        grid=(n // tn,),
        in_specs=[pl.BlockSpec((bsz, d), lambda j: (0, 0)),
                  pl.BlockSpec((d, tn), lambda j: (0, j)),
                  pl.BlockSpec((1, tn), lambda j: (0, j))],
        out_specs=pl.BlockSpec((bsz, tn), lambda j: (0, j)),
        out_shape=jax.ShapeDtypeStruct((bsz, n), F32),
        compiler_params=_cparams(("arbitrary",)),
        name="mod",
    )(c, w_mod, b_mod.reshape(1, n))


_C_Q, _C_KV, _C_IQ, _C_SM, _C_Z, _C_XBC, _C_GA, _C_GB, _C_END = 0, 1024, 1152, 1664, 1792, 2816, 4864, 5888, 6912
_TI = 512
_SM_IW = IDX_DIM
_SM_DT = IDX_DIM + SSM_HEADS


def _inproj_kernel(x_ref, mod_ref, w_ref, kvw_ref, ikw_ref, ikb_ref,
                   q_ref, kv_ref, iq_ref, ik2_ref, sm_ref, smt_ref, z_ref, xbc_ref, ga_ref, gb_ref):
    u = _ln(x_ref[...]) * (1.0 + mod_ref[0, 1:2, :]) + mod_ref[0, 0:1, :]
    ub = u.astype(BF16)

    def mm(a, b):
        return jnp.dot(ub, w_ref[:, a:b], preferred_element_type=F32)

    q_ref[...] = mm(_C_Q, _C_KV).astype(BF16)
    kv = mm(_C_KV, _C_IQ)
    kv = kv * lax.rsqrt(jnp.mean(kv * kv, axis=-1, keepdims=True) + LN_EPS)
    kv_ref[...] = (kv * kvw_ref[...]).astype(BF16)
    iq_ref[...] = mm(_C_IQ, _C_SM).astype(BF16)
    g = mm(_C_SM, _C_Z)
    lane = lax.broadcasted_iota(I32, g.shape, 1)
    is_ik = lane < IDX_DIM
    mu = jnp.sum(jnp.where(is_ik, g, 0.0), axis=-1, keepdims=True) * (1.0 / IDX_DIM)
    gc = g - mu
    var = jnp.sum(jnp.where(is_ik, gc * gc, 0.0), axis=-1, keepdims=True) * (1.0 / IDX_DIM)
    ik = jnp.where(is_ik, gc * lax.rsqrt(var + LN_EPS) * ikw_ref[...] + ikb_ref[...], 0.0)
    ik2_ref[0] = ik.astype(BF16)
    ik2_ref[1] = pltpu.roll(ik, IDX_DIM, axis=1).astype(BF16)
    sm = jnp.where(lane < _SM_IW + IDX_HEADS, g * (IDX_HEADS ** -0.5), g)
    sm_ref[...] = sm
    smt_ref[0] = sm.T
    z_ref[...] = mm(_C_Z, _C_XBC)
    xbc_ref[...] = mm(_C_XBC, _C_GA)
    ga_ref[...] = mm(_C_GA, _C_GB)
    gb_ref[...] = mm(_C_GB, _C_END)


def _inproj_call(x2, mod3, w_perm, kvw, ikw, ikb, seq):
    n_tok, d = x2.shape
    tm = _TI
    row = lambda i: (i, 0)

    def ospec(n):
        return pl.BlockSpec((tm, n), row)

    spt = seq // tm
    sd = jax.ShapeDtypeStruct
    return pl.pallas_call(
        _inproj_kernel,
        grid=(n_tok // tm,),
        in_specs=[pl.BlockSpec((tm, d), row),
                  pl.BlockSpec((1, 6, d), lambda i: (i // spt, 0, 0)),
                  _const_spec(w_perm.shape), _const_spec((1, 128)), _const_spec((1, 128)), _const_spec((1, 128))],
        out_specs=[ospec(1024), ospec(128), ospec(512),
                   pl.BlockSpec((2, tm, LANES), lambda i: (0, i, 0)),
                   ospec(LANES),
                   pl.BlockSpec((1, LANES, tm), lambda i: (i // spt, 0, i % spt)),
                   ospec(1024), ospec(2048), ospec(1024), ospec(1024)],
        out_shape=[sd((n_tok, 1024), BF16), sd((n_tok, 128), BF16), sd((n_tok, 512), BF16),
                   sd((2, n_tok, LANES), BF16), sd((n_tok, LANES), F32), sd((n_tok // seq, LANES, seq), F32),
                   sd((n_tok, 1024), F32), sd((n_tok, 2048), F32), sd((n_tok, 1024), F32), sd((n_tok, 1024), F32)],
        compiler_params=_cparams(("arbitrary",)),
        name="inproj",
    )(x2, mod3, w_perm, kvw, ikw, ikb)


_TQ = 256
_KB = 256
_BIAS_STEP = math.gcd(_TQ, _KB)
_NEAR_DELTAS = tuple(range(-_KB, 1, _BIAS_STEP))
assert REL_MAX_DIST <= _BIAS_STEP
_PBITS = 15
_GUARDS = -2147450880


def _t5_bucket(rel):
    half = REL_BUCKETS // 2
    max_exact = half // 2
    ret = (rel > 0).astype(jnp.int32) * half
    n = jnp.abs(rel)
    nf = jnp.maximum(n, 1).astype(jnp.float32)
    large = max_exact + (jnp.log(nf / max_exact) / math.log(REL_MAX_DIST / max_exact)
                         * (half - max_exact)).astype(jnp.int32)
    large = jnp.minimum(large, half - 1)
    return ret + jnp.where(n < max_exact, n, large)


def _bias_tiles(rel_bias):
    i = jnp.arange(_TQ, dtype=jnp.int32)[:, None]
    c = jnp.arange(_KB, dtype=jnp.int32)[None, :]
    rel = [c + delta - i for delta in _NEAR_DELTAS] + [jnp.full((_TQ, _KB), -REL_MAX_DIST, jnp.int32)]
    bucket = _t5_bucket(jnp.stack(rel))
    b = jnp.zeros((len(rel), A_HEADS, _TQ, _KB), F32)
    for k in range(REL_BUCKETS):
        b = jnp.where((bucket == k)[:, None], rel_bias[k].astype(F32)[None, :, None, None], b)
    return (b * (A_LATENT ** 0.5)).reshape(len(rel), A_HEADS * _TQ, _KB)


def _attn_kernel(q_ref, iq_ref, iwt_ref, ik2_ref, kv_ref, bias_ref, o_ref,
                 keys_ref, pk_ref, pk2_ref, s_ref, mx_ref, l_ref, acc_ref, *, n_sel, jbits):
    tq, kb = _TQ, _KB
    qs = pl.program_id(1) * tq
    nkb = (qs + tq + kb - 1) // kb
    nt = (((1,), (1,)), ((), ()))

    qpos = lax.broadcasted_iota(I32, (1, tq), 1) + qs
    qend = (qpos // CHUNK + 1) * CHUNK
    krow = lax.broadcasted_iota(I32, (kb, tq), 0)
    iq = iq_ref[...]
    iwt = iwt_ref[0] * (IDX_DIM ** -0.5)

    def score_body(j, carry):
        acc = jnp.zeros((kb, tq), F32)
        for p in range(IDX_HEADS // 2):
            pair = iq[:, p * LANES:(p + 1) * LANES]
            for par in range(2):
                h = 2 * p + par
                s = lax.dot_general(ik2_ref[par, 0, j], pair, nt, preferred_element_type=F32)
                acc = acc + jnp.maximum(s, 0.0) * iwt[h:h + 1, :]
        bits = pltpu.bitcast(acc, I32)
        key = jnp.where(bits < 0, bits ^ 0x7FFFFFFF, bits)
        key = jnp.where(bits == INT_MIN, 0, key)
        key = jnp.where(krow + j * kb < qend, key, INT_MIN)
        keys_ref[j] = key
        top = (key >> (32 - _PBITS)) + (1 << (_PBITS - 1))
        pk_ref[j] = (top[:kb // 2] << 16) | top[kb // 2:] | _GUARDS
        return carry

    lax.fori_loop(0, nkb, score_body, 0)

    def count_packed(ref, cand):
        both = (cand << 16) | cand

        def body(j, acc):
            g = ((ref[j] - both) >> _PBITS) & 0x00010001
            parts = [g[8 * i:8 * i + 8] for i in range(kb // 16)]
            while len(parts) > 1:
                parts = [parts[i] + parts[i + 1] for i in range(0, len(parts), 2)]
            return acc + parts[0]
        acc = lax.fori_loop(0, nkb, body, jnp.zeros((8, tq), I32))
        return jnp.sum(((acc & 0xFFFF) + (acc >> 16)).astype(F32), axis=0, keepdims=True)

    def search_packed(ref, offset, n_at_zero):
        def body(it, carry):
            tu, n_tu = carry
            cu = tu | (jnp.int32(1) << (_PBITS - 1 - it))
            n_cu = offset + count_packed(ref, cu)
            ok = n_cu >= n_sel
            return jnp.where(ok, cu, tu), jnp.where(ok, n_cu, n_tu)
        return lax.fori_loop(0, _PBITS, body, (jnp.zeros((1, tq), I32), n_at_zero))

    def count(pred):
        def body(j, acc):
            m = jnp.where(pred(keys_ref[j], krow + j * kb), 1.0, 0.0)
            parts = [m[8 * i:8 * i + 8] for i in range(kb // 8)]
            while len(parts) > 1:
                parts = [parts[i] + parts[i + 1] for i in range(0, len(parts), 2)]
            return acc + parts[0]
        acc = lax.fori_loop(0, nkb, body, jnp.zeros((8, tq), F32))
        return jnp.sum(acc, axis=0, keepdims=True)

    n_all = jnp.full((1, tq), 1.0, F32) * (nkb * kb).astype(F32)
    top_bits, n_top = search_packed(pk_ref, 0.0, n_all)
    n_above = count_packed(pk_ref, top_bits + 1)

    def mid_body(j, carry):
        key = keys_ref[j]
        mid = jnp.where((key >> (32 - _PBITS)) + (1 << (_PBITS - 1)) == top_bits,
                        (key >> (32 - 2 * _PBITS)) & ((1 << _PBITS) - 1), 0)
        pk2_ref[j] = (mid[:kb // 2] << 16) | mid[kb // 2:] | _GUARDS
        return carry

    lax.fori_loop(0, nkb, mid_body, 0)
    mid_bits, n_mid = search_packed(pk2_ref, n_above, n_top)

    def bit_body(it, carry):
        tu, n_tu = carry
        cu = tu | (jnp.int32(1) << (31 - it))
        cs = cu ^ INT_MIN
        n_cu = count(lambda k, kidx: k >= cs)
        ok = n_cu >= n_sel
        return jnp.where(ok, cu, tu), jnp.where(ok, n_cu, n_tu)

    tu0 = (top_bits << (32 - _PBITS)) | (mid_bits << (32 - 2 * _PBITS))
    thr_u, n_ge = lax.fori_loop(2 * _PBITS, 32, bit_body, (tu0, n_mid))
    thr = thr_u ^ INT_MIN

    def tie_search():
        need = n_sel - count(lambda k, kidx: k > thr)

        def j_body(it, jj):
            cj = jj | (jnp.int32(1) << (jbits - 1 - it))
            f = count(lambda k, kidx: (k == thr) & (kidx < cj))
            return jnp.where(f <= need, cj, jj)

        return lax.fori_loop(0, jbits, j_body, jnp.zeros((1, tq), I32))

    jj = lax.cond(jnp.max(n_ge) > n_sel, tie_search, lambda: jnp.full((1, tq), (1 << jbits) - 1, I32))

    q = q_ref[...]
    q_all = jnp.concatenate([q[:, h * LANES:(h + 1) * LANES] for h in range(A_HEADS)], axis=0)
    eye = jnp.where(lax.broadcasted_iota(I32, (tq, tq), 0) == lax.broadcasted_iota(I32, (tq, tq), 1),
                    1.0, 0.0).astype(BF16)
    mx_ref[...] = jnp.full(mx_ref.shape, -jnp.inf, F32)
    exp2_scale = A_LATENT ** -0.5 * math.log2(math.e)

    def logit_body(j, carry):
        k = keys_ref[j]
        kidx = krow + j * kb
        selt = ((k > thr) | ((k == thr) & (kidx < jj))) & (kidx < qend)
        sel = lax.dot_general(eye, jnp.where(selt, 1.0, 0.0).astype(BF16), nt, preferred_element_type=F32) > 0.5
        s = lax.dot_general(q_all, kv_ref[0, j], nt, preferred_element_type=F32)
        v = (j * kb - qs + kb) // _BIAS_STEP
        s = s + bias_ref[jnp.where(v < 0, len(_NEAR_DELTAS), v)]
        s = jnp.concatenate([jnp.where(sel, s[h * tq:(h + 1) * tq], -jnp.inf) for h in range(A_HEADS)], axis=0)
        s_ref[j] = s
        mx = mx_ref[...]
        for c in range(kb // LANES):
            mx = jnp.maximum(mx, s[:, c * LANES:(c + 1) * LANES])
        mx_ref[...] = mx
        return carry

    lax.fori_loop(0, nkb, logit_body, 0)
    m = jnp.max(mx_ref[...], axis=1, keepdims=True)
    m = jnp.where(m == -jnp.inf, 0.0, m)
    mx_ref[...] = jnp.broadcast_to(m, mx_ref.shape)
    l_ref[...] = jnp.zeros(l_ref.shape, F32)
    acc_ref[...] = jnp.zeros(acc_ref.shape, F32)

    def pv_body(j, carry):
        mb = mx_ref[...]
        s = s_ref[j]
        ps = [jnp.exp2((s[:, c * LANES:(c + 1) * LANES] - mb) * exp2_scale) for c in range(kb // LANES)]
        lsum = l_ref[...]
        for pc in ps:
            lsum = lsum + pc
        l_ref[...] = lsum
        p = jnp.concatenate(ps, axis=1).astype(BF16)
        acc_ref[...] += jnp.dot(p, kv_ref[0, j], preferred_element_type=F32)
        return carry

    lax.fori_loop(0, nkb, pv_body, 0)
    o = acc_ref[...] / jnp.sum(l_ref[...], axis=1, keepdims=True)
    for h in range(A_HEADS):
        o_ref[:, h * LANES:(h + 1) * LANES] = o[h * tq:(h + 1) * tq].astype(o_ref.dtype)


def _attn_call(q, iq, iwt, ik2, kvb, bias, bsz, seq):
    tq, kb = _TQ, _KB
    nq = seq // tq
    nblk = seq // kb
    n_sel = min(TOPK_KEYS_MAX, seq // 4)
    jbits = int(seq).bit_length()
    row = lambda b, i: (b * nq + i, 0)
    kern = functools.partial(_attn_kernel, n_sel=float(n_sel), jbits=jbits)
    return pl.pallas_call(
        kern,
        grid=(bsz, nq),
        in_specs=[pl.BlockSpec((tq, A_HEADS * A_LATENT), row),
                  pl.BlockSpec((tq, IDX_HEADS * IDX_DIM), row),
                  pl.BlockSpec((1, IDX_HEADS, tq), lambda b, i: (b, _SM_IW // IDX_HEADS, i)),
                  pl.BlockSpec((2, 1, nblk, kb, LANES), lambda b, i: (0, b, 0, 0, 0)),
                  pl.BlockSpec((1, nblk, kb, A_LATENT), lambda b, i: (b, 0, 0, 0)),
                  _const_spec(bias.shape)],
        out_specs=pl.BlockSpec((tq, A_HEADS * A_LATENT), row),
        out_shape=jax.ShapeDtypeStruct((bsz * seq, A_HEADS * A_LATENT), BF16),
        scratch_shapes=[pltpu.VMEM((nblk, kb, tq), I32),
                        pltpu.VMEM((nblk, kb // 2, tq), I32),
                        pltpu.VMEM((nblk, kb // 2, tq), I32),
                        pltpu.VMEM((nblk, A_HEADS * tq, kb), F32),
                        pltpu.VMEM((A_HEADS * tq, LANES), F32),
                        pltpu.VMEM((A_HEADS * tq, LANES), F32),
                        pltpu.VMEM((A_HEADS * tq, A_LATENT), F32)],
        compiler_params=_cparams(("arbitrary", "arbitrary")),
        name="attention",
    )(q, iq, iwt, ik2, kvb, bias)


_SSD_L = 256
_PAIRS = SSM_HEADS // 2


def _ssd_kernel(z_ref, xbc_ref, sm_ref, dtt_ref, cw_ref, cb_ref, dtb_ref, dtbt_ref, al_ref, alt_ref,
                dsk_ref, nw_ref, o_ref, ext_ref, state_ref, y_ref):
    L = _SSD_L
    hd = SSM_HEADDIM

    @pl.when(pl.program_id(1) == 0)
    def _():
        ext_ref[0:8, :] = jnp.zeros((8, SSM_CONV_DIM), F32)
        state_ref[...] = jnp.zeros(state_ref.shape, F32)

    x = xbc_ref[...]
    ext_ref[8:8 + L, :] = x
    w = cw_ref[...]
    conv = x * w[3:4] + cb_ref[...]
    for k in range(1, SSM_CONV):
        conv = conv + ext_ref[8 - k:8 - k + L, :] * w[SSM_CONV - 1 - k:SSM_CONV - k]
    ext_ref[0:8, :] = x[L - 8:L]
    act = conv * jax.nn.sigmoid(conv)
    xs = act[:, :SSM_D_INNER]
    boff = SSM_D_INNER
    coff = SSM_D_INNER + SSM_GROUPS * SSM_STATE

    def softplus(v):
        return jnp.maximum(v, 0.0) + jnp.log1p(jnp.exp(-jnp.abs(v)))

    dt = softplus(sm_ref[:, _SM_DT:_SM_DT + SSM_HEADS] + dtb_ref[...])
    dtt = softplus(dtt_ref[0] + dtbt_ref[...])
    a_col = dt * (-jnp.exp(al_ref[...]))
    a_row = dtt * (-jnp.exp(alt_ref[...]))
    ri = lax.broadcasted_iota(I32, (L, L), 0)
    ci = lax.broadcasted_iota(I32, (L, L), 1)
    causal = ci <= ri
    cs_col = jnp.dot(jnp.where(causal, 1.0, 0.0), a_col, precision=HI, preferred_element_type=F32)
    cs_row = jnp.dot(a_row, jnp.where(ri <= ci, 1.0, 0.0), precision=HI, preferred_element_type=F32)
    cs_last = cs_col[L - 1:L, :]
    lane = lax.broadcasted_iota(I32, (1, LANES), 1)
    lo = lane < hd
    sub = lax.broadcasted_iota(I32, (LANES, 1), 0)

    for g in range(SSM_GROUPS):
        bm = act[:, boff + g * SSM_STATE: boff + (g + 1) * SSM_STATE].astype(BF16)
        cm = act[:, coff + g * SSM_STATE: coff + (g + 1) * SSM_STATE].astype(BF16)
        cb = lax.dot_general(cm, bm, (((1,), (1,)), ((), ())), preferred_element_type=F32)
        for pp in range(_PAIRS // SSM_GROUPS):
            p = g * (_PAIRS // SSM_GROUPS) + pp
            h0, h1 = 2 * p, 2 * p + 1
            xp = xs[:, p * LANES:(p + 1) * LANES]
            dtl = jnp.where(lo, dt[:, h0:h0 + 1], dt[:, h1:h1 + 1])
            xdt = xp * dtl
            csl = jnp.where(lo, cs_col[:, h0:h0 + 1], cs_col[:, h1:h1 + 1])
            last = jnp.where(lo, cs_last[:, h0:h0 + 1], cs_last[:, h1:h1 + 1])
            ydiag = jnp.zeros((L, LANES), F32)
            for h, msk in ((h0, lo), (h1, jnp.logical_not(lo))):
                seg = cs_col[:, h:h + 1] - cs_row[h:h + 1, :]
                gm = (cb * jnp.exp(jnp.where(causal, seg, -jnp.inf))).astype(BF16)
                ydiag = ydiag + jnp.dot(gm, jnp.where(msk, xdt, 0.0).astype(BF16), preferred_element_type=F32)
            prev = state_ref[p]
            yoff = lax.dot_general(cm, prev.astype(BF16), (((1,), (1,)), ((), ())), preferred_element_type=F32)
            y_ref[:, p * LANES:(p + 1) * LANES] = ydiag + yoff * jnp.exp(csl) + xp * dsk_ref[:, p * LANES:(p + 1) * LANES]
            wx = (xdt * jnp.exp(last - csl)).astype(BF16)
            st = lax.dot_general(wx, bm, (((0,), (0,)), ((), ())), preferred_element_type=F32)
            cdec = jnp.where(sub < hd, jnp.exp(cs_last[:, h0:h0 + 1]), jnp.exp(cs_last[:, h1:h1 + 1]))
            state_ref[p] = prev * cdec + st

    z = z_ref[...]
    y = y_ref[...] * (z * jax.nn.sigmoid(z))
    gw = SSM_D_INNER // SSM_GROUPS
    for g in range(SSM_GROUPS):
        yg = y[:, g * gw:(g + 1) * gw]
        yg = yg * lax.rsqrt(jnp.mean(yg * yg, axis=-1, keepdims=True) + LN_EPS)
        o_ref[:, g * gw:(g + 1) * gw] = (yg * nw_ref[:, g * gw:(g + 1) * gw]).astype(o_ref.dtype)


def _ssd_call(z, xbc, small, smt, conv_w, conv_b, dt_bias, a_log, d_skip, norm_w, bsz, seq):
    L = _SSD_L
    nc = seq // L
    row = lambda b, c: (b * nc + c, 0)
    h = SSM_HEADS
    return pl.pallas_call(
        _ssd_kernel,
        grid=(bsz, nc),
        in_specs=[pl.BlockSpec((L, SSM_D_INNER), row),
                  pl.BlockSpec((L, SSM_CONV_DIM), row),
                  pl.BlockSpec((L, LANES), row),
                  pl.BlockSpec((1, h, L), lambda b, c: (b, _SM_DT // h, c)),
                  _const_spec((SSM_CONV, SSM_CONV_DIM)), _const_spec((1, SSM_CONV_DIM)),
                  _const_spec((1, h)), _const_spec((h, 1)), _const_spec((1, h)), _const_spec((h, 1)),
                  _const_spec((1, SSM_D_INNER)), _const_spec((1, SSM_D_INNER))],
        out_specs=pl.BlockSpec((L, SSM_D_INNER), row),
        out_shape=jax.ShapeDtypeStruct((bsz * seq, SSM_D_INNER), BF16),
        scratch_shapes=[pltpu.VMEM((L + 8, SSM_CONV_DIM), F32),
                        pltpu.VMEM((_PAIRS, LANES, SSM_STATE), F32),
                        pltpu.VMEM((L, SSM_D_INNER), F32)],
        compiler_params=_cparams(("arbitrary", "arbitrary")),
        name="ssd",
    )(z, xbc, small, smt, conv_w, conv_b.reshape(1, -1), dt_bias.reshape(1, h), dt_bias.reshape(h, 1),
      a_log.reshape(1, h), a_log.reshape(h, 1), jnp.repeat(d_skip, SSM_HEADDIM).reshape(1, -1),
      norm_w.reshape(1, -1))


_TM = 512


def _merge_kernel(oa_ref, ob_ref, ga_ref, gb_ref, x_ref, mod_ref, wpa_ref, wpb_ref, wo_ref, g1_ref, b1_ref,
                  wrh_ref, wrl_ref, br_ref, x1_ref, u2_ref, route_ref, gate_ref, cnt_ref, base_ref, *, alpha):
    tm = _TM
    ne = N_EXPERTS

    @pl.when(pl.program_id(0) == 0)
    def _():
        base_ref[...] = jnp.zeros(base_ref.shape, F32)

    ma = jnp.dot(oa_ref[...], wpa_ref[...], preferred_element_type=F32)
    mb = jnp.dot(ob_ref[...], wpb_ref[...], preferred_element_type=F32)
    merged = jax.nn.sigmoid(ga_ref[...]) * ma + jax.nn.sigmoid(gb_ref[...]) * mb
    t = jnp.dot(merged.astype(BF16), wo_ref[...], preferred_element_type=F32)
    x1 = _ln(alpha * x_ref[...] + mod_ref[0, 2:3, :] * t) * g1_ref[...] + b1_ref[...]
    x1_ref[...] = x1
    u2 = _ln(x1) * (1.0 + mod_ref[0, 4:5, :]) + mod_ref[0, 3:4, :]
    u2_ref[...] = u2
    nt = (((1,), (1,)), ((), ()))
    uh = u2.astype(BF16)
    ul = (u2 - uh.astype(F32)).astype(BF16)
    wh, wl = wrh_ref[...], wrl_ref[...]
    logits = (lax.dot_general(wh, uh, nt, preferred_element_type=F32)
              + lax.dot_general(wl, uh, nt, preferred_element_type=F32)
              + lax.dot_general(wh, ul, nt, preferred_element_type=F32)) + br_ref[...]
    eio = lax.broadcasted_iota(I32, (ne, tm), 0).astype(F32)
    vals, ids = [], []
    for _ in range(TOPK_EXPERTS):
        m = jnp.max(logits, axis=0, keepdims=True)
        idx = jnp.min(jnp.where(logits == m, eio, float(ne)), axis=0, keepdims=True)
        vals.append(m)
        ids.append(idx)
        logits = jnp.where(eio == idx, -jnp.inf, logits)
    es = [jnp.exp(v - vals[0]) for v in vals]
    den = es[0] + es[1] + es[2] + es[3]

    onehot = jnp.zeros((ne, tm), F32)
    for idx in ids:
        onehot = onehot + jnp.where(eio == idx, 1.0, 0.0)
    ri = lax.broadcasted_iota(I32, (tm, tm), 0)
    ci = lax.broadcasted_iota(I32, (tm, tm), 1)
    before = jnp.where(ri < ci, 1.0, 0.0).astype(BF16)
    base = base_ref[...]
    prefix = jnp.dot(onehot.astype(BF16), before, preferred_element_type=F32) + base
    sub = lax.broadcasted_iota(I32, (8, tm), 0)
    route = jnp.zeros((8, tm), F32)
    gates = jnp.zeros((8, tm), F32)
    for j in range(TOPK_EXPERTS):
        rank = jnp.sum(jnp.where(eio == ids[j], prefix, 0.0), axis=0, keepdims=True)
        route = jnp.where(sub == j, ids[j], route)
        route = jnp.where(sub == TOPK_EXPERTS + j, rank, route)
        gates = jnp.where(sub == j, es[j] / den, gates)
    route_ref[...] = route.astype(I32)
    gate_ref[...] = gates
    base = base + jnp.sum(onehot, axis=1, keepdims=True)
    base_ref[...] = base
    cnt_ref[...] = jnp.broadcast_to(base, cnt_ref.shape)


def _merge_call(o_a, o_b, g_a, g_b, x2, mod3, wpa, wpb, wo, ln_g, ln_b, w_router, b_router, seq, alpha):
    n_tok, d = x2.shape
    tm = _TM
    ne = N_EXPERTS
    row = lambda i: (i, 0)
    blk = pl.BlockSpec((tm, d), row)
    sm = pl.BlockSpec((8, tm), lambda i: (0, i))
    wrt = w_router.T
    wrh = wrt.astype(BF16)
    wrl = (wrt - wrh.astype(F32)).astype(BF16)
    return pl.pallas_call(
        functools.partial(_merge_kernel, alpha=alpha),
        grid=(n_tok // tm,),
        in_specs=[blk, blk, blk, blk, blk,
                  pl.BlockSpec((1, 6, d), lambda i: ((i * tm) // seq, 0, 0)),
                  _const_spec((d, d)), _const_spec((d, d)), _const_spec((d, d)),
                  _const_spec((1, d)), _const_spec((1, d)), _const_spec((ne, d)), _const_spec((ne, d)),
                  _const_spec((ne, 1))],
        out_specs=[blk, blk, sm, sm, pl.BlockSpec((ne, LANES), lambda i: (0, 0))],
        out_shape=[jax.ShapeDtypeStruct((n_tok, d), F32), jax.ShapeDtypeStruct((n_tok, d), F32),
                   jax.ShapeDtypeStruct((8, n_tok), I32), jax.ShapeDtypeStruct((8, n_tok), F32),
                   jax.ShapeDtypeStruct((ne, LANES), F32)],
        scratch_shapes=[pltpu.VMEM((ne, 1), F32)],
        compiler_params=_cparams(("arbitrary",)),
        name="merge",
    )(o_a, o_b, g_a, g_b, x2, mod3, wpa, wpb, wo, ln_g, ln_b, wrh, wrl, b_router.reshape(ne, 1))


_TD = 4096
_TMB = 512


def _dispatch_kernel(pend_ref, dest_ref, u2_ref, xs_ref, zero_ref, sem):
    @pl.when(pl.program_id(0) == 0)
    def _():
        zero_ref[...] = jnp.zeros(zero_ref.shape, F32)
        for e in range(N_EXPERTS):
            end = pend_ref[e]
            start = pend_ref[e - 1] if e else 0

            @pl.when(end > start)
            def _():
                dst = xs_ref.at[pl.ds(pl.multiple_of(end - _TMB, _TMB), _TMB), :]
                cp = pltpu.make_async_copy(zero_ref, dst, sem)
                cp.start()
                cp.wait()

        def zero_unused(b, carry):
            cp = pltpu.make_async_copy(zero_ref, xs_ref.at[pl.ds(pl.multiple_of(b * _TMB, _TMB), _TMB), :], sem)
            cp.start()
            cp.wait()
            return carry

        lax.fori_loop(pend_ref[N_EXPERTS - 1] // _TMB, xs_ref.shape[0] // _TMB, zero_unused, 0)

    def issue(t, carry):
        for j in range(TOPK_EXPERTS):
            d = dest_ref[t * TOPK_EXPERTS + j]
            pltpu.make_async_copy(u2_ref.at[pl.ds(t, 1), :], xs_ref.at[pl.ds(d, 1), :], sem).start()
        return carry

    lax.fori_loop(0, _TD, issue, 0)
    for _ in range(TOPK_EXPERTS):
        pltpu.make_async_copy(u2_ref, xs_ref.at[pl.ds(0, _TD), :], sem).wait()


def _dispatch_call(pends, dest_flat, u2, n_slots):
    n_tok, d = u2.shape
    grid_spec = pltpu.PrefetchScalarGridSpec(
        num_scalar_prefetch=1,
        grid=(n_tok // _TD,),
        in_specs=[pl.BlockSpec((_TD * TOPK_EXPERTS,), lambda i, pe: (i,), memory_space=pltpu.SMEM),
                  pl.BlockSpec((_TD, d), lambda i, pe: (i, 0))],
        out_specs=pl.BlockSpec(memory_space=pl.ANY),
        scratch_shapes=[pltpu.VMEM((_TMB, d), F32), pltpu.SemaphoreType.DMA(())],
    )
    return pl.pallas_call(
        _dispatch_kernel,
        grid_spec=grid_spec,
        out_shape=jax.ShapeDtypeStruct((n_slots, d), F32),
        compiler_params=_cparams(("arbitrary",)),
        name="dispatch",
    )(pends, dest_flat, u2)


def _expert_kernel(be_ref, nu_ref, xs_ref, w1_ref, b1_ref, w2_ref, b2_ref, y_ref, w1b_ref, w2b_ref):
    i = pl.program_id(0)
    f = w2_ref.shape[1]
    used = i < nu_ref[0]
    new_expert = jnp.logical_or(i == 0, be_ref[i] != be_ref[jnp.maximum(i - 1, 0)])

    @pl.when(jnp.logical_and(used, new_expert))
    def _():
        w1b_ref[...] = w1_ref[0].astype(BF16)
        w2b_ref[...] = w2_ref[0].astype(BF16)

    @pl.when(used)
    def _():
        h = jnp.dot(xs_ref[...].astype(BF16), w1b_ref[...], preferred_element_type=F32) + b1_ref[0]
        gate = jnp.minimum(h[:, :f], SWIGLU_LIMIT)
        up = jnp.clip(h[:, f:], -SWIGLU_LIMIT, SWIGLU_LIMIT)
        act = (up + 1.0) * gate * jax.nn.sigmoid(SWIGLU_ALPHA * gate)
        y_ref[...] = jnp.dot(act.astype(BF16), w2b_ref[...], preferred_element_type=F32) + b2_ref[0]

    @pl.when(i >= nu_ref[0])
    def _():
        y_ref[...] = jnp.zeros(y_ref.shape, F32)


def _expert_call(block_expert, n_used, xs, w1, b1, w2, b2):
    n_slots, d = xs.shape
    ne, _, f2 = w1.shape
    f = f2 // 2
    grid_spec = pltpu.PrefetchScalarGridSpec(
        num_scalar_prefetch=2,
        grid=(n_slots // _TMB,),
        in_specs=[pl.BlockSpec((_TMB, d), lambda i, be, nu: (i, 0)),
                  pl.BlockSpec((1, d, f2), lambda i, be, nu: (be[i], 0, 0)),
                  pl.BlockSpec((1, 1, f2), lambda i, be, nu: (be[i], 0, 0)),
                  pl.BlockSpec((1, f, d), lambda i, be, nu: (be[i], 0, 0)),
                  pl.BlockSpec((1, 1, d), lambda i, be, nu: (be[i], 0, 0))],
        out_specs=pl.BlockSpec((_TMB, d), lambda i, be, nu: (i, 0)),
        scratch_shapes=[pltpu.VMEM((d, f2), BF16), pltpu.VMEM((f, d), BF16)],
    )
    return pl.pallas_call(
        _expert_kernel,
        grid_spec=grid_spec,
        out_shape=jax.ShapeDtypeStruct((n_slots, d), F32),
        compiler_params=_cparams(("arbitrary",)),
        name="experts",
    )(block_expert, n_used, xs, w1, b1.reshape(ne, 1, f2), w2, b2.reshape(ne, 1, d))


_TC = 1024


def _combine_kernel(dest_ref, gate_ref, x1_ref, mod_ref, g2_ref, b2_ref, y_hbm, o_ref, buf_ref, sem, *, alpha):
    def issue(tb, carry):
        row0 = pl.multiple_of(tb * 8, 8)
        for r in range(8):
            for j in range(TOPK_EXPERTS):
                d = dest_ref[(row0 + r) * TOPK_EXPERTS + j]
                pltpu.make_async_copy(y_hbm.at[pl.ds(d, 1), :], buf_ref.at[j, pl.ds(row0 + r, 1), :], sem).start()
        return carry

    lax.fori_loop(0, _TC // 8, issue, 0)
    for j in range(TOPK_EXPERTS):
        pltpu.make_async_copy(y_hbm.at[pl.ds(0, _TC), :], buf_ref.at[j], sem).wait()
    gates = gate_ref[...]
    y = gates[:, 0:1] * buf_ref[0]
    for j in range(1, TOPK_EXPERTS):
        y = y + gates[:, j:j + 1] * buf_ref[j]
    o_ref[...] = _ln(alpha * x1_ref[...] + mod_ref[0, 5:6, :] * y) * g2_ref[...] + b2_ref[...]


def _combine_call(dest_flat, gates, x1, mod3, ln_g, ln_b, y, seq, alpha):
    n_tok, d = x1.shape
    tc = _TC
    row = lambda i: (i, 0)
    return pl.pallas_call(
        functools.partial(_combine_kernel, alpha=alpha),
        grid=(n_tok // tc,),
        in_specs=[pl.BlockSpec((tc * TOPK_EXPERTS,), lambda i: (i,), memory_space=pltpu.SMEM),
                  pl.BlockSpec((tc, TOPK_EXPERTS), row),
                  pl.BlockSpec((tc, d), row),
                  pl.BlockSpec((1, 6, d), lambda i: ((i * tc) // seq, 0, 0)),
                  _const_spec((1, d)), _const_spec((1, d)),
                  pl.BlockSpec(memory_space=pl.ANY)],
        out_specs=pl.BlockSpec((tc, d), row),
        out_shape=jax.ShapeDtypeStruct((n_tok, d), F32),
        scratch_shapes=[pltpu.VMEM((TOPK_EXPERTS, tc, d), F32), pltpu.SemaphoreType.DMA(())],
        compiler_params=_cparams(("arbitrary",)),
        name="combine",
    )(dest_flat, gates, x1, mod3, ln_g, ln_b, y)


def _permute_w_in(w):
    d = w.shape[0]
    s = np.cumsum([0, A_HEADS * A_LATENT, A_LATENT, IDX_HEADS * IDX_DIM, IDX_DIM, IDX_HEADS,
                   SSM_D_INNER, SSM_CONV_DIM, SSM_HEADS, d, d]).tolist()
    q, kv, iq, ik, iw, z, xbc, dt, ga, gb = [w[:, s[i]:s[i + 1]] for i in range(10)]
    pad1 = jnp.zeros((d, _SM_DT - _SM_IW - IDX_HEADS), w.dtype)
    pad2 = jnp.zeros((d, LANES - _SM_DT - SSM_HEADS), w.dtype)
    return jnp.concatenate([q, kv, iq, ik, iw, pad1, dt, pad2, z, xbc, ga, gb], axis=1).astype(BF16)


def _pad_lanes(v, fill=0.0):
    return jnp.pad(v.reshape(1, -1), ((0, 0), (0, LANES - v.shape[-1])), constant_values=fill)


def kernel(x, c, w_mod, b_mod, w_in, kv_norm_w, idx_k_norm_w, idx_k_norm_b, rel_bias, conv_w, conv_b, dt_bias,
           a_log, d_skip, ssm_norm_w, w_proj_a, w_proj_b, w_out, ln1_g, ln1_b, w_router, b_router, w1, b1, w2, b2,
           ln2_g, ln2_b):
    bsz, seq, d = x.shape
    depth = w_mod.shape[0]
    alpha = (2.0 * depth) ** 0.25
    n_tok = bsz * seq
    assert seq % max(_TQ, _KB, _SSD_L, _TI, _TM, _TC) == 0 and n_tok % _TD == 0, (bsz, seq)
    n_asg = n_tok * TOPK_EXPERTS
    n_blocks = n_asg // _TMB + N_EXPERTS
    n_slots = n_blocks * _TMB
    nblk = seq // _KB
    bias = _bias_tiles(rel_bias)
    x2 = x.reshape(n_tok, d)
    for l in range(depth):
        mod3 = _mod_call(c, w_mod[l], b_mod[l]).reshape(bsz, 6, d)
        q, kvn, iq, ik2, small, smt, z, xbc, g_a, g_b = _inproj_call(
            x2, mod3, _permute_w_in(w_in[l]), kv_norm_w[l].reshape(1, -1),
            _pad_lanes(idx_k_norm_w[l]), _pad_lanes(idx_k_norm_b[l]), seq)
        kvb = kvn.reshape(bsz, nblk, _KB, A_LATENT)
        o_a = _attn_call(q, iq, smt, ik2.reshape(2, bsz, nblk, _KB, LANES), kvb, bias, bsz, seq)
        o_b = _ssd_call(z, xbc, small, smt, conv_w[l], conv_b[l], dt_bias[l], a_log[l], d_skip[l], ssm_norm_w[l],
                        bsz, seq)
        x1, u2, route, gates, cnt = _merge_call(
            o_a, o_b, g_a, g_b, x2, mod3, w_proj_a[l].astype(BF16), w_proj_b[l].astype(BF16),
            w_out[l].astype(BF16), ln1_g[l].reshape(1, -1), ln1_b[l].reshape(1, -1), w_router[l], b_router[l],
            seq, alpha)
        counts = cnt[:, 0].astype(I32)
        padded = (counts + _TMB - 1) // _TMB * _TMB
        pends = jnp.cumsum(padded).astype(I32)
        pstarts = pends - padded
        eid = route[:TOPK_EXPERTS]
        onehot = eid[:, :, None] == jnp.arange(N_EXPERTS, dtype=I32)
        dest = jnp.sum(jnp.where(onehot, pstarts, 0), axis=-1) + route[TOPK_EXPERTS:2 * TOPK_EXPERTS]
        dest_flat = dest.T.reshape(n_asg)
        block_start = jnp.arange(n_blocks, dtype=I32) * _TMB
        block_expert = jnp.minimum(jnp.sum(block_start[:, None] >= pends[None, :], axis=1), N_EXPERTS - 1).astype(I32)
        n_used = (pends[-1:] // _TMB).astype(I32)
        xs = _dispatch_call(pends, dest_flat, u2, n_slots)
        y = _expert_call(block_expert, n_used, xs, w1[l], b1[l], w2[l], b2[l])
        x2 = _combine_call(dest_flat, gates[:TOPK_EXPERTS].T, x1, mod3, ln2_g[l].reshape(1, -1),
                           ln2_b[l].reshape(1, -1), y, seq, alpha)
    return x2.reshape(bsz, seq, d)
```

```python
import functools
import math

import jax
import jax.numpy as jnp
import numpy as np
from jax import lax
from jax.experimental import pallas as pl
from jax.experimental.pallas import tpu as pltpu

F32 = jnp.float32
BF16 = jnp.bfloat16
I32 = jnp.int32

CHUNK = 64
A_HEADS = 8
A_LATENT = 128
IDX_HEADS = 8
IDX_DIM = 64
TOPK_KEYS_MAX = 256
REL_BUCKETS = 32
REL_MAX_DIST = 128
SSM_D_INNER = 1024
SSM_HEADDIM = 64
SSM_HEADS = SSM_D_INNER // SSM_HEADDIM
SSM_GROUPS = 4
SSM_STATE = 128
SSM_CONV = 4
SSM_CONV_DIM = SSM_D_INNER + 2 * SSM_GROUPS * SSM_STATE
N_EXPERTS = 32
TOPK_EXPERTS = 4
SWIGLU_LIMIT = 7.0
SWIGLU_ALPHA = 1.702
LN_EPS = 1e-5

LANES = 128
SUBLANES = 8
INT_MIN = -2147483648
VMEM_LIMIT = 56 * 1024 * 1024

HI = lax.Precision.HIGHEST


def _cparams(sem):
    return pltpu.CompilerParams(dimension_semantics=sem, vmem_limit_bytes=VMEM_LIMIT)


def _ln(x):
    mu = jnp.mean(x, axis=-1, keepdims=True)
    xc = x - mu
    var = jnp.mean(xc * xc, axis=-1, keepdims=True)
    return xc * lax.rsqrt(var + LN_EPS)


def _const_spec(shape):
    nd = len(shape)
    return pl.BlockSpec(shape, lambda *_: (0,) * nd, pipeline_mode=pl.Buffered(1))


def _mod_kernel(c_ref, w_ref, b_ref, o_ref):
    c = c_ref[...]
    sc = c * jax.nn.sigmoid(c)
    o_ref[...] = jnp.dot(sc, w_ref[...], precision=HI, preferred_element_type=F32) + b_ref[...]


def _mod_call(c, w_mod, b_mod):
    bsz, d = c.shape
    n = w_mod.shape[1]
    tn = 1024
    return pl.pallas_call(
        _mod_kernel,
        grid=(n // tn,),
        in_specs=[pl.BlockSpec((bsz, d), lambda j: (0, 0)),
                  pl.BlockSpec((d, tn), lambda j: (0, j)),
                  pl.BlockSpec((1, tn), lambda j: (0, j))],
        out_specs=pl.BlockSpec((bsz, tn), lambda j: (0, j)),
        out_shape=jax.ShapeDtypeStruct((bsz, n), F32),
        compiler_params=_cparams(("arbitrary",)),
        name="mod",
    )(c, w_mod, b_mod.reshape(1, n))


D_MODEL = 1024
_W_Q, _W_KV, _W_IQ = A_HEADS * A_LATENT, A_LATENT, IDX_HEADS * IDX_DIM
_GROUP_WIDTHS = (_W_Q, _W_KV, _W_IQ, LANES, SSM_D_INNER, SSM_CONV_DIM, D_MODEL, D_MODEL)
_C_Q, _C_KV, _C_IQ, _C_SM, _C_Z, _C_XBC, _C_GA, _C_GB, _C_END = np.cumsum((0,) + _GROUP_WIDTHS).tolist()
_TI = 512
_SM_IW = IDX_DIM
_SM_DT = IDX_DIM + SSM_HEADS


def _inproj_kernel(x_ref, mod_ref, w_ref, kvw_ref, ikw_ref, ikb_ref,
                   q_ref, kv_ref, iq_ref, ik2_ref, sm_ref, smt_ref, z_ref, xbc_ref, ga_ref, gb_ref):
    u = _ln(x_ref[...]) * (1.0 + mod_ref[0, 1:2, :]) + mod_ref[0, 0:1, :]
    ub = u.astype(BF16)

    def mm(a, b):
        return jnp.dot(ub, w_ref[:, a:b], preferred_element_type=F32)

    q_ref[...] = mm(_C_Q, _C_KV).astype(BF16)
    kv = mm(_C_KV, _C_IQ)
    kv = kv * lax.rsqrt(jnp.mean(kv * kv, axis=-1, keepdims=True) + LN_EPS)
    kv_ref[...] = (kv * kvw_ref[...]).astype(BF16)
    iq_ref[...] = mm(_C_IQ, _C_SM).astype(BF16)
    g = mm(_C_SM, _C_Z)
    lane = lax.broadcasted_iota(I32, g.shape, 1)
    is_ik = lane < IDX_DIM
    mu = jnp.sum(jnp.where(is_ik, g, 0.0), axis=-1, keepdims=True) * (1.0 / IDX_DIM)
    gc = g - mu
    var = jnp.sum(jnp.where(is_ik, gc * gc, 0.0), axis=-1, keepdims=True) * (1.0 / IDX_DIM)
    ik = jnp.where(is_ik, gc * lax.rsqrt(var + LN_EPS) * ikw_ref[...] + ikb_ref[...], 0.0)
    ik2_ref[0] = ik.astype(BF16)
    ik2_ref[1] = pltpu.roll(ik, IDX_DIM, axis=1).astype(BF16)
    sm = jnp.where(lane < _SM_IW + IDX_HEADS, g * (IDX_HEADS ** -0.5), g)
    sm_ref[...] = sm
    smt_ref[0] = sm.T
    z_ref[...] = mm(_C_Z, _C_XBC)
    xbc_ref[...] = mm(_C_XBC, _C_GA)
    ga_ref[...] = mm(_C_GA, _C_GB)
    gb_ref[...] = mm(_C_GB, _C_END)


def _inproj_call(x2, mod3, w_perm, kvw, ikw, ikb, seq):
    n_tok, d = x2.shape
    tm = _TI
    row = lambda i: (i, 0)

    def ospec(n):
        return pl.BlockSpec((tm, n), row)

    spt = seq // tm
    sd = jax.ShapeDtypeStruct
    return pl.pallas_call(
        _inproj_kernel,
        grid=(n_tok // tm,),
        in_specs=[pl.BlockSpec((tm, d), row),
                  pl.BlockSpec((1, 6, d), lambda i: (i // spt, 0, 0)),
                  _const_spec(w_perm.shape), _const_spec((1, _W_KV)), _const_spec((1, LANES)), _const_spec((1, LANES))],
        out_specs=[ospec(_W_Q), ospec(_W_KV), ospec(_W_IQ),
                   pl.BlockSpec((2, tm, LANES), lambda i: (0, i, 0)),
                   ospec(LANES),
                   pl.BlockSpec((1, LANES, tm), lambda i: (i // spt, 0, i % spt)),
                   ospec(SSM_D_INNER), ospec(SSM_CONV_DIM), ospec(D_MODEL), ospec(D_MODEL)],
        out_shape=[sd((n_tok, _W_Q), BF16), sd((n_tok, _W_KV), BF16), sd((n_tok, _W_IQ), BF16),
                   sd((2, n_tok, LANES), BF16), sd((n_tok, LANES), F32), sd((n_tok // seq, LANES, seq), F32),
                   sd((n_tok, SSM_D_INNER), F32), sd((n_tok, SSM_CONV_DIM), F32),
                   sd((n_tok, D_MODEL), F32), sd((n_tok, D_MODEL), F32)],
        compiler_params=_cparams(("arbitrary",)),
        name="inproj",
    )(x2, mod3, w_perm, kvw, ikw, ikb)


_TQ = 256
_KB = 256
_BIAS_STEP = math.gcd(_TQ, _KB)
_NEAR_DELTAS = tuple(range(-_KB, 1, _BIAS_STEP))
assert REL_MAX_DIST <= _BIAS_STEP
_HALF = 16
_PBITS = _HALF - 1
_FIELD_ONES = (1 << _HALF) | 1
_GUARDS = (_FIELD_ONES << _PBITS) - (1 << 32)


def _t5_bucket(rel):
    half = REL_BUCKETS // 2
    max_exact = half // 2
    ret = (rel > 0).astype(jnp.int32) * half
    n = jnp.abs(rel)
    nf = jnp.maximum(n, 1).astype(jnp.float32)
    large = max_exact + (jnp.log(nf / max_exact) / math.log(REL_MAX_DIST / max_exact)
                         * (half - max_exact)).astype(jnp.int32)
    large = jnp.minimum(large, half - 1)
    return ret + jnp.where(n < max_exact, n, large)


def _bias_tiles(rel_bias):
    i = jnp.arange(_TQ, dtype=jnp.int32)[:, None]
    c = jnp.arange(_KB, dtype=jnp.int32)[None, :]
    rel = [c + delta - i for delta in _NEAR_DELTAS] + [jnp.full((_TQ, _KB), -REL_MAX_DIST, jnp.int32)]
    bucket = _t5_bucket(jnp.stack(rel))
    b = jnp.zeros((len(rel), A_HEADS, _TQ, _KB), F32)
    for k in range(REL_BUCKETS):
        b = jnp.where((bucket == k)[:, None], rel_bias[k].astype(F32)[None, :, None, None], b)
    return (b * (A_LATENT ** 0.5)).reshape(len(rel), A_HEADS * _TQ, _KB)


def _attn_kernel(q_ref, iq_ref, iwt_ref, ik2_ref, kv_ref, bias_ref, o_ref,
                 keys_ref, pk_ref, pk2_ref, s_ref, mx_ref, l_ref, acc_ref, *, n_sel, jbits):
    tq, kb = _TQ, _KB
    qs = pl.program_id(1) * tq
    nkb = (qs + tq + kb - 1) // kb
    nt = (((1,), (1,)), ((), ()))

    qpos = lax.broadcasted_iota(I32, (1, tq), 1) + qs
    qend = (qpos // CHUNK + 1) * CHUNK
    krow = lax.broadcasted_iota(I32, (kb, tq), 0)
    iq = iq_ref[...]
    iwt = iwt_ref[0] * (IDX_DIM ** -0.5)

    def score_body(j, carry):
        acc = jnp.zeros((kb, tq), F32)
        for p in range(IDX_HEADS // 2):
            pair = iq[:, p * LANES:(p + 1) * LANES]
            for par in range(2):
                h = 2 * p + par
                s = lax.dot_general(ik2_ref[par, 0, j], pair, nt, preferred_element_type=F32)
                acc = acc + jnp.maximum(s, 0.0) * iwt[h:h + 1, :]
        bits = pltpu.bitcast(acc, I32)
        key = jnp.where(bits < 0, bits ^ 0x7FFFFFFF, bits)
        key = jnp.where(bits == INT_MIN, 0, key)
        key = jnp.where(krow + j * kb < qend, key, INT_MIN)
        keys_ref[j] = key
        top = (key >> (32 - _PBITS)) + (1 << (_PBITS - 1))
        pk_ref[j] = (top[:kb // 2] << _HALF) | top[kb // 2:] | _GUARDS
        return carry

    lax.fori_loop(0, nkb, score_body, 0)

    def select(nk):
        def over_blocks(body, init):
            for j in range(nk):
                init = body(j, init)
            return init

        def count_packed(ref, cand):
            both = (cand << _HALF) | cand

            def body(j, acc):
                g = ((ref[j] - both) >> _PBITS) & _FIELD_ONES
                parts = [g[SUBLANES * i:SUBLANES * (i + 1)] for i in range(kb // 2 // SUBLANES)]
                while len(parts) > 1:
                    parts = [parts[i] + parts[i + 1] for i in range(0, len(parts), 2)]
                return acc + parts[0]
            acc = over_blocks(body, jnp.zeros((SUBLANES, tq), I32))
            return jnp.sum(((acc & ((1 << _HALF) - 1)) + (acc >> _HALF)).astype(F32), axis=0, keepdims=True)

        def search_packed(ref, offset, n_at_zero):
            def body(it, carry):
                tu, n_tu = carry
                cu = tu | (jnp.int32(1) << (_PBITS - 1 - it))
                n_cu = offset + count_packed(ref, cu)
                ok = n_cu >= n_sel
                return jnp.where(ok, cu, tu), jnp.where(ok, n_cu, n_tu)
            return lax.fori_loop(0, _PBITS, body, (jnp.zeros((1, tq), I32), n_at_zero))

        def count(pred):
            def body(j, acc):
                m = jnp.where(pred(keys_ref[j], krow + j * kb), 1.0, 0.0)
                parts = [m[SUBLANES * i:SUBLANES * (i + 1)] for i in range(kb // SUBLANES)]
                while len(parts) > 1:
                    parts = [parts[i] + parts[i + 1] for i in range(0, len(parts), 2)]
                return acc + parts[0]
            acc = over_blocks(body, jnp.zeros((SUBLANES, tq), F32))
            return jnp.sum(acc, axis=0, keepdims=True)

        n_all = jnp.full((1, tq), float(nk * kb), F32)
        top_bits, n_top = search_packed(pk_ref, 0.0, n_all)
        n_above = count_packed(pk_ref, top_bits + 1)

        def mid_body(j, carry):
            key = keys_ref[j]
            mid = jnp.where((key >> (32 - _PBITS)) + (1 << (_PBITS - 1)) == top_bits,
                            (key >> (32 - 2 * _PBITS)) & ((1 << _PBITS) - 1), 0)
            pk2_ref[j] = (mid[:kb // 2] << _HALF) | mid[kb // 2:] | _GUARDS
            return carry

        over_blocks(mid_body, 0)
        mid_bits, n_mid = search_packed(pk2_ref, n_above, n_top)

        def bit_body(it, carry):
            tu, n_tu = carry
            cu = tu | (jnp.int32(1) << (31 - it))
            cs = cu ^ INT_MIN
            n_cu = count(lambda k, kidx: k >= cs)
            ok = n_cu >= n_sel
            return jnp.where(ok, cu, tu), jnp.where(ok, n_cu, n_tu)

        tu0 = (top_bits << (32 - _PBITS)) | (mid_bits << (32 - 2 * _PBITS))
        thr_u, n_ge = lax.fori_loop(2 * _PBITS, 32, bit_body, (tu0, n_mid))
        thr = thr_u ^ INT_MIN

        def tie_search():
            need = n_sel - count(lambda k, kidx: k > thr)

            def j_body(it, jj):
                cj = jj | (jnp.int32(1) << (jbits - 1 - it))
                f = count(lambda k, kidx: (k == thr) & (kidx < cj))
                return jnp.where(f <= need, cj, jj)

            return lax.fori_loop(0, jbits, j_body, jnp.zeros((1, tq), I32))

        jj = lax.cond(jnp.max(n_ge) > n_sel, tie_search, lambda: jnp.full((1, tq), (1 << jbits) - 1, I32))
        return thr, jj

    thr, jj = lax.switch(nkb - 1, [functools.partial(select, n) for n in range(1, keys_ref.shape[0] + 1)])


    q = q_ref[...]
    q_all = jnp.concatenate([q[:, h * LANES:(h + 1) * LANES] for h in range(A_HEADS)], axis=0)
    eye = jnp.where(lax.broadcasted_iota(I32, (tq, tq), 0) == lax.broadcasted_iota(I32, (tq, tq), 1),
                    1.0, 0.0).astype(BF16)
    mx_ref[...] = jnp.full(mx_ref.shape, -jnp.inf, F32)
    exp2_scale = A_LATENT ** -0.5 * math.log2(math.e)

    def logit_body(j, carry):
        k = keys_ref[j]
        kidx = krow + j * kb
        selt = ((k > thr) | ((k == thr) & (kidx < jj))) & (kidx < qend)
        sel = lax.dot_general(eye, jnp.where(selt, 1.0, 0.0).astype(BF16), nt, preferred_element_type=F32) > 0.5
        s = lax.dot_general(q_all, kv_ref[0, j], nt, preferred_element_type=F32)
        v = (j * kb - qs + kb) // _BIAS_STEP
        s = s + bias_ref[jnp.where(v < 0, len(_NEAR_DELTAS), v)]
        s = jnp.concatenate([jnp.where(sel, s[h * tq:(h + 1) * tq], -jnp.inf) for h in range(A_HEADS)], axis=0)
        s_ref[j] = s
        mx = mx_ref[...]
        for c in range(kb // LANES):
            mx = jnp.maximum(mx, s[:, c * LANES:(c + 1) * LANES])
        mx_ref[...] = mx
        return carry

    lax.fori_loop(0, nkb, logit_body, 0)
    m = jnp.max(mx_ref[...], axis=1, keepdims=True)
    m = jnp.where(m == -jnp.inf, 0.0, m)
    mx_ref[...] = jnp.broadcast_to(m, mx_ref.shape)
    l_ref[...] = jnp.zeros(l_ref.shape, F32)
    acc_ref[...] = jnp.zeros(acc_ref.shape, F32)

    def pv_body(j, carry):
        mb = mx_ref[...]
        s = s_ref[j]
        ps = [jnp.exp2((s[:, c * LANES:(c + 1) * LANES] - mb) * exp2_scale) for c in range(kb // LANES)]
        lsum = l_ref[...]
        for pc in ps:
            lsum = lsum + pc
        l_ref[...] = lsum
        p = jnp.concatenate(ps, axis=1).astype(BF16)
        acc_ref[...] += jnp.dot(p, kv_ref[0, j], preferred_element_type=F32)
        return carry

    lax.fori_loop(0, nkb, pv_body, 0)
    o = acc_ref[...] / jnp.sum(l_ref[...], axis=1, keepdims=True)
    for h in range(A_HEADS):
        o_ref[:, h * LANES:(h + 1) * LANES] = o[h * tq:(h + 1) * tq].astype(o_ref.dtype)


def _attn_call(q, iq, iwt, ik2, kvb, bias, bsz, seq):
    tq, kb = _TQ, _KB
    nq = seq // tq
    nblk = seq // kb
    n_sel = min(TOPK_KEYS_MAX, seq // 4)
    jbits = int(seq).bit_length()
    row = lambda b, i: (b * nq + i, 0)
    kern = functools.partial(_attn_kernel, n_sel=float(n_sel), jbits=jbits)
    return pl.pallas_call(
        kern,
        grid=(bsz, nq),
        in_specs=[pl.BlockSpec((tq, A_HEADS * A_LATENT), row),
                  pl.BlockSpec((tq, IDX_HEADS * IDX_DIM), row),
                  pl.BlockSpec((1, IDX_HEADS, tq), lambda b, i: (b, _SM_IW // IDX_HEADS, i)),
                  pl.BlockSpec((2, 1, nblk, kb, LANES), lambda b, i: (0, b, 0, 0, 0)),
                  pl.BlockSpec((1, nblk, kb, A_LATENT), lambda b, i: (b, 0, 0, 0)),
                  _const_spec(bias.shape)],
        out_specs=pl.BlockSpec((tq, A_HEADS * A_LATENT), row),
        out_shape=jax.ShapeDtypeStruct((bsz * seq, A_HEADS * A_LATENT), BF16),
        scratch_shapes=[pltpu.VMEM((nblk, kb, tq), I32),
                        pltpu.VMEM((nblk, kb // 2, tq), I32),
                        pltpu.VMEM((nblk, kb // 2, tq), I32),
                        pltpu.VMEM((nblk, A_HEADS * tq, kb), F32),
                        pltpu.VMEM((A_HEADS * tq, LANES), F32),
                        pltpu.VMEM((A_HEADS * tq, LANES), F32),
                        pltpu.VMEM((A_HEADS * tq, A_LATENT), F32)],
        compiler_params=_cparams(("arbitrary", "arbitrary")),
        name="attention",
    )(q, iq, iwt, ik2, kvb, bias)


_SSD_L = 256
_PAIRS = SSM_HEADS // 2
_CARRY = SUBLANES


def _ssd_kernel(z_ref, xbc_ref, sm_ref, dtt_ref, cw_ref, cb_ref, dtb_ref, dtbt_ref, al_ref, alt_ref,
                dsk_ref, nw_ref, o_ref, ext_ref, state_ref, y_ref):
    L = _SSD_L
    hd = SSM_HEADDIM

    @pl.when(pl.program_id(1) == 0)
    def _():
        ext_ref[0:_CARRY, :] = jnp.zeros((_CARRY, SSM_CONV_DIM), F32)
        state_ref[...] = jnp.zeros(state_ref.shape, F32)

    x = xbc_ref[...]
    ext_ref[_CARRY:_CARRY + L, :] = x
    w = cw_ref[...]
    conv = x * w[3:4] + cb_ref[...]
    for k in range(1, SSM_CONV):
        conv = conv + ext_ref[_CARRY - k:_CARRY - k + L, :] * w[SSM_CONV - 1 - k:SSM_CONV - k]
    ext_ref[0:_CARRY, :] = x[L - _CARRY:L]
    act = conv * jax.nn.sigmoid(conv)
    xs = act[:, :SSM_D_INNER]
    boff = SSM_D_INNER
    coff = SSM_D_INNER + SSM_GROUPS * SSM_STATE

    def softplus(v):
        return jnp.maximum(v, 0.0) + jnp.log1p(jnp.exp(-jnp.abs(v)))

    dt = softplus(sm_ref[:, _SM_DT:_SM_DT + SSM_HEADS] + dtb_ref[...])
    dtt = softplus(dtt_ref[0] + dtbt_ref[...])
    a_col = dt * (-jnp.exp(al_ref[...]))
    a_row = dtt * (-jnp.exp(alt_ref[...]))
    ri = lax.broadcasted_iota(I32, (L, L), 0)
    ci = lax.broadcasted_iota(I32, (L, L), 1)
    causal = ci <= ri
    cs_col = jnp.dot(jnp.where(causal, 1.0, 0.0), a_col, precision=HI, preferred_element_type=F32)
    cs_row = jnp.dot(a_row, jnp.where(ri <= ci, 1.0, 0.0), precision=HI, preferred_element_type=F32)
    cs_last = cs_col[L - 1:L, :]
    lane = lax.broadcasted_iota(I32, (1, LANES), 1)
    lo = lane < hd
    sub = lax.broadcasted_iota(I32, (LANES, 1), 0)

    for g in range(SSM_GROUPS):
        bm = act[:, boff + g * SSM_STATE: boff + (g + 1) * SSM_STATE].astype(BF16)
        cm = act[:, coff + g * SSM_STATE: coff + (g + 1) * SSM_STATE].astype(BF16)
        cb = lax.dot_general(cm, bm, (((1,), (1,)), ((), ())), preferred_element_type=F32)
        for pp in range(_PAIRS // SSM_GROUPS):
            p = g * (_PAIRS // SSM_GROUPS) + pp
            h0, h1 = 2 * p, 2 * p + 1
            xp = xs[:, p * LANES:(p + 1) * LANES]
            dtl = jnp.where(lo, dt[:, h0:h0 + 1], dt[:, h1:h1 + 1])
            xdt = xp * dtl
            csl = jnp.where(lo, cs_col[:, h0:h0 + 1], cs_col[:, h1:h1 + 1])
            last = jnp.where(lo, cs_last[:, h0:h0 + 1], cs_last[:, h1:h1 + 1])
            ydiag = jnp.zeros((L, LANES), F32)
            for h, msk in ((h0, lo), (h1, jnp.logical_not(lo))):
                seg = cs_col[:, h:h + 1] - cs_row[h:h + 1, :]
                gm = (cb * jnp.exp(jnp.where(causal, seg, -jnp.inf))).astype(BF16)
                ydiag = ydiag + jnp.dot(gm, jnp.where(msk, xdt, 0.0).astype(BF16), preferred_element_type=F32)
            prev = state_ref[p]
            yoff = lax.dot_general(cm, prev.astype(BF16), (((1,), (1,)), ((), ())), preferred_element_type=F32)
            y_ref[:, p * LANES:(p + 1) * LANES] = ydiag + yoff * jnp.exp(csl) + xp * dsk_ref[:, p * LANES:(p + 1) * LANES]
            wx = (xdt * jnp.exp(last - csl)).astype(BF16)
            st = lax.dot_general(wx, bm, (((0,), (0,)), ((), ())), preferred_element_type=F32)
            cdec = jnp.where(sub < hd, jnp.exp(cs_last[:, h0:h0 + 1]), jnp.exp(cs_last[:, h1:h1 + 1]))
            state_ref[p] = prev * cdec + st

    z = z_ref[...]
    y = y_ref[...] * (z * jax.nn.sigmoid(z))
    gw = SSM_D_INNER // SSM_GROUPS
    for g in range(SSM_GROUPS):
        yg = y[:, g * gw:(g + 1) * gw]
        yg = yg * lax.rsqrt(jnp.mean(yg * yg, axis=-1, keepdims=True) + LN_EPS)
        o_ref[:, g * gw:(g + 1) * gw] = (yg * nw_ref[:, g * gw:(g + 1) * gw]).astype(o_ref.dtype)


def _ssd_call(z, xbc, small, smt, conv_w, conv_b, dt_bias, a_log, d_skip, norm_w, bsz, seq):
    L = _SSD_L
    nc = seq // L
    row = lambda b, c: (b * nc + c, 0)
    h = SSM_HEADS
    return pl.pallas_call(
        _ssd_kernel,
        grid=(bsz, nc),
        in_specs=[pl.BlockSpec((L, SSM_D_INNER), row),
                  pl.BlockSpec((L, SSM_CONV_DIM), row),
                  pl.BlockSpec((L, LANES), row),
                  pl.BlockSpec((1, h, L), lambda b, c: (b, _SM_DT // h, c)),
                  _const_spec((SSM_CONV, SSM_CONV_DIM)), _const_spec((1, SSM_CONV_DIM)),
                  _const_spec((1, h)), _const_spec((h, 1)), _const_spec((1, h)), _const_spec((h, 1)),
                  _const_spec((1, SSM_D_INNER)), _const_spec((1, SSM_D_INNER))],
        out_specs=pl.BlockSpec((L, SSM_D_INNER), row),
        out_shape=jax.ShapeDtypeStruct((bsz * seq, SSM_D_INNER), BF16),
        scratch_shapes=[pltpu.VMEM((L + _CARRY, SSM_CONV_DIM), F32),
                        pltpu.VMEM((_PAIRS, LANES, SSM_STATE), F32),
                        pltpu.VMEM((L, SSM_D_INNER), F32)],
        compiler_params=_cparams(("arbitrary", "arbitrary")),
        name="ssd",
    )(z, xbc, small, smt, conv_w, conv_b.reshape(1, -1), dt_bias.reshape(1, h), dt_bias.reshape(h, 1),
      a_log.reshape(1, h), a_log.reshape(h, 1), jnp.repeat(d_skip, SSM_HEADDIM).reshape(1, -1),
      norm_w.reshape(1, -1))


_TM = 512
_ROUTE_ROWS = 2 * TOPK_EXPERTS


def _merge_kernel(oa_ref, ob_ref, ga_ref, gb_ref, x_ref, mod_ref, wpa_ref, wpb_ref, wo_ref, g1_ref, b1_ref,
                  wrh_ref, wrl_ref, br_ref, x1_ref, u2_ref, route_ref, gate_ref, cnt_ref, base_ref, *, alpha):
    tm = _TM
    ne = N_EXPERTS

    @pl.when(pl.program_id(0) == 0)
    def _():
        base_ref[...] = jnp.zeros(base_ref.shape, F32)

    ma = jnp.dot(oa_ref[...], wpa_ref[...], preferred_element_type=F32)
    mb = jnp.dot(ob_ref[...], wpb_ref[...], preferred_element_type=F32)
    merged = jax.nn.sigmoid(ga_ref[...]) * ma + jax.nn.sigmoid(gb_ref[...]) * mb
    t = jnp.dot(merged.astype(BF16), wo_ref[...], preferred_element_type=F32)
    x1 = _ln(alpha * x_ref[...] + mod_ref[0, 2:3, :] * t) * g1_ref[...] + b1_ref[...]
    x1_ref[...] = x1
    u2 = _ln(x1) * (1.0 + mod_ref[0, 4:5, :]) + mod_ref[0, 3:4, :]
    u2_ref[...] = u2
    nt = (((1,), (1,)), ((), ()))
    uh = u2.astype(BF16)
    ul = (u2 - uh.astype(F32)).astype(BF16)
    wh, wl = wrh_ref[...], wrl_ref[...]
    logits = (lax.dot_general(wh, uh, nt, preferred_element_type=F32)
              + lax.dot_general(wl, uh, nt, preferred_element_type=F32)
              + lax.dot_general(wh, ul, nt, preferred_element_type=F32)) + br_ref[...]
    eio = lax.broadcasted_iota(I32, (ne, tm), 0).astype(F32)
    vals, ids = [], []
    for _ in range(TOPK_EXPERTS):
        m = jnp.max(logits, axis=0, keepdims=True)
        idx = jnp.min(jnp.where(logits == m, eio, float(ne)), axis=0, keepdims=True)
        vals.append(m)
        ids.append(idx)
        logits = jnp.where(eio == idx, -jnp.inf, logits)
    es = [jnp.exp(v - vals[0]) for v in vals]
    den = es[0] + es[1] + es[2] + es[3]

    onehot = jnp.zeros((ne, tm), F32)
    for idx in ids:
        onehot = onehot + jnp.where(eio == idx, 1.0, 0.0)
    ri = lax.broadcasted_iota(I32, (tm, tm), 0)
    ci = lax.broadcasted_iota(I32, (tm, tm), 1)
    before = jnp.where(ri < ci, 1.0, 0.0).astype(BF16)
    base = base_ref[...]
    prefix = jnp.dot(onehot.astype(BF16), before, preferred_element_type=F32) + base
    sub = lax.broadcasted_iota(I32, (_ROUTE_ROWS, tm), 0)
    route = jnp.zeros((_ROUTE_ROWS, tm), F32)
    gates = jnp.zeros((_ROUTE_ROWS, tm), F32)
    for j in range(TOPK_EXPERTS):
        rank = jnp.sum(jnp.where(eio == ids[j], prefix, 0.0), axis=0, keepdims=True)
        route = jnp.where(sub == j, ids[j], route)
        route = jnp.where(sub == TOPK_EXPERTS + j, rank, route)
        gates = jnp.where(sub == j, es[j] / den, gates)
    route_ref[...] = route.astype(I32)
    gate_ref[...] = gates
    base = base + jnp.sum(onehot, axis=1, keepdims=True)
    base_ref[...] = base
    cnt_ref[...] = jnp.broadcast_to(base, cnt_ref.shape)


def _merge_call(o_a, o_b, g_a, g_b, x2, mod3, wpa, wpb, wo, ln_g, ln_b, w_router, b_router, seq, alpha):
    n_tok, d = x2.shape
    tm = _TM
    ne = N_EXPERTS
    row = lambda i: (i, 0)
    blk = pl.BlockSpec((tm, d), row)
    sm = pl.BlockSpec((_ROUTE_ROWS, tm), lambda i: (0, i))
    wrt = w_router.T
    wrh = wrt.astype(BF16)
    wrl = (wrt - wrh.astype(F32)).astype(BF16)
    return pl.pallas_call(
        functools.partial(_merge_kernel, alpha=alpha),
        grid=(n_tok // tm,),
        in_specs=[blk, blk, blk, blk, blk,
                  pl.BlockSpec((1, 6, d), lambda i: ((i * tm) // seq, 0, 0)),
                  _const_spec((d, d)), _const_spec((d, d)), _const_spec((d, d)),
                  _const_spec((1, d)), _const_spec((1, d)), _const_spec((ne, d)), _const_spec((ne, d)),
                  _const_spec((ne, 1))],
        out_specs=[blk, blk, sm, sm, pl.BlockSpec((ne, LANES), lambda i: (0, 0))],
        out_shape=[jax.ShapeDtypeStruct((n_tok, d), F32), jax.ShapeDtypeStruct((n_tok, d), F32),
                   jax.ShapeDtypeStruct((_ROUTE_ROWS, n_tok), I32), jax.ShapeDtypeStruct((_ROUTE_ROWS, n_tok), F32),
                   jax.ShapeDtypeStruct((ne, LANES), F32)],
        scratch_shapes=[pltpu.VMEM((ne, 1), F32)],
        compiler_params=_cparams(("arbitrary",)),
        name="merge",
    )(o_a, o_b, g_a, g_b, x2, mod3, wpa, wpb, wo, ln_g, ln_b, wrh, wrl, b_router.reshape(ne, 1))


_TD = 4096
_TMB = 512


def _dispatch_kernel(pend_ref, dest_ref, u2_ref, xs_ref, zero_ref, sem):
    @pl.when(pl.program_id(0) == 0)
    def _():
        zero_ref[...] = jnp.zeros(zero_ref.shape, F32)
        for e in range(N_EXPERTS):
            end = pend_ref[e]
            start = pend_ref[e - 1] if e else 0

            @pl.when(end > start)
            def _():
                dst = xs_ref.at[pl.ds(pl.multiple_of(end - _TMB, _TMB), _TMB), :]
                cp = pltpu.make_async_copy(zero_ref, dst, sem)
                cp.start()
                cp.wait()

        def zero_unused(b, carry):
            cp = pltpu.make_async_copy(zero_ref, xs_ref.at[pl.ds(pl.multiple_of(b * _TMB, _TMB), _TMB), :], sem)
            cp.start()
            cp.wait()
            return carry

        lax.fori_loop(pend_ref[N_EXPERTS - 1] // _TMB, xs_ref.shape[0] // _TMB, zero_unused, 0)

    def issue(t, carry):
        for j in range(TOPK_EXPERTS):
            d = dest_ref[t * TOPK_EXPERTS + j]
            pltpu.make_async_copy(u2_ref.at[pl.ds(t, 1), :], xs_ref.at[pl.ds(d, 1), :], sem).start()
        return carry

    lax.fori_loop(0, _TD, issue, 0)
    for _ in range(TOPK_EXPERTS):
        pltpu.make_async_copy(u2_ref, xs_ref.at[pl.ds(0, _TD), :], sem).wait()


def _dispatch_call(pends, dest_flat, u2, n_slots):
    n_tok, d = u2.shape
    grid_spec = pltpu.PrefetchScalarGridSpec(
        num_scalar_prefetch=1,
        grid=(n_tok // _TD,),
        in_specs=[pl.BlockSpec((_TD * TOPK_EXPERTS,), lambda i, pe: (i,), memory_space=pltpu.SMEM),
                  pl.BlockSpec((_TD, d), lambda i, pe: (i, 0))],
        out_specs=pl.BlockSpec(memory_space=pl.ANY),
        scratch_shapes=[pltpu.VMEM((_TMB, d), F32), pltpu.SemaphoreType.DMA(())],
    )
    return pl.pallas_call(
        _dispatch_kernel,
        grid_spec=grid_spec,
        out_shape=jax.ShapeDtypeStruct((n_slots, d), F32),
        compiler_params=_cparams(("arbitrary",)),
        name="dispatch",
    )(pends, dest_flat, u2)


def _expert_kernel(be_ref, nu_ref, xs_ref, w1_ref, b1_ref, w2_ref, b2_ref, y_ref, w1b_ref, w2b_ref):
    i = pl.program_id(0)
    f = w2_ref.shape[1]
    used = i < nu_ref[0]
    new_expert = jnp.logical_or(i == 0, be_ref[i] != be_ref[jnp.maximum(i - 1, 0)])

    @pl.when(jnp.logical_and(used, new_expert))
    def _():
        w1b_ref[...] = w1_ref[0].astype(BF16)
        w2b_ref[...] = w2_ref[0].astype(BF16)

    @pl.when(used)
    def _():
        h = jnp.dot(xs_ref[...].astype(BF16), w1b_ref[...], preferred_element_type=F32) + b1_ref[0]
        gate = jnp.minimum(h[:, :f], SWIGLU_LIMIT)
        up = jnp.clip(h[:, f:], -SWIGLU_LIMIT, SWIGLU_LIMIT)
        act = (up + 1.0) * gate * jax.nn.sigmoid(SWIGLU_ALPHA * gate)
        y_ref[...] = jnp.dot(act.astype(BF16), w2b_ref[...], preferred_element_type=F32) + b2_ref[0]

    @pl.when(i >= nu_ref[0])
    def _():
        y_ref[...] = jnp.zeros(y_ref.shape, F32)


def _expert_call(block_expert, n_used, xs, w1, b1, w2, b2):
    n_slots, d = xs.shape
    ne, _, f2 = w1.shape
    f = f2 // 2
    grid_spec = pltpu.PrefetchScalarGridSpec(
        num_scalar_prefetch=2,
        grid=(n_slots // _TMB,),
        in_specs=[pl.BlockSpec((_TMB, d), lambda i, be, nu: (i, 0)),
                  pl.BlockSpec((1, d, f2), lambda i, be, nu: (be[i], 0, 0)),
                  pl.BlockSpec((1, 1, f2), lambda i, be, nu: (be[i], 0, 0)),
                  pl.BlockSpec((1, f, d), lambda i, be, nu: (be[i], 0, 0)),
                  pl.BlockSpec((1, 1, d), lambda i, be, nu: (be[i], 0, 0))],
        out_specs=pl.BlockSpec((_TMB, d), lambda i, be, nu: (i, 0)),
        scratch_shapes=[pltpu.VMEM((d, f2), BF16), pltpu.VMEM((f, d), BF16)],
    )
    return pl.pallas_call(
        _expert_kernel,
        grid_spec=grid_spec,
        out_shape=jax.ShapeDtypeStruct((n_slots, d), F32),
        compiler_params=_cparams(("arbitrary",)),
        name="experts",
    )(block_expert, n_used, xs, w1, b1.reshape(ne, 1, f2), w2, b2.reshape(ne, 1, d))


_TC = 1024


def _combine_kernel(dest_ref, gate_ref, x1_ref, mod_ref, g2_ref, b2_ref, y_hbm, o_ref, buf_ref, sem, *, alpha):
    def issue(tb, carry):
        row0 = pl.multiple_of(tb * SUBLANES, SUBLANES)
        for r in range(SUBLANES):
            for j in range(TOPK_EXPERTS):
                d = dest_ref[(row0 + r) * TOPK_EXPERTS + j]
                pltpu.make_async_copy(y_hbm.at[pl.ds(d, 1), :], buf_ref.at[j, pl.ds(row0 + r, 1), :], sem).start()
        return carry

    lax.fori_loop(0, _TC // SUBLANES, issue, 0)
    for j in range(TOPK_EXPERTS):
        pltpu.make_async_copy(y_hbm.at[pl.ds(0, _TC), :], buf_ref.at[j], sem).wait()
    gates = gate_ref[...]
    y = gates[:, 0:1] * buf_ref[0]
    for j in range(1, TOPK_EXPERTS):
        y = y + gates[:, j:j + 1] * buf_ref[j]
    o_ref[...] = _ln(alpha * x1_ref[...] + mod_ref[0, 5:6, :] * y) * g2_ref[...] + b2_ref[...]


def _combine_call(dest_flat, gates, x1, mod3, ln_g, ln_b, y, seq, alpha):
    n_tok, d = x1.shape
    tc = _TC
    row = lambda i: (i, 0)
    return pl.pallas_call(
        functools.partial(_combine_kernel, alpha=alpha),
        grid=(n_tok // tc,),
        in_specs=[pl.BlockSpec((tc * TOPK_EXPERTS,), lambda i: (i,), memory_space=pltpu.SMEM),
                  pl.BlockSpec((tc, TOPK_EXPERTS), row),
                  pl.BlockSpec((tc, d), row),
                  pl.BlockSpec((1, 6, d), lambda i: ((i * tc) // seq, 0, 0)),
                  _const_spec((1, d)), _const_spec((1, d)),
                  pl.BlockSpec(memory_space=pl.ANY)],
        out_specs=pl.BlockSpec((tc, d), row),
        out_shape=jax.ShapeDtypeStruct((n_tok, d), F32),
        scratch_shapes=[pltpu.VMEM((TOPK_EXPERTS, tc, d), F32), pltpu.SemaphoreType.DMA(())],
        compiler_params=_cparams(("arbitrary",)),
        name="combine",
    )(dest_flat, gates, x1, mod3, ln_g, ln_b, y)


def _permute_w_in(w):
    d = w.shape[0]
    s = np.cumsum([0, A_HEADS * A_LATENT, A_LATENT, IDX_HEADS * IDX_DIM, IDX_DIM, IDX_HEADS,
                   SSM_D_INNER, SSM_CONV_DIM, SSM_HEADS, d, d]).tolist()
    q, kv, iq, ik, iw, z, xbc, dt, ga, gb = [w[:, s[i]:s[i + 1]] for i in range(10)]
    pad1 = jnp.zeros((d, _SM_DT - _SM_IW - IDX_HEADS), w.dtype)
    pad2 = jnp.zeros((d, LANES - _SM_DT - SSM_HEADS), w.dtype)
    return jnp.concatenate([q, kv, iq, ik, iw, pad1, dt, pad2, z, xbc, ga, gb], axis=1).astype(BF16)


def _pad_lanes(v, fill=0.0):
    return jnp.pad(v.reshape(1, -1), ((0, 0), (0, LANES - v.shape[-1])), constant_values=fill)


def kernel(x, c, w_mod, b_mod, w_in, kv_norm_w, idx_k_norm_w, idx_k_norm_b, rel_bias, conv_w, conv_b, dt_bias,
           a_log, d_skip, ssm_norm_w, w_proj_a, w_proj_b, w_out, ln1_g, ln1_b, w_router, b_router, w1, b1, w2, b2,
           ln2_g, ln2_b):
    bsz, seq, d = x.shape
    depth = w_mod.shape[0]
    alpha = (2.0 * depth) ** 0.25
    n_tok = bsz * seq
    assert d == D_MODEL, d
    assert seq % max(_TQ, _KB, _SSD_L, _TI, _TM, _TC) == 0 and n_tok % _TD == 0, (bsz, seq)
    n_asg = n_tok * TOPK_EXPERTS
    n_blocks = n_asg // _TMB + N_EXPERTS
    n_slots = n_blocks * _TMB
    nblk = seq // _KB
    bias = _bias_tiles(rel_bias)
    x2 = x.reshape(n_tok, d)
    for l in range(depth):
        mod3 = _mod_call(c, w_mod[l], b_mod[l]).reshape(bsz, 6, d)
        q, kvn, iq, ik2, small, smt, z, xbc, g_a, g_b = _inproj_call(
            x2, mod3, _permute_w_in(w_in[l]), kv_norm_w[l].reshape(1, -1),
            _pad_lanes(idx_k_norm_w[l]), _pad_lanes(idx_k_norm_b[l]), seq)
        kvb = kvn.reshape(bsz, nblk, _KB, A_LATENT)
        o_a = _attn_call(q, iq, smt, ik2.reshape(2, bsz, nblk, _KB, LANES), kvb, bias, bsz, seq)
        o_b = _ssd_call(z, xbc, small, smt, conv_w[l], conv_b[l], dt_bias[l], a_log[l], d_skip[l], ssm_norm_w[l],
                        bsz, seq)
        x1, u2, route, gates, cnt = _merge_call(
            o_a, o_b, g_a, g_b, x2, mod3, w_proj_a[l].astype(BF16), w_proj_b[l].astype(BF16),
            w_out[l].astype(BF16), ln1_g[l].reshape(1, -1), ln1_b[l].reshape(1, -1), w_router[l], b_router[l],
            seq, alpha)
        counts = cnt[:, 0].astype(I32)
        padded = (counts + _TMB - 1) // _TMB * _TMB
        pends = jnp.cumsum(padded).astype(I32)
        pstarts = pends - padded
        eid = route[:TOPK_EXPERTS]
        onehot = eid[:, :, None] == jnp.arange(N_EXPERTS, dtype=I32)
        dest = jnp.sum(jnp.where(onehot, pstarts, 0), axis=-1) + route[TOPK_EXPERTS:2 * TOPK_EXPERTS]
        dest_flat = dest.T.reshape(n_asg)
        block_start = jnp.arange(n_blocks, dtype=I32) * _TMB
        block_expert = jnp.minimum(jnp.sum(block_start[:, None] >= pends[None, :], axis=1), N_EXPERTS - 1).astype(I32)
        n_used = (pends[-1:] // _TMB).astype(I32)
        xs = _dispatch_call(pends, dest_flat, u2, n_slots)
        y = _expert_call(block_expert, n_used, xs, w1[l], b1[l], w2[l], b2[l])
        x2 = _combine_call(dest_flat, gates[:TOPK_EXPERTS].T, x1, mod3, ln2_g[l].reshape(1, -1),
                           ln2_b[l].reshape(1, -1), y, seq, alpha)
    return x2.reshape(bsz, seq, d)
```

```python
import functools
import math

import jax
import jax.numpy as jnp
import numpy as np
from jax import lax
from jax.experimental import pallas as pl
from jax.experimental.pallas import tpu as pltpu

F32 = jnp.float32
BF16 = jnp.bfloat16
I32 = jnp.int32

CHUNK = 64
A_HEADS = 8
A_LATENT = 128
IDX_HEADS = 8
IDX_DIM = 64
TOPK_KEYS_MAX = 256
REL_BUCKETS = 32
REL_MAX_DIST = 128
SSM_D_INNER = 1024
SSM_HEADDIM = 64
SSM_HEADS = SSM_D_INNER // SSM_HEADDIM
SSM_GROUPS = 4
SSM_STATE = 128
SSM_CONV = 4
SSM_CONV_DIM = SSM_D_INNER + 2 * SSM_GROUPS * SSM_STATE
N_EXPERTS = 32
TOPK_EXPERTS = 4
SWIGLU_LIMIT = 7.0
SWIGLU_ALPHA = 1.702
LN_EPS = 1e-5

LANES = 128
SUBLANES = 8
INT_MIN = -2147483648
VMEM_LIMIT = 56 * 1024 * 1024

HI = lax.Precision.HIGHEST


def _cparams(sem):
    return pltpu.CompilerParams(dimension_semantics=sem, vmem_limit_bytes=VMEM_LIMIT)


def _ln(x):
    mu = jnp.mean(x, axis=-1, keepdims=True)
    xc = x - mu
    var = jnp.mean(xc * xc, axis=-1, keepdims=True)
    return xc * lax.rsqrt(var + LN_EPS)


def _const_spec(shape):
    nd = len(shape)
    return pl.BlockSpec(shape, lambda *_: (0,) * nd, pipeline_mode=pl.Buffered(1))


def _mod_kernel(c_ref, w_ref, b_ref, o_ref):
    c = c_ref[...]
    sc = c * jax.nn.sigmoid(c)
    o_ref[...] = jnp.dot(sc, w_ref[...], precision=HI, preferred_element_type=F32) + b_ref[...]


def _mod_call(c, w_mod, b_mod):
    bsz, d = c.shape
    n = w_mod.shape[1]
    tn = 1024
    return pl.pallas_call(
        _mod_kernel,
        grid=(n // tn,),
        in_specs=[pl.BlockSpec((bsz, d), lambda j: (0, 0)),
                  pl.BlockSpec((d, tn), lambda j: (0, j)),
                  pl.BlockSpec((1, tn), lambda j: (0, j))],
        out_specs=pl.BlockSpec((bsz, tn), lambda j: (0, j)),
        out_shape=jax.ShapeDtypeStruct((bsz, n), F32),
        compiler_params=_cparams(("arbitrary",)),
        name="mod",
    )(c, w_mod, b_mod.reshape(1, n))


D_MODEL = 1024
_W_Q, _W_KV, _W_IQ = A_HEADS * A_LATENT, A_LATENT, IDX_HEADS * IDX_DIM
_GROUP_WIDTHS = (_W_Q, _W_KV, _W_IQ, LANES, SSM_D_INNER, SSM_CONV_DIM, D_MODEL, D_MODEL)
_C_Q, _C_KV, _C_IQ, _C_SM, _C_Z, _C_XBC, _C_GA, _C_GB, _C_END = np.cumsum((0,) + _GROUP_WIDTHS).tolist()
_TI = 512
_SM_IW = IDX_DIM
_SM_DT = IDX_DIM + SSM_HEADS


def _inproj_kernel(x_ref, mod_ref, w_ref, kvw_ref, ikw_ref, ikb_ref,
                   q_ref, kv_ref, iq_ref, ik2_ref, sm_ref, smt_ref, z_ref, xbc_ref, ga_ref, gb_ref):
    u = _ln(x_ref[...]) * (1.0 + mod_ref[0, 1:2, :]) + mod_ref[0, 0:1, :]
    ub = u.astype(BF16)

    def mm(a, b):
        return jnp.dot(ub, w_ref[:, a:b], preferred_element_type=F32)

    q_ref[...] = mm(_C_Q, _C_KV).astype(BF16)
    kv = mm(_C_KV, _C_IQ)
    kv = kv * lax.rsqrt(jnp.mean(kv * kv, axis=-1, keepdims=True) + LN_EPS)
    kv_ref[...] = (kv * kvw_ref[...]).astype(BF16)
    iq_ref[...] = mm(_C_IQ, _C_SM).astype(BF16)
    g = mm(_C_SM, _C_Z)
    lane = lax.broadcasted_iota(I32, g.shape, 1)
    is_ik = lane < IDX_DIM
    mu = jnp.sum(jnp.where(is_ik, g, 0.0), axis=-1, keepdims=True) * (1.0 / IDX_DIM)
    gc = g - mu
    var = jnp.sum(jnp.where(is_ik, gc * gc, 0.0), axis=-1, keepdims=True) * (1.0 / IDX_DIM)
    ik = jnp.where(is_ik, gc * lax.rsqrt(var + LN_EPS) * ikw_ref[...] + ikb_ref[...], 0.0)
    ik2_ref[0] = ik.astype(BF16)
    ik2_ref[1] = pltpu.roll(ik, IDX_DIM, axis=1).astype(BF16)
    sm = jnp.where(lane < _SM_IW + IDX_HEADS, g * (IDX_HEADS ** -0.5), g)
    sm_ref[...] = sm
    smt_ref[0] = sm.T
    z_ref[...] = mm(_C_Z, _C_XBC)
    xbc_ref[...] = mm(_C_XBC, _C_GA)
    ga_ref[...] = mm(_C_GA, _C_GB)
    gb_ref[...] = mm(_C_GB, _C_END)


def _inproj_call(x2, mod3, w_perm, kvw, ikw, ikb, seq):
    n_tok, d = x2.shape
    tm = _TI
    row = lambda i: (i, 0)

    def ospec(n):
        return pl.BlockSpec((tm, n), row)

    spt = seq // tm
    sd = jax.ShapeDtypeStruct
    return pl.pallas_call(
        _inproj_kernel,
        grid=(n_tok // tm,),
        in_specs=[pl.BlockSpec((tm, d), row),
                  pl.BlockSpec((1, 6, d), lambda i: (i // spt, 0, 0)),
                  _const_spec(w_perm.shape), _const_spec((1, _W_KV)), _const_spec((1, LANES)), _const_spec((1, LANES))],
        out_specs=[ospec(_W_Q), ospec(_W_KV), ospec(_W_IQ),
                   pl.BlockSpec((2, tm, LANES), lambda i: (0, i, 0)),
                   ospec(LANES),
                   pl.BlockSpec((1, LANES, tm), lambda i: (i // spt, 0, i % spt)),
                   ospec(SSM_D_INNER), ospec(SSM_CONV_DIM), ospec(D_MODEL), ospec(D_MODEL)],
        out_shape=[sd((n_tok, _W_Q), BF16), sd((n_tok, _W_KV), BF16), sd((n_tok, _W_IQ), BF16),
                   sd((2, n_tok, LANES), BF16), sd((n_tok, LANES), F32), sd((n_tok // seq, LANES, seq), F32),
                   sd((n_tok, SSM_D_INNER), F32), sd((n_tok, SSM_CONV_DIM), F32),
                   sd((n_tok, D_MODEL), F32), sd((n_tok, D_MODEL), F32)],
        compiler_params=_cparams(("arbitrary",)),
        name="inproj",
    )(x2, mod3, w_perm, kvw, ikw, ikb)


_TQ = 256
_KB = 256
_BIAS_STEP = math.gcd(_TQ, _KB)
_NEAR_DELTAS = tuple(range(-_KB, 1, _BIAS_STEP))
assert REL_MAX_DIST <= _BIAS_STEP
_HALF = 16
_PBITS = _HALF - 1
_FIELD_ONES = (1 << _HALF) | 1
_GUARDS = (_FIELD_ONES << _PBITS) - (1 << 32)


def _t5_bucket(rel):
    half = REL_BUCKETS // 2
    max_exact = half // 2
    ret = (rel > 0).astype(jnp.int32) * half
    n = jnp.abs(rel)
    nf = jnp.maximum(n, 1).astype(jnp.float32)
    large = max_exact + (jnp.log(nf / max_exact) / math.log(REL_MAX_DIST / max_exact)
                         * (half - max_exact)).astype(jnp.int32)
    large = jnp.minimum(large, half - 1)
    return ret + jnp.where(n < max_exact, n, large)


def _bias_tiles(rel_bias):
    i = jnp.arange(_TQ, dtype=jnp.int32)[:, None]
    c = jnp.arange(_KB, dtype=jnp.int32)[None, :]
    rel = [c + delta - i for delta in _NEAR_DELTAS] + [jnp.full((_TQ, _KB), -REL_MAX_DIST, jnp.int32)]
    bucket = _t5_bucket(jnp.stack(rel))
    b = jnp.zeros((len(rel), A_HEADS, _TQ, _KB), F32)
    for k in range(REL_BUCKETS):
        b = jnp.where((bucket == k)[:, None], rel_bias[k].astype(F32)[None, :, None, None], b)
    return (b * (A_LATENT ** 0.5)).reshape(len(rel), A_HEADS * _TQ, _KB)


def _attn_kernel(q_ref, iq_ref, iwt_ref, ik2_ref, kv_ref, bias_ref, o_ref,
                 keys_ref, pk_ref, pk2_ref, s_ref, mx_ref, l_ref, acc_ref, *, n_sel, jbits):
    tq, kb = _TQ, _KB
    qs = pl.program_id(1) * tq
    nkb = (qs + tq + kb - 1) // kb
    nt = (((1,), (1,)), ((), ()))

    qpos = lax.broadcasted_iota(I32, (1, tq), 1) + qs
    qend = (qpos // CHUNK + 1) * CHUNK
    krow = lax.broadcasted_iota(I32, (kb, tq), 0)
    iq = iq_ref[...]
    iwt = iwt_ref[0] * (IDX_DIM ** -0.5)

    def score_body(j, carry):
        acc = jnp.zeros((kb, tq), F32)
        for p in range(IDX_HEADS // 2):
            pair = iq[:, p * LANES:(p + 1) * LANES]
            for par in range(2):
                h = 2 * p + par
                s = lax.dot_general(ik2_ref[par, 0, j], pair, nt, preferred_element_type=F32)
                acc = acc + jnp.maximum(s, 0.0) * iwt[h:h + 1, :]
        bits = pltpu.bitcast(acc, I32)
        key = jnp.where(bits < 0, bits ^ 0x7FFFFFFF, bits)
        key = jnp.where(bits == INT_MIN, 0, key)
        key = jnp.where(krow + j * kb < qend, key, INT_MIN)
        keys_ref[j] = key
        top = (key >> (32 - _PBITS)) + (1 << (_PBITS - 1))
        pk_ref[j] = (top[:kb // 2] << _HALF) | top[kb // 2:] | _GUARDS
        return carry

    lax.fori_loop(0, nkb, score_body, 0)

    def select(nk):
        if nk * kb <= n_sel:
            return jnp.full((1, tq), INT_MIN, I32), jnp.full((1, tq), (1 << jbits) - 1, I32)

        def over_blocks(body, init):
            for j in range(nk):
                init = body(j, init)
            return init

        def count_packed(ref, cand):
            both = (cand << _HALF) | cand

            def body(j, acc):
                g = ((ref[j] - both) >> _PBITS) & _FIELD_ONES
                parts = [g[SUBLANES * i:SUBLANES * (i + 1)] for i in range(kb // 2 // SUBLANES)]
                while len(parts) > 1:
                    parts = [parts[i] + parts[i + 1] for i in range(0, len(parts), 2)]
                return acc + parts[0]
            acc = over_blocks(body, jnp.zeros((SUBLANES, tq), I32))
            return jnp.sum(((acc & ((1 << _HALF) - 1)) + (acc >> _HALF)).astype(F32), axis=0, keepdims=True)

        def search_packed(ref, offset, n_at_zero):
            def body(it, carry):
                tu, n_tu = carry
                cu = tu | (jnp.int32(1) << (_PBITS - 1 - it))
                n_cu = offset + count_packed(ref, cu)
                ok = n_cu >= n_sel
                return jnp.where(ok, cu, tu), jnp.where(ok, n_cu, n_tu)
            return lax.fori_loop(0, _PBITS, body, (jnp.zeros((1, tq), I32), n_at_zero))

        def count(pred):
            def body(j, acc):
                m = jnp.where(pred(keys_ref[j], krow + j * kb), 1.0, 0.0)
                parts = [m[SUBLANES * i:SUBLANES * (i + 1)] for i in range(kb // SUBLANES)]
                while len(parts) > 1:
                    parts = [parts[i] + parts[i + 1] for i in range(0, len(parts), 2)]
                return acc + parts[0]
            acc = over_blocks(body, jnp.zeros((SUBLANES, tq), F32))
            return jnp.sum(acc, axis=0, keepdims=True)

        n_all = jnp.full((1, tq), float(nk * kb), F32)
        top_bits, n_top = search_packed(pk_ref, 0.0, n_all)
        n_above = count_packed(pk_ref, top_bits + 1)

        def mid_body(j, carry):
            key = keys_ref[j]
            mid = jnp.where((key >> (32 - _PBITS)) + (1 << (_PBITS - 1)) == top_bits,
                            (key >> (32 - 2 * _PBITS)) & ((1 << _PBITS) - 1), 0)
            pk2_ref[j] = (mid[:kb // 2] << _HALF) | mid[kb // 2:] | _GUARDS
            return carry

        over_blocks(mid_body, 0)
        mid_bits, n_mid = search_packed(pk2_ref, n_above, n_top)

        def bit_body(it, carry):
            tu, n_tu = carry
            cu = tu | (jnp.int32(1) << (31 - it))
            cs = cu ^ INT_MIN
            n_cu = count(lambda k, kidx: k >= cs)
            ok = n_cu >= n_sel
            return jnp.where(ok, cu, tu), jnp.where(ok, n_cu, n_tu)

        tu0 = (top_bits << (32 - _PBITS)) | (mid_bits << (32 - 2 * _PBITS))
        thr_u, n_ge = lax.fori_loop(2 * _PBITS, 32, bit_body, (tu0, n_mid))
        thr = thr_u ^ INT_MIN

        def tie_search():
            need = n_sel - count(lambda k, kidx: k > thr)

            def j_body(it, jj):
                cj = jj | (jnp.int32(1) << (jbits - 1 - it))
                f = count(lambda k, kidx: (k == thr) & (kidx < cj))
                return jnp.where(f <= need, cj, jj)

            return lax.fori_loop(0, jbits, j_body, jnp.zeros((1, tq), I32))

        jj = lax.cond(jnp.max(n_ge) > n_sel, tie_search, lambda: jnp.full((1, tq), (1 << jbits) - 1, I32))
        return thr, jj

    thr, jj = lax.switch(nkb - 1, [functools.partial(select, n) for n in range(1, keys_ref.shape[0] + 1)])


    q = q_ref[...]
    q_all = jnp.concatenate([q[:, h * LANES:(h + 1) * LANES] for h in range(A_HEADS)], axis=0)
    eye = jnp.where(lax.broadcasted_iota(I32, (tq, tq), 0) == lax.broadcasted_iota(I32, (tq, tq), 1),
                    1.0, 0.0).astype(BF16)
    mx_ref[...] = jnp.full(mx_ref.shape, -jnp.inf, F32)
    exp2_scale = A_LATENT ** -0.5 * math.log2(math.e)

    def logit_body(j, carry):
        k = keys_ref[j]
        kidx = krow + j * kb
        selt = ((k > thr) | ((k == thr) & (kidx < jj))) & (kidx < qend)
        sel = lax.dot_general(eye, jnp.where(selt, 1.0, 0.0).astype(BF16), nt, preferred_element_type=F32) > 0.5
        s = lax.dot_general(q_all, kv_ref[0, j], nt, preferred_element_type=F32)
        v = (j * kb - qs + kb) // _BIAS_STEP
        s = s + bias_ref[jnp.where(v < 0, len(_NEAR_DELTAS), v)]
        s = jnp.concatenate([jnp.where(sel, s[h * tq:(h + 1) * tq], -jnp.inf) for h in range(A_HEADS)], axis=0)
        s_ref[j] = s
        mx = mx_ref[...]
        for c in range(kb // LANES):
            mx = jnp.maximum(mx, s[:, c * LANES:(c + 1) * LANES])
        mx_ref[...] = mx
        return carry

    lax.fori_loop(0, nkb, logit_body, 0)
    m = jnp.max(mx_ref[...], axis=1, keepdims=True)
    m = jnp.where(m == -jnp.inf, 0.0, m)
    mx_ref[...] = jnp.broadcast_to(m, mx_ref.shape)
    l_ref[...] = jnp.zeros(l_ref.shape, F32)
    acc_ref[...] = jnp.zeros(acc_ref.shape, F32)

    def pv_body(j, carry):
        mb = mx_ref[...]
        s = s_ref[j]
        ps = [jnp.exp2((s[:, c * LANES:(c + 1) * LANES] - mb) * exp2_scale) for c in range(kb // LANES)]
        lsum = l_ref[...]
        for pc in ps:
            lsum = lsum + pc
        l_ref[...] = lsum
        p = jnp.concatenate(ps, axis=1).astype(BF16)
        acc_ref[...] += jnp.dot(p, kv_ref[0, j], preferred_element_type=F32)
        return carry

    lax.fori_loop(0, nkb, pv_body, 0)
    o = acc_ref[...] / jnp.sum(l_ref[...], axis=1, keepdims=True)
    for h in range(A_HEADS):
        o_ref[:, h * LANES:(h + 1) * LANES] = o[h * tq:(h + 1) * tq].astype(o_ref.dtype)


def _attn_call(q, iq, iwt, ik2, kvb, bias, bsz, seq):
    tq, kb = _TQ, _KB
    nq = seq // tq
    nblk = seq // kb
    n_sel = min(TOPK_KEYS_MAX, seq // 4)
    jbits = int(seq).bit_length()
    row = lambda b, i: (b * nq + i, 0)
    kern = functools.partial(_attn_kernel, n_sel=float(n_sel), jbits=jbits)
    return pl.pallas_call(
        kern,
        grid=(bsz, nq),
        in_specs=[pl.BlockSpec((tq, A_HEADS * A_LATENT), row),
                  pl.BlockSpec((tq, IDX_HEADS * IDX_DIM), row),
                  pl.BlockSpec((1, IDX_HEADS, tq), lambda b, i: (b, _SM_IW // IDX_HEADS, i)),
                  pl.BlockSpec((2, 1, nblk, kb, LANES), lambda b, i: (0, b, 0, 0, 0)),
                  pl.BlockSpec((1, nblk, kb, A_LATENT), lambda b, i: (b, 0, 0, 0)),
                  _const_spec(bias.shape)],
        out_specs=pl.BlockSpec((tq, A_HEADS * A_LATENT), row),
        out_shape=jax.ShapeDtypeStruct((bsz * seq, A_HEADS * A_LATENT), BF16),
        scratch_shapes=[pltpu.VMEM((nblk, kb, tq), I32),
                        pltpu.VMEM((nblk, kb // 2, tq), I32),
                        pltpu.VMEM((nblk, kb // 2, tq), I32),
                        pltpu.VMEM((nblk, A_HEADS * tq, kb), F32),
                        pltpu.VMEM((A_HEADS * tq, LANES), F32),
                        pltpu.VMEM((A_HEADS * tq, LANES), F32),
                        pltpu.VMEM((A_HEADS * tq, A_LATENT), F32)],
        compiler_params=_cparams(("arbitrary", "arbitrary")),
        name="attention",
    )(q, iq, iwt, ik2, kvb, bias)


_SSD_L = 256
_PAIRS = SSM_HEADS // 2
_CARRY = SUBLANES


def _ssd_kernel(z_ref, xbc_ref, sm_ref, dtt_ref, cw_ref, cb_ref, dtb_ref, dtbt_ref, al_ref, alt_ref,
                dsk_ref, nw_ref, o_ref, ext_ref, state_ref, y_ref):
    L = _SSD_L
    hd = SSM_HEADDIM

    @pl.when(pl.program_id(1) == 0)
    def _():
        ext_ref[0:_CARRY, :] = jnp.zeros((_CARRY, SSM_CONV_DIM), F32)
        state_ref[...] = jnp.zeros(state_ref.shape, F32)

    x = xbc_ref[...]
    ext_ref[_CARRY:_CARRY + L, :] = x
    w = cw_ref[...]
    conv = x * w[3:4] + cb_ref[...]
    for k in range(1, SSM_CONV):
        conv = conv + ext_ref[_CARRY - k:_CARRY - k + L, :] * w[SSM_CONV - 1 - k:SSM_CONV - k]
    ext_ref[0:_CARRY, :] = x[L - _CARRY:L]
    act = conv * jax.nn.sigmoid(conv)
    xs = act[:, :SSM_D_INNER]
    boff = SSM_D_INNER
    coff = SSM_D_INNER + SSM_GROUPS * SSM_STATE

    def softplus(v):
        return jnp.maximum(v, 0.0) + jnp.log1p(jnp.exp(-jnp.abs(v)))

    dt = softplus(sm_ref[:, _SM_DT:_SM_DT + SSM_HEADS] + dtb_ref[...])
    dtt = softplus(dtt_ref[0] + dtbt_ref[...])
    a_col = dt * (-jnp.exp(al_ref[...]))
    a_row = dtt * (-jnp.exp(alt_ref[...]))
    ri = lax.broadcasted_iota(I32, (L, L), 0)
    ci = lax.broadcasted_iota(I32, (L, L), 1)
    causal = ci <= ri
    cs_col = jnp.dot(jnp.where(causal, 1.0, 0.0), a_col, precision=HI, preferred_element_type=F32)
    cs_row = jnp.dot(a_row, jnp.where(ri <= ci, 1.0, 0.0), precision=HI, preferred_element_type=F32)
    cs_last = cs_col[L - 1:L, :]
    lane = lax.broadcasted_iota(I32, (1, LANES), 1)
    lo = lane < hd
    sub = lax.broadcasted_iota(I32, (LANES, 1), 0)

    for g in range(SSM_GROUPS):
        bm = act[:, boff + g * SSM_STATE: boff + (g + 1) * SSM_STATE].astype(BF16)
        cm = act[:, coff + g * SSM_STATE: coff + (g + 1) * SSM_STATE].astype(BF16)
        cb = lax.dot_general(cm, bm, (((1,), (1,)), ((), ())), preferred_element_type=F32)
        for pp in range(_PAIRS // SSM_GROUPS):
            p = g * (_PAIRS // SSM_GROUPS) + pp
            h0, h1 = 2 * p, 2 * p + 1
            xp = xs[:, p * LANES:(p + 1) * LANES]
            dtl = jnp.where(lo, dt[:, h0:h0 + 1], dt[:, h1:h1 + 1])
            xdt = xp * dtl
            csl = jnp.where(lo, cs_col[:, h0:h0 + 1], cs_col[:, h1:h1 + 1])
            last = jnp.where(lo, cs_last[:, h0:h0 + 1], cs_last[:, h1:h1 + 1])
            ydiag = jnp.zeros((L, LANES), F32)
            for h, msk in ((h0, lo), (h1, jnp.logical_not(lo))):
                seg = cs_col[:, h:h + 1] - cs_row[h:h + 1, :]
                gm = (cb * jnp.exp(jnp.where(causal, seg, -jnp.inf))).astype(BF16)
                ydiag = ydiag + jnp.dot(gm, jnp.where(msk, xdt, 0.0).astype(BF16), preferred_element_type=F32)
            prev = state_ref[p]
            yoff = lax.dot_general(cm, prev.astype(BF16), (((1,), (1,)), ((), ())), preferred_element_type=F32)
            y_ref[:, p * LANES:(p + 1) * LANES] = ydiag + yoff * jnp.exp(csl) + xp * dsk_ref[:, p * LANES:(p + 1) * LANES]
            wx = (xdt * jnp.exp(last - csl)).astype(BF16)
            st = lax.dot_general(wx, bm, (((0,), (0,)), ((), ())), preferred_element_type=F32)
            cdec = jnp.where(sub < hd, jnp.exp(cs_last[:, h0:h0 + 1]), jnp.exp(cs_last[:, h1:h1 + 1]))
            state_ref[p] = prev * cdec + st

    z = z_ref[...]
    y = y_ref[...] * (z * jax.nn.sigmoid(z))
    gw = SSM_D_INNER // SSM_GROUPS
    for g in range(SSM_GROUPS):
        yg = y[:, g * gw:(g + 1) * gw]
        yg = yg * lax.rsqrt(jnp.mean(yg * yg, axis=-1, keepdims=True) + LN_EPS)
        o_ref[:, g * gw:(g + 1) * gw] = (yg * nw_ref[:, g * gw:(g + 1) * gw]).astype(o_ref.dtype)


def _ssd_call(z, xbc, small, smt, conv_w, conv_b, dt_bias, a_log, d_skip, norm_w, bsz, seq):
    L = _SSD_L
    nc = seq // L
    row = lambda b, c: (b * nc + c, 0)
    h = SSM_HEADS
    return pl.pallas_call(
        _ssd_kernel,
        grid=(bsz, nc),
        in_specs=[pl.BlockSpec((L, SSM_D_INNER), row),
                  pl.BlockSpec((L, SSM_CONV_DIM), row),
                  pl.BlockSpec((L, LANES), row),
                  pl.BlockSpec((1, h, L), lambda b, c: (b, _SM_DT // h, c)),
                  _const_spec((SSM_CONV, SSM_CONV_DIM)), _const_spec((1, SSM_CONV_DIM)),
                  _const_spec((1, h)), _const_spec((h, 1)), _const_spec((1, h)), _const_spec((h, 1)),
                  _const_spec((1, SSM_D_INNER)), _const_spec((1, SSM_D_INNER))],
        out_specs=pl.BlockSpec((L, SSM_D_INNER), row),
        out_shape=jax.ShapeDtypeStruct((bsz * seq, SSM_D_INNER), BF16),
        scratch_shapes=[pltpu.VMEM((L + _CARRY, SSM_CONV_DIM), F32),
                        pltpu.VMEM((_PAIRS, LANES, SSM_STATE), F32),
                        pltpu.VMEM((L, SSM_D_INNER), F32)],
        compiler_params=_cparams(("arbitrary", "arbitrary")),
        name="ssd",
    )(z, xbc, small, smt, conv_w, conv_b.reshape(1, -1), dt_bias.reshape(1, h), dt_bias.reshape(h, 1),
      a_log.reshape(1, h), a_log.reshape(h, 1), jnp.repeat(d_skip, SSM_HEADDIM).reshape(1, -1),
      norm_w.reshape(1, -1))


_TM = 512
_ROUTE_ROWS = 2 * TOPK_EXPERTS


def _merge_kernel(oa_ref, ob_ref, ga_ref, gb_ref, x_ref, mod_ref, wpa_ref, wpb_ref, wo_ref, g1_ref, b1_ref,
                  wrh_ref, wrl_ref, br_ref, x1_ref, u2_ref, route_ref, gate_ref, cnt_ref, base_ref, *, alpha):
    tm = _TM
    ne = N_EXPERTS

    @pl.when(pl.program_id(0) == 0)
    def _():
        base_ref[...] = jnp.zeros(base_ref.shape, F32)

    ma = jnp.dot(oa_ref[...], wpa_ref[...], preferred_element_type=F32)
    mb = jnp.dot(ob_ref[...], wpb_ref[...], preferred_element_type=F32)
    merged = jax.nn.sigmoid(ga_ref[...]) * ma + jax.nn.sigmoid(gb_ref[...]) * mb
    t = jnp.dot(merged.astype(BF16), wo_ref[...], preferred_element_type=F32)
    x1 = _ln(alpha * x_ref[...] + mod_ref[0, 2:3, :] * t) * g1_ref[...] + b1_ref[...]
    x1_ref[...] = x1
    u2 = _ln(x1) * (1.0 + mod_ref[0, 4:5, :]) + mod_ref[0, 3:4, :]
    u2_ref[...] = u2
    nt = (((1,), (1,)), ((), ()))
    uh = u2.astype(BF16)
    ul = (u2 - uh.astype(F32)).astype(BF16)
    wh, wl = wrh_ref[...], wrl_ref[...]
    logits = (lax.dot_general(wh, uh, nt, preferred_element_type=F32)
              + lax.dot_general(wl, uh, nt, preferred_element_type=F32)
              + lax.dot_general(wh, ul, nt, preferred_element_type=F32)) + br_ref[...]
    eio = lax.broadcasted_iota(I32, (ne, tm), 0).astype(F32)
    vals, ids = [], []
    for _ in range(TOPK_EXPERTS):
        m = jnp.max(logits, axis=0, keepdims=True)
        idx = jnp.min(jnp.where(logits == m, eio, float(ne)), axis=0, keepdims=True)
        vals.append(m)
        ids.append(idx)
        logits = jnp.where(eio == idx, -jnp.inf, logits)
    es = [jnp.exp(v - vals[0]) for v in vals]
    den = es[0] + es[1] + es[2] + es[3]

    onehot = jnp.zeros((ne, tm), F32)
    for idx in ids:
        onehot = onehot + jnp.where(eio == idx, 1.0, 0.0)
    ri = lax.broadcasted_iota(I32, (tm, tm), 0)
    ci = lax.broadcasted_iota(I32, (tm, tm), 1)
    before = jnp.where(ri < ci, 1.0, 0.0).astype(BF16)
    base = base_ref[...]
    prefix = jnp.dot(onehot.astype(BF16), before, preferred_element_type=F32) + base
    sub = lax.broadcasted_iota(I32, (_ROUTE_ROWS, tm), 0)
    route = jnp.zeros((_ROUTE_ROWS, tm), F32)
    gates = jnp.zeros((_ROUTE_ROWS, tm), F32)
    for j in range(TOPK_EXPERTS):
        rank = jnp.sum(jnp.where(eio == ids[j], prefix, 0.0), axis=0, keepdims=True)
        route = jnp.where(sub == j, ids[j], route)
        route = jnp.where(sub == TOPK_EXPERTS + j, rank, route)
        gates = jnp.where(sub == j, es[j] / den, gates)
    route_ref[...] = route.astype(I32)
    gate_ref[...] = gates
    base = base + jnp.sum(onehot, axis=1, keepdims=True)
    base_ref[...] = base
    cnt_ref[...] = jnp.broadcast_to(base, cnt_ref.shape)


def _merge_call(o_a, o_b, g_a, g_b, x2, mod3, wpa, wpb, wo, ln_g, ln_b, w_router, b_router, seq, alpha):
    n_tok, d = x2.shape
    tm = _TM
    ne = N_EXPERTS
    row = lambda i: (i, 0)
    blk = pl.BlockSpec((tm, d), row)
    sm = pl.BlockSpec((_ROUTE_ROWS, tm), lambda i: (0, i))
    wrt = w_router.T
    wrh = wrt.astype(BF16)
    wrl = (wrt - wrh.astype(F32)).astype(BF16)
    return pl.pallas_call(
        functools.partial(_merge_kernel, alpha=alpha),
        grid=(n_tok // tm,),
        in_specs=[blk, blk, blk, blk, blk,
                  pl.BlockSpec((1, 6, d), lambda i: ((i * tm) // seq, 0, 0)),
                  _const_spec((d, d)), _const_spec((d, d)), _const_spec((d, d)),
                  _const_spec((1, d)), _const_spec((1, d)), _const_spec((ne, d)), _const_spec((ne, d)),
                  _const_spec((ne, 1))],
        out_specs=[blk, blk, sm, sm, pl.BlockSpec((ne, LANES), lambda i: (0, 0))],
        out_shape=[jax.ShapeDtypeStruct((n_tok, d), F32), jax.ShapeDtypeStruct((n_tok, d), F32),
                   jax.ShapeDtypeStruct((_ROUTE_ROWS, n_tok), I32), jax.ShapeDtypeStruct((_ROUTE_ROWS, n_tok), F32),
                   jax.ShapeDtypeStruct((ne, LANES), F32)],
        scratch_shapes=[pltpu.VMEM((ne, 1), F32)],
        compiler_params=_cparams(("arbitrary",)),
        name="merge",
    )(o_a, o_b, g_a, g_b, x2, mod3, wpa, wpb, wo, ln_g, ln_b, wrh, wrl, b_router.reshape(ne, 1))


_TD = 4096
_TMB = 512


def _dispatch_kernel(pend_ref, dest_ref, u2_ref, xs_ref, zero_ref, sem):
    @pl.when(pl.program_id(0) == 0)
    def _():
        zero_ref[...] = jnp.zeros(zero_ref.shape, F32)
        for e in range(N_EXPERTS):
            end = pend_ref[e]
            start = pend_ref[e - 1] if e else 0

            @pl.when(end > start)
            def _():
                dst = xs_ref.at[pl.ds(pl.multiple_of(end - _TMB, _TMB), _TMB), :]
                cp = pltpu.make_async_copy(zero_ref, dst, sem)
                cp.start()
                cp.wait()

        def zero_unused(b, carry):
            cp = pltpu.make_async_copy(zero_ref, xs_ref.at[pl.ds(pl.multiple_of(b * _TMB, _TMB), _TMB), :], sem)
            cp.start()
            cp.wait()
            return carry

        lax.fori_loop(pend_ref[N_EXPERTS - 1] // _TMB, xs_ref.shape[0] // _TMB, zero_unused, 0)

    def issue(tb, carry):
        row0 = pl.multiple_of(tb * SUBLANES, SUBLANES)
        for r in range(SUBLANES):
            for j in range(TOPK_EXPERTS):
                d = dest_ref[(row0 + r) * TOPK_EXPERTS + j]
                pltpu.make_async_copy(u2_ref.at[pl.ds(row0 + r, 1), :], xs_ref.at[pl.ds(d, 1), :], sem).start()
        return carry

    lax.fori_loop(0, _TD // SUBLANES, issue, 0)
    for _ in range(TOPK_EXPERTS):
        pltpu.make_async_copy(u2_ref, xs_ref.at[pl.ds(0, _TD), :], sem).wait()


def _dispatch_call(pends, dest_flat, u2, n_slots):
    n_tok, d = u2.shape
    grid_spec = pltpu.PrefetchScalarGridSpec(
        num_scalar_prefetch=1,
        grid=(n_tok // _TD,),
        in_specs=[pl.BlockSpec((_TD * TOPK_EXPERTS,), lambda i, pe: (i,), memory_space=pltpu.SMEM),
                  pl.BlockSpec((_TD, d), lambda i, pe: (i, 0))],
        out_specs=pl.BlockSpec(memory_space=pl.ANY),
        scratch_shapes=[pltpu.VMEM((_TMB, d), F32), pltpu.SemaphoreType.DMA(())],
    )
    return pl.pallas_call(
        _dispatch_kernel,
        grid_spec=grid_spec,
        out_shape=jax.ShapeDtypeStruct((n_slots, d), F32),
        compiler_params=_cparams(("arbitrary",)),
        name="dispatch",
    )(pends, dest_flat, u2)


def _expert_kernel(be_ref, nu_ref, xs_ref, w1_ref, b1_ref, w2_ref, b2_ref, y_ref, w1b_ref, w2b_ref):
    i = pl.program_id(0)
    f = w2_ref.shape[1]
    used = i < nu_ref[0]
    new_expert = jnp.logical_or(i == 0, be_ref[i] != be_ref[jnp.maximum(i - 1, 0)])

    @pl.when(jnp.logical_and(used, new_expert))
    def _():
        w1b_ref[...] = w1_ref[0].astype(BF16)
        w2b_ref[...] = w2_ref[0].astype(BF16)

    @pl.when(used)
    def _():
        h = jnp.dot(xs_ref[...].astype(BF16), w1b_ref[...], preferred_element_type=F32) + b1_ref[0]
        gate = jnp.minimum(h[:, :f], SWIGLU_LIMIT)
        up = jnp.clip(h[:, f:], -SWIGLU_LIMIT, SWIGLU_LIMIT)
        act = (up + 1.0) * gate * jax.nn.sigmoid(SWIGLU_ALPHA * gate)
        y_ref[...] = jnp.dot(act.astype(BF16), w2b_ref[...], preferred_element_type=F32) + b2_ref[0]

    @pl.when(i >= nu_ref[0])
    def _():
        y_ref[...] = jnp.zeros(y_ref.shape, F32)


def _expert_call(block_expert, n_used, xs, w1, b1, w2, b2):
    n_slots, d = xs.shape
    ne, _, f2 = w1.shape
    f = f2 // 2
    grid_spec = pltpu.PrefetchScalarGridSpec(
        num_scalar_prefetch=2,
        grid=(n_slots // _TMB,),
        in_specs=[pl.BlockSpec((_TMB, d), lambda i, be, nu: (i, 0)),
                  pl.BlockSpec((1, d, f2), lambda i, be, nu: (be[i], 0, 0)),
                  pl.BlockSpec((1, 1, f2), lambda i, be, nu: (be[i], 0, 0)),
                  pl.BlockSpec((1, f, d), lambda i, be, nu: (be[i], 0, 0)),
                  pl.BlockSpec((1, 1, d), lambda i, be, nu: (be[i], 0, 0))],
        out_specs=pl.BlockSpec((_TMB, d), lambda i, be, nu: (i, 0)),
        scratch_shapes=[pltpu.VMEM((d, f2), BF16), pltpu.VMEM((f, d), BF16)],
    )
    return pl.pallas_call(
        _expert_kernel,
        grid_spec=grid_spec,
        out_shape=jax.ShapeDtypeStruct((n_slots, d), F32),
        compiler_params=_cparams(("arbitrary",)),
        name="experts",
    )(block_expert, n_used, xs, w1, b1.reshape(ne, 1, f2), w2, b2.reshape(ne, 1, d))


_TC = 1024


def _combine_kernel(dest_ref, gate_ref, x1_ref, mod_ref, g2_ref, b2_ref, y_hbm, o_ref, buf_ref, sem, *, alpha):
    def issue(tb, carry):
        row0 = pl.multiple_of(tb * SUBLANES, SUBLANES)
        for r in range(SUBLANES):
            for j in range(TOPK_EXPERTS):
                d = dest_ref[(row0 + r) * TOPK_EXPERTS + j]
                pltpu.make_async_copy(y_hbm.at[pl.ds(d, 1), :], buf_ref.at[j, pl.ds(row0 + r, 1), :], sem).start()
        return carry

    lax.fori_loop(0, _TC // SUBLANES, issue, 0)
    for j in range(TOPK_EXPERTS):
        pltpu.make_async_copy(y_hbm.at[pl.ds(0, _TC), :], buf_ref.at[j], sem).wait()
    gates = gate_ref[...]
    y = gates[:, 0:1] * buf_ref[0]
    for j in range(1, TOPK_EXPERTS):
        y = y + gates[:, j:j + 1] * buf_ref[j]
    o_ref[...] = _ln(alpha * x1_ref[...] + mod_ref[0, 5:6, :] * y) * g2_ref[...] + b2_ref[...]


def _combine_call(dest_flat, gates, x1, mod3, ln_g, ln_b, y, seq, alpha):
    n_tok, d = x1.shape
    tc = _TC
    row = lambda i: (i, 0)
    return pl.pallas_call(
        functools.partial(_combine_kernel, alpha=alpha),
        grid=(n_tok // tc,),
        in_specs=[pl.BlockSpec((tc * TOPK_EXPERTS,), lambda i: (i,), memory_space=pltpu.SMEM),
                  pl.BlockSpec((tc, TOPK_EXPERTS), row),
                  pl.BlockSpec((tc, d), row),
                  pl.BlockSpec((1, 6, d), lambda i: ((i * tc) // seq, 0, 0)),
                  _const_spec((1, d)), _const_spec((1, d)),
                  pl.BlockSpec(memory_space=pl.ANY)],
        out_specs=pl.BlockSpec((tc, d), row),
        out_shape=jax.ShapeDtypeStruct((n_tok, d), F32),
        scratch_shapes=[pltpu.VMEM((TOPK_EXPERTS, tc, d), F32), pltpu.SemaphoreType.DMA(())],
        compiler_params=_cparams(("arbitrary",)),
        name="combine",
    )(dest_flat, gates, x1, mod3, ln_g, ln_b, y)


def _permute_w_in(w):
    d = w.shape[0]
    s = np.cumsum([0, A_HEADS * A_LATENT, A_LATENT, IDX_HEADS * IDX_DIM, IDX_DIM, IDX_HEADS,
                   SSM_D_INNER, SSM_CONV_DIM, SSM_HEADS, d, d]).tolist()
    q, kv, iq, ik, iw, z, xbc, dt, ga, gb = [w[:, s[i]:s[i + 1]] for i in range(10)]
    pad1 = jnp.zeros((d, _SM_DT - _SM_IW - IDX_HEADS), w.dtype)
    pad2 = jnp.zeros((d, LANES - _SM_DT - SSM_HEADS), w.dtype)
    return jnp.concatenate([q, kv, iq, ik, iw, pad1, dt, pad2, z, xbc, ga, gb], axis=1).astype(BF16)


def _pad_lanes(v, fill=0.0):
    return jnp.pad(v.reshape(1, -1), ((0, 0), (0, LANES - v.shape[-1])), constant_values=fill)


def kernel(x, c, w_mod, b_mod, w_in, kv_norm_w, idx_k_norm_w, idx_k_norm_b, rel_bias, conv_w, conv_b, dt_bias,
           a_log, d_skip, ssm_norm_w, w_proj_a, w_proj_b, w_out, ln1_g, ln1_b, w_router, b_router, w1, b1, w2, b2,
           ln2_g, ln2_b):
    bsz, seq, d = x.shape
    depth = w_mod.shape[0]
    alpha = (2.0 * depth) ** 0.25
    n_tok = bsz * seq
    assert d == D_MODEL, d
    assert seq % max(_TQ, _KB, _SSD_L, _TI, _TM, _TC) == 0 and n_tok % _TD == 0, (bsz, seq)
    n_asg = n_tok * TOPK_EXPERTS
    n_blocks = n_asg // _TMB + N_EXPERTS
    n_slots = n_blocks * _TMB
    nblk = seq // _KB
    bias = _bias_tiles(rel_bias)
    x2 = x.reshape(n_tok, d)
    for l in range(depth):
        mod3 = _mod_call(c, w_mod[l], b_mod[l]).reshape(bsz, 6, d)
        q, kvn, iq, ik2, small, smt, z, xbc, g_a, g_b = _inproj_call(
            x2, mod3, _permute_w_in(w_in[l]), kv_norm_w[l].reshape(1, -1),
            _pad_lanes(idx_k_norm_w[l]), _pad_lanes(idx_k_norm_b[l]), seq)
        kvb = kvn.reshape(bsz, nblk, _KB, A_LATENT)
        o_a = _attn_call(q, iq, smt, ik2.reshape(2, bsz, nblk, _KB, LANES), kvb, bias, bsz, seq)
        o_b = _ssd_call(z, xbc, small, smt, conv_w[l], conv_b[l], dt_bias[l], a_log[l], d_skip[l], ssm_norm_w[l],
                        bsz, seq)
        x1, u2, route, gates, cnt = _merge_call(
            o_a, o_b, g_a, g_b, x2, mod3, w_proj_a[l].astype(BF16), w_proj_b[l].astype(BF16),
            w_out[l].astype(BF16), ln1_g[l].reshape(1, -1), ln1_b[l].reshape(1, -1), w_router[l], b_router[l],
            seq, alpha)
        counts = cnt[:, 0].astype(I32)
        padded = (counts + _TMB - 1) // _TMB * _TMB
        pends = jnp.cumsum(padded).astype(I32)
        pstarts = pends - padded
        eid = route[:TOPK_EXPERTS]
        onehot = eid[:, :, None] == jnp.arange(N_EXPERTS, dtype=I32)
        dest = jnp.sum(jnp.where(onehot, pstarts, 0), axis=-1) + route[TOPK_EXPERTS:2 * TOPK_EXPERTS]
        dest_flat = dest.T.reshape(n_asg)
        block_start = jnp.arange(n_blocks, dtype=I32) * _TMB
        block_expert = jnp.minimum(jnp.sum(block_start[:, None] >= pends[None, :], axis=1), N_EXPERTS - 1).astype(I32)
        n_used = (pends[-1:] // _TMB).astype(I32)
        xs = _dispatch_call(pends, dest_flat, u2, n_slots)
        y = _expert_call(block_expert, n_used, xs, w1[l], b1[l], w2[l], b2[l])
        x2 = _combine_call(dest_flat, gates[:TOPK_EXPERTS].T, x1, mod3, ln2_g[l].reshape(1, -1),
                           ln2_b[l].reshape(1, -1), y, seq, alpha)
    return x2.reshape(bsz, seq, d)
```

```python
import functools
import math

import jax
import jax.numpy as jnp
import numpy as np
from jax import lax
from jax.experimental import pallas as pl
from jax.experimental.pallas import tpu as pltpu

F32 = jnp.float32
BF16 = jnp.bfloat16
I32 = jnp.int32

CHUNK = 64
A_HEADS = 8
A_LATENT = 128
IDX_HEADS = 8
IDX_DIM = 64
TOPK_KEYS_MAX = 256
REL_BUCKETS = 32
REL_MAX_DIST = 128
SSM_D_INNER = 1024
SSM_HEADDIM = 64
SSM_HEADS = SSM_D_INNER // SSM_HEADDIM
SSM_GROUPS = 4
SSM_STATE = 128
SSM_CONV = 4
SSM_CONV_DIM = SSM_D_INNER + 2 * SSM_GROUPS * SSM_STATE
N_EXPERTS = 32
TOPK_EXPERTS = 4
SWIGLU_LIMIT = 7.0
SWIGLU_ALPHA = 1.702
LN_EPS = 1e-5

LANES = 128
SUBLANES = 8
INT_MIN = -2147483648
VMEM_LIMIT = 56 * 1024 * 1024

HI = lax.Precision.HIGHEST


def _cparams(sem):
    return pltpu.CompilerParams(dimension_semantics=sem, vmem_limit_bytes=VMEM_LIMIT)


def _ln(x):
    mu = jnp.mean(x, axis=-1, keepdims=True)
    xc = x - mu
    var = jnp.mean(xc * xc, axis=-1, keepdims=True)
    return xc * lax.rsqrt(var + LN_EPS)


def _const_spec(shape):
    nd = len(shape)
    return pl.BlockSpec(shape, lambda *_: (0,) * nd, pipeline_mode=pl.Buffered(1))


def _mod_kernel(c_ref, w_ref, b_ref, o_ref):
    c = c_ref[...]
    sc = c * jax.nn.sigmoid(c)
    o_ref[...] = jnp.dot(sc, w_ref[...], precision=HI, preferred_element_type=F32) + b_ref[...]


def _mod_call(c, w_mod, b_mod):
    bsz, d = c.shape
    n = w_mod.shape[1]
    tn = 1024
    return pl.pallas_call(
        _mod_kernel,
        grid=(n // tn,),
        in_specs=[pl.BlockSpec((bsz, d), lambda j: (0, 0)),
                  pl.BlockSpec((d, tn), lambda j: (0, j)),
                  pl.BlockSpec((1, tn), lambda j: (0, j))],
        out_specs=pl.BlockSpec((bsz, tn), lambda j: (0, j)),
        out_shape=jax.ShapeDtypeStruct((bsz, n), F32),
        compiler_params=_cparams(("arbitrary",)),
        name="mod",
    )(c, w_mod, b_mod.reshape(1, n))


D_MODEL = 1024
_W_Q, _W_KV, _W_IQ = A_HEADS * A_LATENT, A_LATENT, IDX_HEADS * IDX_DIM
_GROUP_WIDTHS = (_W_Q, _W_KV, _W_IQ, LANES, SSM_D_INNER, SSM_CONV_DIM, D_MODEL, D_MODEL)
_C_Q, _C_KV, _C_IQ, _C_SM, _C_Z, _C_XBC, _C_GA, _C_GB, _C_END = np.cumsum((0,) + _GROUP_WIDTHS).tolist()
_TI = 512
_SM_IW = IDX_DIM
_SM_DT = IDX_DIM + SSM_HEADS


def _inproj_kernel(x_ref, mod_ref, w_ref, kvw_ref, ikw_ref, ikb_ref,
                   q_ref, kv_ref, iq_ref, ik2_ref, sm_ref, smt_ref, z_ref, xbc_ref, ga_ref, gb_ref):
    u = _ln(x_ref[...]) * (1.0 + mod_ref[0, 1:2, :]) + mod_ref[0, 0:1, :]
    ub = u.astype(BF16)

    def mm(a, b):
        return jnp.dot(ub, w_ref[:, a:b], preferred_element_type=F32)

    q_ref[...] = mm(_C_Q, _C_KV).astype(BF16)
    kv = mm(_C_KV, _C_IQ)
    kv = kv * lax.rsqrt(jnp.mean(kv * kv, axis=-1, keepdims=True) + LN_EPS)
    kv_ref[...] = (kv * kvw_ref[...]).astype(BF16)
    iq_ref[...] = mm(_C_IQ, _C_SM).astype(BF16)
    g = mm(_C_SM, _C_Z)
    lane = lax.broadcasted_iota(I32, g.shape, 1)
    is_ik = lane < IDX_DIM
    mu = jnp.sum(jnp.where(is_ik, g, 0.0), axis=-1, keepdims=True) * (1.0 / IDX_DIM)
    gc = g - mu
    var = jnp.sum(jnp.where(is_ik, gc * gc, 0.0), axis=-1, keepdims=True) * (1.0 / IDX_DIM)
    ik = jnp.where(is_ik, gc * lax.rsqrt(var + LN_EPS) * ikw_ref[...] + ikb_ref[...], 0.0)
    ik2_ref[0] = ik.astype(BF16)
    ik2_ref[1] = pltpu.roll(ik, IDX_DIM, axis=1).astype(BF16)
    sm = jnp.where(lane < _SM_IW + IDX_HEADS, g * (IDX_HEADS ** -0.5), g)
    sm_ref[...] = sm
    smt_ref[0] = sm.T
    z_ref[...] = mm(_C_Z, _C_XBC)
    xbc_ref[...] = mm(_C_XBC, _C_GA)
    ga_ref[...] = mm(_C_GA, _C_GB)
    gb_ref[...] = mm(_C_GB, _C_END)


def _inproj_call(x2, mod3, w_perm, kvw, ikw, ikb, seq):
    n_tok, d = x2.shape
    tm = _TI
    row = lambda i: (i, 0)

    def ospec(n):
        return pl.BlockSpec((tm, n), row)

    spt = seq // tm
    sd = jax.ShapeDtypeStruct
    return pl.pallas_call(
        _inproj_kernel,
        grid=(n_tok // tm,),
        in_specs=[pl.BlockSpec((tm, d), row),
                  pl.BlockSpec((1, 6, d), lambda i: (i // spt, 0, 0)),
                  _const_spec(w_perm.shape), _const_spec((1, _W_KV)), _const_spec((1, LANES)), _const_spec((1, LANES))],
        out_specs=[ospec(_W_Q), ospec(_W_KV), ospec(_W_IQ),
                   pl.BlockSpec((2, tm, LANES), lambda i: (0, i, 0)),
                   ospec(LANES),
                   pl.BlockSpec((1, LANES, tm), lambda i: (i // spt, 0, i % spt)),
                   ospec(SSM_D_INNER), ospec(SSM_CONV_DIM), ospec(D_MODEL), ospec(D_MODEL)],
        out_shape=[sd((n_tok, _W_Q), BF16), sd((n_tok, _W_KV), BF16), sd((n_tok, _W_IQ), BF16),
                   sd((2, n_tok, LANES), BF16), sd((n_tok, LANES), F32), sd((n_tok // seq, LANES, seq), F32),
                   sd((n_tok, SSM_D_INNER), F32), sd((n_tok, SSM_CONV_DIM), F32),
                   sd((n_tok, D_MODEL), F32), sd((n_tok, D_MODEL), F32)],
        compiler_params=_cparams(("arbitrary",)),
        name="inproj",
    )(x2, mod3, w_perm, kvw, ikw, ikb)


_TQ = 256
_KB = 256
_BIAS_STEP = math.gcd(_TQ, _KB)
_NEAR_DELTAS = tuple(range(-_KB, 1, _BIAS_STEP))
assert REL_MAX_DIST <= _BIAS_STEP
_HALF = 16
_PBITS = _HALF - 1
_FIELD_ONES = (1 << _HALF) | 1
_GUARDS = (_FIELD_ONES << _PBITS) - (1 << 32)


def _t5_bucket(rel):
    half = REL_BUCKETS // 2
    max_exact = half // 2
    ret = (rel > 0).astype(jnp.int32) * half
    n = jnp.abs(rel)
    nf = jnp.maximum(n, 1).astype(jnp.float32)
    large = max_exact + (jnp.log(nf / max_exact) / math.log(REL_MAX_DIST / max_exact)
                         * (half - max_exact)).astype(jnp.int32)
    large = jnp.minimum(large, half - 1)
    return ret + jnp.where(n < max_exact, n, large)


def _bias_tiles(rel_bias):
    i = jnp.arange(_TQ, dtype=jnp.int32)[:, None]
    c = jnp.arange(_KB, dtype=jnp.int32)[None, :]
    rel = [c + delta - i for delta in _NEAR_DELTAS] + [jnp.full((_TQ, _KB), -REL_MAX_DIST, jnp.int32)]
    bucket = _t5_bucket(jnp.stack(rel))
    b = jnp.zeros((len(rel), A_HEADS, _TQ, _KB), F32)
    for k in range(REL_BUCKETS):
        b = jnp.where((bucket == k)[:, None], rel_bias[k].astype(F32)[None, :, None, None], b)
    return (b * (A_LATENT ** 0.5)).reshape(len(rel), A_HEADS * _TQ, _KB)


def _attn_kernel(q_ref, iq_ref, iwt_ref, ik2_ref, kv_ref, bias_ref, o_ref,
                 keys_ref, pk_ref, pk2_ref, s_ref, mx_ref, l_ref, acc_ref, *, n_sel, jbits):
    tq, kb = _TQ, _KB
    qs = pl.program_id(1) * tq
    nkb = (qs + tq + kb - 1) // kb
    nt = (((1,), (1,)), ((), ()))

    qpos = lax.broadcasted_iota(I32, (1, tq), 1) + qs
    qend = (qpos // CHUNK + 1) * CHUNK
    krow = lax.broadcasted_iota(I32, (kb, tq), 0)
    iq = iq_ref[...]
    iwt = iwt_ref[0] * (IDX_DIM ** -0.5)

    def score_body(j, carry):
        acc = jnp.zeros((kb, tq), F32)
        for p in range(IDX_HEADS // 2):
            pair = iq[:, p * LANES:(p + 1) * LANES]
            for par in range(2):
                h = 2 * p + par
                s = lax.dot_general(ik2_ref[par, 0, j], pair, nt, preferred_element_type=F32)
                acc = acc + jnp.maximum(s, 0.0) * iwt[h:h + 1, :]
        bits = pltpu.bitcast(acc, I32)
        key = jnp.where(bits < 0, bits ^ 0x7FFFFFFF, bits)
        key = jnp.where(bits == INT_MIN, 0, key)
        key = jnp.where(krow + j * kb < qend, key, INT_MIN)
        keys_ref[j] = key
        top = (key >> (32 - _PBITS)) + (1 << (_PBITS - 1))
        pk_ref[j] = (top[:kb // 2] << _HALF) | top[kb // 2:] | _GUARDS
        return carry

    lax.fori_loop(0, nkb, score_body, 0)

    def select(nk):
        if nk * kb <= n_sel:
            return jnp.full((1, tq), INT_MIN, I32), jnp.full((1, tq), (1 << jbits) - 1, I32)

        def over_blocks(body, init):
            for j in range(nk):
                init = body(j, init)
            return init

        def count_packed(ref, cand):
            both = (cand << _HALF) | cand

            def body(j, acc):
                g = ((ref[j] - both) >> _PBITS) & _FIELD_ONES
                parts = [g[SUBLANES * i:SUBLANES * (i + 1)] for i in range(kb // 2 // SUBLANES)]
                while len(parts) > 1:
                    parts = [parts[i] + parts[i + 1] for i in range(0, len(parts), 2)]
                return acc + parts[0]
            acc = over_blocks(body, jnp.zeros((SUBLANES, tq), I32))
            return jnp.sum(((acc & ((1 << _HALF) - 1)) + (acc >> _HALF)).astype(F32), axis=0, keepdims=True)

        def search_packed(ref, offset, n_at_zero):
            def body(it, carry):
                tu, n_tu = carry
                cu = tu | (jnp.int32(1) << (_PBITS - 1 - it))
                n_cu = offset + count_packed(ref, cu)
                ok = n_cu >= n_sel
                return jnp.where(ok, cu, tu), jnp.where(ok, n_cu, n_tu)
            return lax.fori_loop(0, _PBITS, body, (jnp.zeros((1, tq), I32), n_at_zero), unroll=3)

        def count(pred):
            def body(j, acc):
                m = jnp.where(pred(keys_ref[j], krow + j * kb), 1.0, 0.0)
                parts = [m[SUBLANES * i:SUBLANES * (i + 1)] for i in range(kb // SUBLANES)]
                while len(parts) > 1:
                    parts = [parts[i] + parts[i + 1] for i in range(0, len(parts), 2)]
                return acc + parts[0]
            acc = over_blocks(body, jnp.zeros((SUBLANES, tq), F32))
            return jnp.sum(acc, axis=0, keepdims=True)

        n_all = jnp.full((1, tq), float(nk * kb), F32)
        top_bits, n_top = search_packed(pk_ref, 0.0, n_all)
        n_above = count_packed(pk_ref, top_bits + 1)

        def mid_body(j, carry):
            key = keys_ref[j]
            mid = jnp.where((key >> (32 - _PBITS)) + (1 << (_PBITS - 1)) == top_bits,
                            (key >> (32 - 2 * _PBITS)) & ((1 << _PBITS) - 1), 0)
            pk2_ref[j] = (mid[:kb // 2] << _HALF) | mid[kb // 2:] | _GUARDS
            return carry

        over_blocks(mid_body, 0)
        mid_bits, n_mid = search_packed(pk2_ref, n_above, n_top)

        def bit_body(it, carry):
            tu, n_tu = carry
            cu = tu | (jnp.int32(1) << (31 - it))
            cs = cu ^ INT_MIN
            n_cu = count(lambda k, kidx: k >= cs)
            ok = n_cu >= n_sel
            return jnp.where(ok, cu, tu), jnp.where(ok, n_cu, n_tu)

        tu0 = (top_bits << (32 - _PBITS)) | (mid_bits << (32 - 2 * _PBITS))
        thr_u, n_ge = lax.fori_loop(2 * _PBITS, 32, bit_body, (tu0, n_mid))
        thr = thr_u ^ INT_MIN

        def tie_search():
            need = n_sel - count(lambda k, kidx: k > thr)

            def j_body(it, jj):
                cj = jj | (jnp.int32(1) << (jbits - 1 - it))
                f = count(lambda k, kidx: (k == thr) & (kidx < cj))
                return jnp.where(f <= need, cj, jj)

            return lax.fori_loop(0, jbits, j_body, jnp.zeros((1, tq), I32))

        jj = lax.cond(jnp.max(n_ge) > n_sel, tie_search, lambda: jnp.full((1, tq), (1 << jbits) - 1, I32))
        return thr, jj

    thr, jj = lax.switch(nkb - 1, [functools.partial(select, n) for n in range(1, keys_ref.shape[0] + 1)])


    q = q_ref[...]
    q_all = jnp.concatenate([q[:, h * LANES:(h + 1) * LANES] for h in range(A_HEADS)], axis=0)
    eye = jnp.where(lax.broadcasted_iota(I32, (tq, tq), 0) == lax.broadcasted_iota(I32, (tq, tq), 1),
                    1.0, 0.0).astype(BF16)
    mx_ref[...] = jnp.full(mx_ref.shape, -jnp.inf, F32)
    exp2_scale = A_LATENT ** -0.5 * math.log2(math.e)

    def logit_body(j, carry):
        k = keys_ref[j]
        kidx = krow + j * kb
        selt = ((k > thr) | ((k == thr) & (kidx < jj))) & (kidx < qend)
        sel = lax.dot_general(eye, jnp.where(selt, 1.0, 0.0).astype(BF16), nt, preferred_element_type=F32) > 0.5
        s = lax.dot_general(q_all, kv_ref[0, j], nt, preferred_element_type=F32)
        v = (j * kb - qs + kb) // _BIAS_STEP
        s = s + bias_ref[jnp.where(v < 0, len(_NEAR_DELTAS), v)]
        s = jnp.concatenate([jnp.where(sel, s[h * tq:(h + 1) * tq], -jnp.inf) for h in range(A_HEADS)], axis=0)
        s_ref[j] = s
        mx = mx_ref[...]
        for c in range(kb // LANES):
            mx = jnp.maximum(mx, s[:, c * LANES:(c + 1) * LANES])
        mx_ref[...] = mx
        return carry

    lax.fori_loop(0, nkb, logit_body, 0)
    m = jnp.max(mx_ref[...], axis=1, keepdims=True)
    m = jnp.where(m == -jnp.inf, 0.0, m)
    mx_ref[...] = jnp.broadcast_to(m, mx_ref.shape)
    l_ref[...] = jnp.zeros(l_ref.shape, F32)
    acc_ref[...] = jnp.zeros(acc_ref.shape, F32)

    def pv_body(j, carry):
        mb = mx_ref[...]
        s = s_ref[j]
        ps = [jnp.exp2((s[:, c * LANES:(c + 1) * LANES] - mb) * exp2_scale) for c in range(kb // LANES)]
        lsum = l_ref[...]
        for pc in ps:
            lsum = lsum + pc
        l_ref[...] = lsum
        p = jnp.concatenate(ps, axis=1).astype(BF16)
        acc_ref[...] += jnp.dot(p, kv_ref[0, j], preferred_element_type=F32)
        return carry

    lax.fori_loop(0, nkb, pv_body, 0)
    o = acc_ref[...] / jnp.sum(l_ref[...], axis=1, keepdims=True)
    for h in range(A_HEADS):
        o_ref[:, h * LANES:(h + 1) * LANES] = o[h * tq:(h + 1) * tq].astype(o_ref.dtype)


def _attn_call(q, iq, iwt, ik2, kvb, bias, bsz, seq):
    tq, kb = _TQ, _KB
    nq = seq // tq
    nblk = seq // kb
    n_sel = min(TOPK_KEYS_MAX, seq // 4)
    jbits = int(seq).bit_length()
    row = lambda b, i: (b * nq + i, 0)
    kern = functools.partial(_attn_kernel, n_sel=float(n_sel), jbits=jbits)
    return pl.pallas_call(
        kern,
        grid=(bsz, nq),
        in_specs=[pl.BlockSpec((tq, A_HEADS * A_LATENT), row),
                  pl.BlockSpec((tq, IDX_HEADS * IDX_DIM), row),
                  pl.BlockSpec((1, IDX_HEADS, tq), lambda b, i: (b, _SM_IW // IDX_HEADS, i)),
                  pl.BlockSpec((2, 1, nblk, kb, LANES), lambda b, i: (0, b, 0, 0, 0)),
                  pl.BlockSpec((1, nblk, kb, A_LATENT), lambda b, i: (b, 0, 0, 0)),
                  _const_spec(bias.shape)],
        out_specs=pl.BlockSpec((tq, A_HEADS * A_LATENT), row),
        out_shape=jax.ShapeDtypeStruct((bsz * seq, A_HEADS * A_LATENT), BF16),
        scratch_shapes=[pltpu.VMEM((nblk, kb, tq), I32),
                        pltpu.VMEM((nblk, kb // 2, tq), I32),
                        pltpu.VMEM((nblk, kb // 2, tq), I32),
                        pltpu.VMEM((nblk, A_HEADS * tq, kb), F32),
                        pltpu.VMEM((A_HEADS * tq, LANES), F32),
                        pltpu.VMEM((A_HEADS * tq, LANES), F32),
                        pltpu.VMEM((A_HEADS * tq, A_LATENT), F32)],
        compiler_params=_cparams(("arbitrary", "arbitrary")),
        name="attention",
    )(q, iq, iwt, ik2, kvb, bias)


_SSD_L = 256
_PAIRS = SSM_HEADS // 2
_CARRY = SUBLANES


def _ssd_kernel(z_ref, xbc_ref, sm_ref, dtt_ref, cw_ref, cb_ref, dtb_ref, dtbt_ref, al_ref, alt_ref,
                dsk_ref, nw_ref, o_ref, ext_ref, state_ref, y_ref):
    L = _SSD_L
    hd = SSM_HEADDIM

    @pl.when(pl.program_id(1) == 0)
    def _():
        ext_ref[0:_CARRY, :] = jnp.zeros((_CARRY, SSM_CONV_DIM), F32)
        state_ref[...] = jnp.zeros(state_ref.shape, F32)

    x = xbc_ref[...]
    ext_ref[_CARRY:_CARRY + L, :] = x
    w = cw_ref[...]
    conv = x * w[3:4] + cb_ref[...]
    for k in range(1, SSM_CONV):
        conv = conv + ext_ref[_CARRY - k:_CARRY - k + L, :] * w[SSM_CONV - 1 - k:SSM_CONV - k]
    ext_ref[0:_CARRY, :] = x[L - _CARRY:L]
    act = conv * jax.nn.sigmoid(conv)
    xs = act[:, :SSM_D_INNER]
    boff = SSM_D_INNER
    coff = SSM_D_INNER + SSM_GROUPS * SSM_STATE

    def softplus(v):
        return jnp.maximum(v, 0.0) + jnp.log1p(jnp.exp(-jnp.abs(v)))

    dt = softplus(sm_ref[:, _SM_DT:_SM_DT + SSM_HEADS] + dtb_ref[...])
    dtt = softplus(dtt_ref[0] + dtbt_ref[...])
    a_col = dt * (-jnp.exp(al_ref[...]))
    a_row = dtt * (-jnp.exp(alt_ref[...]))
    ri = lax.broadcasted_iota(I32, (L, L), 0)
    ci = lax.broadcasted_iota(I32, (L, L), 1)
    causal = ci <= ri
    cs_col = jnp.dot(jnp.where(causal, 1.0, 0.0), a_col, precision=HI, preferred_element_type=F32)
    cs_row = jnp.dot(a_row, jnp.where(ri <= ci, 1.0, 0.0), precision=HI, preferred_element_type=F32)
    cs_last = cs_col[L - 1:L, :]
    lane = lax.broadcasted_iota(I32, (1, LANES), 1)
    lo = lane < hd
    sub = lax.broadcasted_iota(I32, (LANES, 1), 0)

    for g in range(SSM_GROUPS):
        bm = act[:, boff + g * SSM_STATE: boff + (g + 1) * SSM_STATE].astype(BF16)
        cm = act[:, coff + g * SSM_STATE: coff + (g + 1) * SSM_STATE].astype(BF16)
        cb = lax.dot_general(cm, bm, (((1,), (1,)), ((), ())), preferred_element_type=F32)
        for pp in range(_PAIRS // SSM_GROUPS):
            p = g * (_PAIRS // SSM_GROUPS) + pp
            h0, h1 = 2 * p, 2 * p + 1
            xp = xs[:, p * LANES:(p + 1) * LANES]
            dtl = jnp.where(lo, dt[:, h0:h0 + 1], dt[:, h1:h1 + 1])
            xdt = xp * dtl
            csl = jnp.where(lo, cs_col[:, h0:h0 + 1], cs_col[:, h1:h1 + 1])
            last = jnp.where(lo, cs_last[:, h0:h0 + 1], cs_last[:, h1:h1 + 1])
            ydiag = jnp.zeros((L, LANES), F32)
            for h, msk in ((h0, lo), (h1, jnp.logical_not(lo))):
                seg = cs_col[:, h:h + 1] - cs_row[h:h + 1, :]
                gm = (cb * jnp.exp(jnp.where(causal, seg, -jnp.inf))).astype(BF16)
                ydiag = ydiag + jnp.dot(gm, jnp.where(msk, xdt, 0.0).astype(BF16), preferred_element_type=F32)
            prev = state_ref[p]
            yoff = lax.dot_general(cm, prev.astype(BF16), (((1,), (1,)), ((), ())), preferred_element_type=F32)
            y_ref[:, p * LANES:(p + 1) * LANES] = ydiag + yoff * jnp.exp(csl) + xp * dsk_ref[:, p * LANES:(p + 1) * LANES]
            wx = (xdt * jnp.exp(last - csl)).astype(BF16)
            st = lax.dot_general(wx, bm, (((0,), (0,)), ((), ())), preferred_element_type=F32)
            cdec = jnp.where(sub < hd, jnp.exp(cs_last[:, h0:h0 + 1]), jnp.exp(cs_last[:, h1:h1 + 1]))
            state_ref[p] = prev * cdec + st

    z = z_ref[...]
    y = y_ref[...] * (z * jax.nn.sigmoid(z))
    gw = SSM_D_INNER // SSM_GROUPS
    for g in range(SSM_GROUPS):
        yg = y[:, g * gw:(g + 1) * gw]
        yg = yg * lax.rsqrt(jnp.mean(yg * yg, axis=-1, keepdims=True) + LN_EPS)
        o_ref[:, g * gw:(g + 1) * gw] = (yg * nw_ref[:, g * gw:(g + 1) * gw]).astype(o_ref.dtype)


def _ssd_call(z, xbc, small, smt, conv_w, conv_b, dt_bias, a_log, d_skip, norm_w, bsz, seq):
    L = _SSD_L
    nc = seq // L
    row = lambda b, c: (b * nc + c, 0)
    h = SSM_HEADS
    return pl.pallas_call(
        _ssd_kernel,
        grid=(bsz, nc),
        in_specs=[pl.BlockSpec((L, SSM_D_INNER), row),
                  pl.BlockSpec((L, SSM_CONV_DIM), row),
                  pl.BlockSpec((L, LANES), row),
                  pl.BlockSpec((1, h, L), lambda b, c: (b, _SM_DT // h, c)),
                  _const_spec((SSM_CONV, SSM_CONV_DIM)), _const_spec((1, SSM_CONV_DIM)),
                  _const_spec((1, h)), _const_spec((h, 1)), _const_spec((1, h)), _const_spec((h, 1)),
                  _const_spec((1, SSM_D_INNER)), _const_spec((1, SSM_D_INNER))],
        out_specs=pl.BlockSpec((L, SSM_D_INNER), row),
        out_shape=jax.ShapeDtypeStruct((bsz * seq, SSM_D_INNER), BF16),
        scratch_shapes=[pltpu.VMEM((L + _CARRY, SSM_CONV_DIM), F32),
                        pltpu.VMEM((_PAIRS, LANES, SSM_STATE), F32),
                        pltpu.VMEM((L, SSM_D_INNER), F32)],
        compiler_params=_cparams(("arbitrary", "arbitrary")),
        name="ssd",
    )(z, xbc, small, smt, conv_w, conv_b.reshape(1, -1), dt_bias.reshape(1, h), dt_bias.reshape(h, 1),
      a_log.reshape(1, h), a_log.reshape(h, 1), jnp.repeat(d_skip, SSM_HEADDIM).reshape(1, -1),
      norm_w.reshape(1, -1))


_TM = 512
_ROUTE_ROWS = 2 * TOPK_EXPERTS


def _merge_kernel(oa_ref, ob_ref, ga_ref, gb_ref, x_ref, mod_ref, wpa_ref, wpb_ref, wo_ref, g1_ref, b1_ref,
                  wrh_ref, wrl_ref, br_ref, x1_ref, u2_ref, route_ref, gate_ref, cnt_ref, base_ref, *, alpha):
    tm = _TM
    ne = N_EXPERTS

    @pl.when(pl.program_id(0) == 0)
    def _():
        base_ref[...] = jnp.zeros(base_ref.shape, F32)

    ma = jnp.dot(oa_ref[...], wpa_ref[...], preferred_element_type=F32)
    mb = jnp.dot(ob_ref[...], wpb_ref[...], preferred_element_type=F32)
    merged = jax.nn.sigmoid(ga_ref[...]) * ma + jax.nn.sigmoid(gb_ref[...]) * mb
    t = jnp.dot(merged.astype(BF16), wo_ref[...], preferred_element_type=F32)
    x1 = _ln(alpha * x_ref[...] + mod_ref[0, 2:3, :] * t) * g1_ref[...] + b1_ref[...]
    x1_ref[...] = x1
    u2 = _ln(x1) * (1.0 + mod_ref[0, 4:5, :]) + mod_ref[0, 3:4, :]
    u2_ref[...] = u2
    nt = (((1,), (1,)), ((), ()))
    uh = u2.astype(BF16)
    ul = (u2 - uh.astype(F32)).astype(BF16)
    wh, wl = wrh_ref[...], wrl_ref[...]
    logits = (lax.dot_general(wh, uh, nt, preferred_element_type=F32)
              + lax.dot_general(wl, uh, nt, preferred_element_type=F32)
              + lax.dot_general(wh, ul, nt, preferred_element_type=F32)) + br_ref[...]
    eio = lax.broadcasted_iota(I32, (ne, tm), 0).astype(F32)
    vals, ids = [], []
    for _ in range(TOPK_EXPERTS):
        m = jnp.max(logits, axis=0, keepdims=True)
        idx = jnp.min(jnp.where(logits == m, eio, float(ne)), axis=0, keepdims=True)
        vals.append(m)
        ids.append(idx)
        logits = jnp.where(eio == idx, -jnp.inf, logits)
    es = [jnp.exp(v - vals[0]) for v in vals]
    den = es[0] + es[1] + es[2] + es[3]

    onehot = jnp.zeros((ne, tm), F32)
    for idx in ids:
        onehot = onehot + jnp.where(eio == idx, 1.0, 0.0)
    ri = lax.broadcasted_iota(I32, (tm, tm), 0)
    ci = lax.broadcasted_iota(I32, (tm, tm), 1)
    before = jnp.where(ri < ci, 1.0, 0.0).astype(BF16)
    base = base_ref[...]
    prefix = jnp.dot(onehot.astype(BF16), before, preferred_element_type=F32) + base
    sub = lax.broadcasted_iota(I32, (_ROUTE_ROWS, tm), 0)
    route = jnp.zeros((_ROUTE_ROWS, tm), F32)
    gates = jnp.zeros((_ROUTE_ROWS, tm), F32)
    for j in range(TOPK_EXPERTS):
        rank = jnp.sum(jnp.where(eio == ids[j], prefix, 0.0), axis=0, keepdims=True)
        route = jnp.where(sub == j, ids[j], route)
        route = jnp.where(sub == TOPK_EXPERTS + j, rank, route)
        gates = jnp.where(sub == j, es[j] / den, gates)
    route_ref[...] = route.astype(I32)
    gate_ref[...] = gates
    base = base + jnp.sum(onehot, axis=1, keepdims=True)
    base_ref[...] = base
    cnt_ref[...] = jnp.broadcast_to(base, cnt_ref.shape)


def _merge_call(o_a, o_b, g_a, g_b, x2, mod3, wpa, wpb, wo, ln_g, ln_b, w_router, b_router, seq, alpha):
    n_tok, d = x2.shape
    tm = _TM
    ne = N_EXPERTS
    row = lambda i: (i, 0)
    blk = pl.BlockSpec((tm, d), row)
    sm = pl.BlockSpec((_ROUTE_ROWS, tm), lambda i: (0, i))
    wrt = w_router.T
    wrh = wrt.astype(BF16)
    wrl = (wrt - wrh.astype(F32)).astype(BF16)
    return pl.pallas_call(
        functools.partial(_merge_kernel, alpha=alpha),
        grid=(n_tok // tm,),
        in_specs=[blk, blk, blk, blk, blk,
                  pl.BlockSpec((1, 6, d), lambda i: ((i * tm) // seq, 0, 0)),
                  _const_spec((d, d)), _const_spec((d, d)), _const_spec((d, d)),
                  _const_spec((1, d)), _const_spec((1, d)), _const_spec((ne, d)), _const_spec((ne, d)),
                  _const_spec((ne, 1))],
        out_specs=[blk, blk, sm, sm, pl.BlockSpec((ne, LANES), lambda i: (0, 0))],
        out_shape=[jax.ShapeDtypeStruct((n_tok, d), F32), jax.ShapeDtypeStruct((n_tok, d), F32),
                   jax.ShapeDtypeStruct((_ROUTE_ROWS, n_tok), I32), jax.ShapeDtypeStruct((_ROUTE_ROWS, n_tok), F32),
                   jax.ShapeDtypeStruct((ne, LANES), F32)],
        scratch_shapes=[pltpu.VMEM((ne, 1), F32)],
        compiler_params=_cparams(("arbitrary",)),
        name="merge",
    )(o_a, o_b, g_a, g_b, x2, mod3, wpa, wpb, wo, ln_g, ln_b, wrh, wrl, b_router.reshape(ne, 1))


_TD = 4096
_TMB = 512


def _dispatch_kernel(pend_ref, dest_ref, u2_ref, xs_ref, zero_ref, sem):
    @pl.when(pl.program_id(0) == 0)
    def _():
        zero_ref[...] = jnp.zeros(zero_ref.shape, F32)
        for e in range(N_EXPERTS):
            end = pend_ref[e]
            start = pend_ref[e - 1] if e else 0

            @pl.when(end > start)
            def _():
                dst = xs_ref.at[pl.ds(pl.multiple_of(end - _TMB, _TMB), _TMB), :]
                cp = pltpu.make_async_copy(zero_ref, dst, sem)
                cp.start()
                cp.wait()

        def zero_unused(b, carry):
            cp = pltpu.make_async_copy(zero_ref, xs_ref.at[pl.ds(pl.multiple_of(b * _TMB, _TMB), _TMB), :], sem)
            cp.start()
            cp.wait()
            return carry

        lax.fori_loop(pend_ref[N_EXPERTS - 1] // _TMB, xs_ref.shape[0] // _TMB, zero_unused, 0)

    def issue(tb, carry):
        row0 = pl.multiple_of(tb * SUBLANES, SUBLANES)
        for r in range(SUBLANES):
            for j in range(TOPK_EXPERTS):
                d = dest_ref[(row0 + r) * TOPK_EXPERTS + j]
                pltpu.make_async_copy(u2_ref.at[pl.ds(row0 + r, 1), :], xs_ref.at[pl.ds(d, 1), :], sem).start()
        return carry

    lax.fori_loop(0, _TD // SUBLANES, issue, 0)
    for _ in range(TOPK_EXPERTS):
        pltpu.make_async_copy(u2_ref, xs_ref.at[pl.ds(0, _TD), :], sem).wait()


def _dispatch_call(pends, dest_flat, u2, n_slots):
    n_tok, d = u2.shape
    grid_spec = pltpu.PrefetchScalarGridSpec(
        num_scalar_prefetch=1,
        grid=(n_tok // _TD,),
        in_specs=[pl.BlockSpec((_TD * TOPK_EXPERTS,), lambda i, pe: (i,), memory_space=pltpu.SMEM),
                  pl.BlockSpec((_TD, d), lambda i, pe: (i, 0))],
        out_specs=pl.BlockSpec(memory_space=pl.ANY),
        scratch_shapes=[pltpu.VMEM((_TMB, d), F32), pltpu.SemaphoreType.DMA(())],
    )
    return pl.pallas_call(
        _dispatch_kernel,
        grid_spec=grid_spec,
        out_shape=jax.ShapeDtypeStruct((n_slots, d), F32),
        compiler_params=_cparams(("arbitrary",)),
        name="dispatch",
    )(pends, dest_flat, u2)


def _expert_kernel(be_ref, nu_ref, xs_ref, w1_ref, b1_ref, w2_ref, b2_ref, y_ref, w1b_ref, w2b_ref):
    i = pl.program_id(0)
    f = w2_ref.shape[1]
    used = i < nu_ref[0]
    new_expert = jnp.logical_or(i == 0, be_ref[i] != be_ref[jnp.maximum(i - 1, 0)])

    @pl.when(jnp.logical_and(used, new_expert))
    def _():
        w1b_ref[...] = w1_ref[0].astype(BF16)
        w2b_ref[...] = w2_ref[0].astype(BF16)

    @pl.when(used)
    def _():
        h = jnp.dot(xs_ref[...].astype(BF16), w1b_ref[...], preferred_element_type=F32) + b1_ref[0]
        gate = jnp.minimum(h[:, :f], SWIGLU_LIMIT)
        up = jnp.clip(h[:, f:], -SWIGLU_LIMIT, SWIGLU_LIMIT)
        act = (up + 1.0) * gate * jax.nn.sigmoid(SWIGLU_ALPHA * gate)
        y_ref[...] = jnp.dot(act.astype(BF16), w2b_ref[...], preferred_element_type=F32) + b2_ref[0]

    @pl.when(i >= nu_ref[0])
    def _():
        y_ref[...] = jnp.zeros(y_ref.shape, F32)


def _expert_call(block_expert, n_used, xs, w1, b1, w2, b2):
    n_slots, d = xs.shape
    ne, _, f2 = w1.shape
    f = f2 // 2
    grid_spec = pltpu.PrefetchScalarGridSpec(
        num_scalar_prefetch=2,
        grid=(n_slots // _TMB,),
        in_specs=[pl.BlockSpec((_TMB, d), lambda i, be, nu: (i, 0)),
                  pl.BlockSpec((1, d, f2), lambda i, be, nu: (be[i], 0, 0)),
                  pl.BlockSpec((1, 1, f2), lambda i, be, nu: (be[i], 0, 0)),
                  pl.BlockSpec((1, f, d), lambda i, be, nu: (be[i], 0, 0)),
                  pl.BlockSpec((1, 1, d), lambda i, be, nu: (be[i], 0, 0))],
        out_specs=pl.BlockSpec((_TMB, d), lambda i, be, nu: (i, 0)),
        scratch_shapes=[pltpu.VMEM((d, f2), BF16), pltpu.VMEM((f, d), BF16)],
    )
    return pl.pallas_call(
        _expert_kernel,
        grid_spec=grid_spec,
        out_shape=jax.ShapeDtypeStruct((n_slots, d), F32),
        compiler_params=_cparams(("arbitrary",)),
        name="experts",
    )(block_expert, n_used, xs, w1, b1.reshape(ne, 1, f2), w2, b2.reshape(ne, 1, d))


_TC = 1024


def _combine_kernel(dest_ref, gate_ref, x1_ref, mod_ref, g2_ref, b2_ref, y_hbm, o_ref, buf_ref, sem, *, alpha):
    def issue(tb, carry):
        row0 = pl.multiple_of(tb * SUBLANES, SUBLANES)
        for r in range(SUBLANES):
            for j in range(TOPK_EXPERTS):
                d = dest_ref[(row0 + r) * TOPK_EXPERTS + j]
                pltpu.make_async_copy(y_hbm.at[pl.ds(d, 1), :], buf_ref.at[j, pl.ds(row0 + r, 1), :], sem).start()
        return carry

    lax.fori_loop(0, _TC // SUBLANES, issue, 0)
    for j in range(TOPK_EXPERTS):
        pltpu.make_async_copy(y_hbm.at[pl.ds(0, _TC), :], buf_ref.at[j], sem).wait()
    gates = gate_ref[...]
    y = gates[:, 0:1] * buf_ref[0]
    for j in range(1, TOPK_EXPERTS):
        y = y + gates[:, j:j + 1] * buf_ref[j]
    o_ref[...] = _ln(alpha * x1_ref[...] + mod_ref[0, 5:6, :] * y) * g2_ref[...] + b2_ref[...]


def _combine_call(dest_flat, gates, x1, mod3, ln_g, ln_b, y, seq, alpha):
    n_tok, d = x1.shape
    tc = _TC
    row = lambda i: (i, 0)
    return pl.pallas_call(
        functools.partial(_combine_kernel, alpha=alpha),
        grid=(n_tok // tc,),
        in_specs=[pl.BlockSpec((tc * TOPK_EXPERTS,), lambda i: (i,), memory_space=pltpu.SMEM),
                  pl.BlockSpec((tc, TOPK_EXPERTS), row),
                  pl.BlockSpec((tc, d), row),
                  pl.BlockSpec((1, 6, d), lambda i: ((i * tc) // seq, 0, 0)),
                  _const_spec((1, d)), _const_spec((1, d)),
                  pl.BlockSpec(memory_space=pl.ANY)],
        out_specs=pl.BlockSpec((tc, d), row),
        out_shape=jax.ShapeDtypeStruct((n_tok, d), F32),
        scratch_shapes=[pltpu.VMEM((TOPK_EXPERTS, tc, d), F32), pltpu.SemaphoreType.DMA(())],
        compiler_params=_cparams(("arbitrary",)),
        name="combine",
    )(dest_flat, gates, x1, mod3, ln_g, ln_b, y)


def _permute_w_in(w):
    d = w.shape[0]
    s = np.cumsum([0, A_HEADS * A_LATENT, A_LATENT, IDX_HEADS * IDX_DIM, IDX_DIM, IDX_HEADS,
                   SSM_D_INNER, SSM_CONV_DIM, SSM_HEADS, d, d]).tolist()
    q, kv, iq, ik, iw, z, xbc, dt, ga, gb = [w[:, s[i]:s[i + 1]] for i in range(10)]
    pad1 = jnp.zeros((d, _SM_DT - _SM_IW - IDX_HEADS), w.dtype)
    pad2 = jnp.zeros((d, LANES - _SM_DT - SSM_HEADS), w.dtype)
    return jnp.concatenate([q, kv, iq, ik, iw, pad1, dt, pad2, z, xbc, ga, gb], axis=1).astype(BF16)


def _pad_lanes(v, fill=0.0):
    return jnp.pad(v.reshape(1, -1), ((0, 0), (0, LANES - v.shape[-1])), constant_values=fill)


def kernel(x, c, w_mod, b_mod, w_in, kv_norm_w, idx_k_norm_w, idx_k_norm_b, rel_bias, conv_w, conv_b, dt_bias,
           a_log, d_skip, ssm_norm_w, w_proj_a, w_proj_b, w_out, ln1_g, ln1_b, w_router, b_router, w1, b1, w2, b2,
           ln2_g, ln2_b):
    bsz, seq, d = x.shape
    depth = w_mod.shape[0]
    alpha = (2.0 * depth) ** 0.25
    n_tok = bsz * seq
    assert d == D_MODEL, d
    assert seq % max(_TQ, _KB, _SSD_L, _TI, _TM, _TC) == 0 and n_tok % _TD == 0, (bsz, seq)
    n_asg = n_tok * TOPK_EXPERTS
    n_blocks = n_asg // _TMB + N_EXPERTS
    n_slots = n_blocks * _TMB
    nblk = seq // _KB
    bias = _bias_tiles(rel_bias)
    x2 = x.reshape(n_tok, d)
    for l in range(depth):
        mod3 = _mod_call(c, w_mod[l], b_mod[l]).reshape(bsz, 6, d)
        q, kvn, iq, ik2, small, smt, z, xbc, g_a, g_b = _inproj_call(
            x2, mod3, _permute_w_in(w_in[l]), kv_norm_w[l].reshape(1, -1),
            _pad_lanes(idx_k_norm_w[l]), _pad_lanes(idx_k_norm_b[l]), seq)
        kvb = kvn.reshape(bsz, nblk, _KB, A_LATENT)
        o_a = _attn_call(q, iq, smt, ik2.reshape(2, bsz, nblk, _KB, LANES), kvb, bias, bsz, seq)
        o_b = _ssd_call(z, xbc, small, smt, conv_w[l], conv_b[l], dt_bias[l], a_log[l], d_skip[l], ssm_norm_w[l],
                        bsz, seq)
        x1, u2, route, gates, cnt = _merge_call(
            o_a, o_b, g_a, g_b, x2, mod3, w_proj_a[l].astype(BF16), w_proj_b[l].astype(BF16),
            w_out[l].astype(BF16), ln1_g[l].reshape(1, -1), ln1_b[l].reshape(1, -1), w_router[l], b_router[l],
            seq, alpha)
        counts = cnt[:, 0].astype(I32)
        padded = (counts + _TMB - 1) // _TMB * _TMB
        pends = jnp.cumsum(padded).astype(I32)
        pstarts = pends - padded
        eid = route[:TOPK_EXPERTS]
        onehot = eid[:, :, None] == jnp.arange(N_EXPERTS, dtype=I32)
        dest = jnp.sum(jnp.where(onehot, pstarts, 0), axis=-1) + route[TOPK_EXPERTS:2 * TOPK_EXPERTS]
        dest_flat = dest.T.reshape(n_asg)
        block_start = jnp.arange(n_blocks, dtype=I32) * _TMB
        block_expert = jnp.minimum(jnp.sum(block_start[:, None] >= pends[None, :], axis=1), N_EXPERTS - 1).astype(I32)
        n_used = (pends[-1:] // _TMB).astype(I32)
        xs = _dispatch_call(pends, dest_flat, u2, n_slots)
        y = _expert_call(block_expert, n_used, xs, w1[l], b1[l], w2[l], b2[l])
        x2 = _combine_call(dest_flat, gates[:TOPK_EXPERTS].T, x1, mod3, ln2_g[l].reshape(1, -1),
                           ln2_b[l].reshape(1, -1), y, seq, alpha)
    return x2.reshape(bsz, seq, d)
```
